```python
import jax, jax.numpy as jnp
from jax import lax
import numpy as np

D_MODEL = 4096
BATCH = 4
SEQ = 4096
DEPTH = 1

HEAD_DIM = 64
MIX_WIDTH = D_MODEL
N_HEADS_TOTAL = MIX_WIDTH // HEAD_DIM
N_HEADS_SWA = N_HEADS_TOTAL // 2
N_KV_SWA = N_HEADS_SWA // 8
GQA_GROUP = N_HEADS_SWA // N_KV_SWA
N_HEADS_DIL = N_HEADS_TOTAL - N_HEADS_SWA
SWA_WINDOW = 128
DILATED_BRANCHES = ((128, 1), (512, 4), (2048, 16))
BLOCK = 128
N_EXPERTS = 32
TOP_K = 4
D_FF = D_MODEL // 2
EXPERT_BLOCK = 512
SWIGLU_LIMIT = 7.0
SWIGLU_ALPHA = 1.702
ALIBI_MAX_BIAS = 8.0
EPS = 1e-6

SWA_Q = N_HEADS_SWA * HEAD_DIM
SWA_KV = N_KV_SWA * HEAD_DIM
DIL_W = N_HEADS_DIL * HEAD_DIM
IN_COLS = SWA_Q + 2 * SWA_KV + 3 * DIL_W
SPLITS = (SWA_Q, SWA_Q + SWA_KV, SWA_Q + 2 * SWA_KV, SWA_Q + 2 * SWA_KV + DIL_W,
          SWA_Q + 2 * SWA_KV + 2 * DIL_W)

kernel_name = 'hybrid_swa_sink_dilated_moe'


def rms_norm(x, g):
    xf = x.astype(jnp.float32)
    y = xf * lax.rsqrt(jnp.mean(xf * xf, axis=-1, keepdims=True) + EPS)
    return (y * g.astype(jnp.float32)).astype(x.dtype)


def alibi_slopes(n):
    return 2.0 ** (-ALIBI_MAX_BIAS * (jnp.arange(n, dtype=jnp.float32) + 1.0) / n)


def banded_attention(q, k, v, max_diff, dist_scale, slopes):
    n, L, hkv, g, hd = q.shape
    nb = L // BLOCK
    qb = q.reshape(n, nb, BLOCK, hkv, g, hd)

    def band(t):
        tb = t.reshape(n, nb, BLOCK, hkv, hd)
        prev = jnp.pad(tb, ((0, 0), (1, 0), (0, 0), (0, 0), (0, 0)))[:, :-1]
        return jnp.concatenate([prev, tb], axis=2)

    kb, vb = band(k), band(v)
    s = jnp.einsum('nbqhgd,nbkhd->nbhgqk', qb, kb,
                   preferred_element_type=jnp.float32) * (HEAD_DIM ** -0.5)
    qi = jnp.arange(BLOCK)[:, None]
    kj = jnp.arange(2 * BLOCK)[None, :]
    diff = qi + BLOCK - kj
    blk = jnp.arange(nb)[:, None, None]
    valid = (diff >= 0) & (diff <= max_diff) & ((blk > 0) | (kj >= BLOCK))
    dist = (diff * dist_scale).astype(jnp.float32)
    s = s - slopes.astype(jnp.float32)[:, :, None, None] * dist
    s = jnp.where(valid[None, :, None, None], s, -jnp.inf)
    m = jnp.max(s, axis=-1, keepdims=True)
    p = jnp.exp(s - m)
    l = jnp.sum(p, axis=-1, keepdims=True)
    o = jnp.einsum('nbhgqk,nbkhd->nbqhgd', p, vb.astype(jnp.float32))
    o = o / l[..., 0].transpose(0, 1, 4, 2, 3)[..., None]
    o = o.reshape(n, L, hkv, g, hd)
    lse = (m + jnp.log(l))[..., 0].transpose(0, 1, 4, 2, 3).reshape(n, L, hkv, g)
    return o, lse


def dilated_branch(q, k, v, window, dilation, slopes):
    b, s, h, hd = q.shape
    seg = dilation * BLOCK
    sp = -(-s // seg) * seg
    L = sp // dilation

    def split(t):
        t = jnp.pad(t, ((0, 0), (0, sp - s), (0, 0), (0, 0)))
        return t.reshape(b, L, dilation, h, hd).transpose(0, 2, 1, 3, 4).reshape(b * dilation, L, h, hd)

    qs, ks, vs = split(q), split(k), split(v)
    o, lse = banded_attention(qs[:, :, :, None], ks, vs, window // dilation, dilation, slopes[:, None])
    o = o[:, :, :, 0].reshape(b, dilation, L, h, hd).transpose(0, 2, 1, 3, 4).reshape(b, sp, h, hd)[:, :s]
    lse = lse[..., 0].reshape(b, dilation, L, h).transpose(0, 2, 1, 3).reshape(b, sp, h)[:, :s]
    return o, lse


def moe_ffn(t, w_router, b_router, w_gate_up, b_gate_up, w_down, b_down):
    n_tok, d = t.shape
    n_assign = n_tok * TOP_K
    n_blocks = -(-n_assign // EXPERT_BLOCK) + N_EXPERTS
    n_rows = n_blocks * EXPERT_BLOCK

    logits = jnp.einsum('td,de->te', t, w_router, preferred_element_type=jnp.float32) \
        + b_router.astype(jnp.float32)
    top_v, top_i = lax.top_k(logits, TOP_K)
    gates = jax.nn.softmax(top_v, axis=-1)

    flat_e = top_i.reshape(-1)
    order = jnp.argsort(flat_e)
    sorted_e = flat_e[order]
    tok = (order // TOP_K).astype(jnp.int32)
    group_sizes = jnp.bincount(flat_e, length=N_EXPERTS).astype(jnp.int32)
    group_starts = jnp.cumsum(group_sizes) - group_sizes
    padded_sizes = -(-group_sizes // EXPERT_BLOCK) * EXPERT_BLOCK
    padded_ends = jnp.cumsum(padded_sizes)
    padded_starts = padded_ends - padded_sizes
    rank = jnp.arange(n_assign, dtype=jnp.int32) - group_starts[sorted_e]
    dest = padded_starts[sorted_e] + rank

    row_tok = jnp.full((n_rows,), n_tok, jnp.int32).at[dest].set(tok)
    row_w = jnp.zeros((n_rows,), jnp.float32).at[dest].set(gates.reshape(-1)[order])
    block_start = jnp.arange(n_blocks, dtype=jnp.int32) * EXPERT_BLOCK
    block_e = jnp.minimum(jnp.searchsorted(padded_ends, block_start, side='right'),
                          N_EXPERTS - 1).astype(jnp.int32)

    t_pad = jnp.concatenate([t, jnp.zeros((1, d), t.dtype)], axis=0)
    xs = t_pad[row_tok].reshape(n_blocks, EXPERT_BLOCK, d)

    def expert_block(args):
        xb, e = args
        gu = xb @ w_gate_up[e] + b_gate_up[e]
        gate, up = gu[:, ::2], gu[:, 1::2]
        gate = jnp.minimum(gate, SWIGLU_LIMIT)
        up = jnp.clip(up, -SWIGLU_LIMIT, SWIGLU_LIMIT)
        act = (up + 1.0) * gate * jax.nn.sigmoid(SWIGLU_ALPHA * gate)
        return act @ w_down[e] + b_down[e]

    out = lax.map(expert_block, (xs, block_e)).reshape(n_rows, d)
    y = jnp.zeros((n_tok + 1, d), t.dtype).at[row_tok].add(out * row_w.astype(out.dtype)[:, None])
    return y[:n_tok]


def setup_inputs(seed: int = 0) -> dict:
    key = jax.random.key(seed)
    ks = jax.random.split(key, 16)
    f32 = jnp.float32
    nrm = lambda k, shape, scale: jax.random.normal(k, shape, f32) * scale
    return {
        'x': nrm(ks[0], (BATCH, SEQ, D_MODEL), 1.0),
        'norm1_g': 1.0 + nrm(ks[1], (DEPTH, D_MODEL), 0.02),
        'w_in': nrm(ks[2], (DEPTH, D_MODEL, IN_COLS), D_MODEL ** -0.5),
        'q_norm_swa': 1.0 + nrm(ks[3], (DEPTH, HEAD_DIM), 0.02),
        'k_norm_swa': 1.0 + nrm(ks[4], (DEPTH, HEAD_DIM), 0.02),
        'q_norm_dil': 1.0 + nrm(ks[5], (DEPTH, HEAD_DIM), 0.02),
        'k_norm_dil': 1.0 + nrm(ks[6], (DEPTH, HEAD_DIM), 0.02),
        'sinks': nrm(ks[7], (DEPTH, N_HEADS_SWA), 1.0),
        'w_out': nrm(ks[8], (DEPTH, MIX_WIDTH, D_MODEL), MIX_WIDTH ** -0.5),
        'norm2_g': 1.0 + nrm(ks[9], (DEPTH, D_MODEL), 0.02),
        'w_router': nrm(ks[10], (DEPTH, D_MODEL, N_EXPERTS), D_MODEL ** -0.5),
        'b_router': nrm(ks[11], (DEPTH, N_EXPERTS), 0.01),
        'w_gate_up': nrm(ks[12], (DEPTH, N_EXPERTS, D_MODEL, 2 * D_FF), D_MODEL ** -0.5),
        'b_gate_up': nrm(ks[13], (DEPTH, N_EXPERTS, 2 * D_FF), 0.02),
        'w_down': nrm(ks[14], (DEPTH, N_EXPERTS, D_FF, D_MODEL), D_FF ** -0.5),
        'b_down': nrm(ks[15], (DEPTH, N_EXPERTS, D_MODEL), 0.02),
    }


def reference(x, norm1_g, w_in, q_norm_swa, k_norm_swa, q_norm_dil, k_norm_dil, sinks,
              w_out, norm2_g, w_router, b_router, w_gate_up, b_gate_up, w_down, b_down):
    b, s, d = x.shape
    slopes_swa = alibi_slopes(N_HEADS_SWA).reshape(N_KV_SWA, GQA_GROUP)
    slopes_dil = alibi_slopes(N_HEADS_DIL)
    for l in range(DEPTH):
        h = rms_norm(x, norm1_g[l])
        proj = jnp.einsum('bsd,dc->bsc', h, w_in[l])
        q_a, k_a, v_a, q_b, k_b, v_b = jnp.split(proj, SPLITS, axis=-1)

        q_a = rms_norm(q_a.reshape(b, s, N_KV_SWA, GQA_GROUP, HEAD_DIM), q_norm_swa[l])
        k_a = rms_norm(k_a.reshape(b, s, N_KV_SWA, HEAD_DIM), k_norm_swa[l])
        v_a = v_a.reshape(b, s, N_KV_SWA, HEAD_DIM)
        o_a, lse_a = banded_attention(q_a, k_a, v_a, SWA_WINDOW - 1, 1, slopes_swa)
        sink = sinks[l].astype(jnp.float32).reshape(N_KV_SWA, GQA_GROUP)
        o_a = o_a * jax.nn.sigmoid(lse_a - sink)[..., None]
        o_a = o_a.reshape(b, s, SWA_Q).astype(x.dtype)

        q_b = rms_norm(q_b.reshape(b, s, N_HEADS_DIL, HEAD_DIM), q_norm_dil[l])
        k_b = rms_norm(k_b.reshape(b, s, N_HEADS_DIL, HEAD_DIM), k_norm_dil[l])
        v_b = v_b.reshape(b, s, N_HEADS_DIL, HEAD_DIM)
        outs, lses = [], []
        for window, dilation in DILATED_BRANCHES:
            o_i, lse_i = dilated_branch(q_b, k_b, v_b, window, dilation, slopes_dil)
            outs.append(o_i)
            lses.append(lse_i)
        wts = jax.nn.softmax(jnp.stack(lses, axis=0), axis=0)
        o_b = jnp.sum(wts[..., None] * jnp.stack(outs, axis=0), axis=0)
        o_b = o_b.reshape(b, s, DIL_W).astype(x.dtype)

        mixed = jnp.concatenate([o_a, o_b], axis=-1)
        x = x + jnp.einsum('bsc,cd->bsd', mixed, w_out[l])

        h2 = rms_norm(x, norm2_g[l]).reshape(b * s, d)
        y = moe_ffn(h2, w_router[l], b_router[l], w_gate_up[l], b_gate_up[l], w_down[l], b_down[l])
        x = x + y.reshape(b, s, d)
    return x
```

```python
import functools

import jax
import jax.numpy as jnp
import numpy as np
from jax import lax
from jax.experimental import pallas as pl
from jax.experimental.pallas import tpu as pltpu

F32 = jnp.float32
BF16 = jnp.bfloat16

HEAD_DIM = 64
LANES = 128
ATTN_BLOCK = 128
GQA_GROUP = 8
SWA_WINDOW = 128
DILATED_BRANCHES = ((128, 1), (512, 4), (2048, 16))
N_EXPERTS = 32
TOP_K = 4
SWIGLU_LIMIT = 7.0
SWIGLU_ALPHA = 1.702
ALIBI_MAX_BIAS = 8.0
EPS = 1e-6
MASK_DIST = 1e30
NEG_BIG = -1e30

VMEM_LIMIT = 52 * 1024 * 1024

PROJ_TN = 512
ATTN_W = 512
EXPERT_TM = 512
GATHER_ROWS = 256
COMBINE_TM = 128


def _cparams(n_axes):
    return pltpu.CompilerParams(dimension_semantics=("arbitrary",) * n_axes,
                                vmem_limit_bytes=VMEM_LIMIT)


def _inproj_kernel(flag_ref, x_ref, g_ref, w_ref, gain_ref, ones_ref, o_ref, h_scr):
    j = pl.program_id(1)

    @pl.when(j == 0)
    def _():
        x = x_ref[...]
        ms = jnp.mean(x * x, axis=-1, keepdims=True)
        h_scr[...] = (x * lax.rsqrt(ms + EPS) * g_ref[...]).astype(BF16)

    acc = jnp.dot(h_scr[...], w_ref[...], preferred_element_type=F32)

    @pl.when(flag_ref[j] == 1)
    def _():
        ss = jnp.dot((acc * acc).astype(BF16), ones_ref[...], preferred_element_type=F32)
        o_ref[...] = (acc * lax.rsqrt(ss * (1.0 / HEAD_DIM) + EPS) * gain_ref[...]).astype(BF16)

    @pl.when(flag_ref[j] == 0)
    def _():
        o_ref[...] = acc.astype(BF16)


def _inproj(x2d, g1, w, gains, flags, tm):
    t, d = x2d.shape
    c = w.shape[1]
    tn = PROJ_TN
    head_id = np.arange(tn) // HEAD_DIM
    ones_bd = jnp.asarray(head_id[:, None] == head_id[None, :], dtype=BF16)
    grid_spec = pltpu.PrefetchScalarGridSpec(
        num_scalar_prefetch=1,
        grid=(t // tm, c // tn),
        in_specs=[
            pl.BlockSpec((tm, d), lambda i, j, f: (i, 0)),
            pl.BlockSpec((1, d), lambda i, j, f: (0, 0)),
            pl.BlockSpec((d, tn), lambda i, j, f: (0, j)),
            pl.BlockSpec((1, tn), lambda i, j, f: (0, j)),
            pl.BlockSpec((tn, tn), lambda i, j, f: (0, 0)),
        ],
        out_specs=pl.BlockSpec((tm, tn), lambda i, j, f: (i, j)),
        scratch_shapes=[pltpu.VMEM((tm, d), BF16)],
    )
    return pl.pallas_call(
        _inproj_kernel, grid_spec=grid_spec,
        out_shape=jax.ShapeDtypeStruct((t, c), BF16),
        compiler_params=_cparams(2), name="inproj",
    )(flags, x2d, g1, w, gains, ones_bd)


def _attn_kernel(slope_ref, q_ref, kp_ref, kc_ref, vp_ref, vc_ref, *rest,
                 n_pairs, kv_shared, max_diff, dist_scale, with_sink):
    if with_sink:
        sink_ref, o_ref = rest
        lse_ref = None
    else:
        o_ref, lse_ref = rest
    blk = pl.program_id(2)
    cg = pl.program_id(3)

    qi = lax.broadcasted_iota(jnp.int32, (ATTN_BLOCK, 2 * ATTN_BLOCK), 0)
    kj = lax.broadcasted_iota(jnp.int32, (ATTN_BLOCK, 2 * ATTN_BLOCK), 1)
    dist = qi + ATTN_BLOCK - kj
    valid = (dist >= 0) & (dist <= max_diff) & ((blk > 0) | (kj >= ATTN_BLOCK))
    dist_m = jnp.where(valid, (dist * dist_scale).astype(F32), MASK_DIST)

    lane = lax.broadcasted_iota(jnp.int32, (ATTN_BLOCK, LANES), 1)
    low = lane < HEAD_DIM

    for p in range(n_pairs):
        cols = slice(p * LANES, (p + 1) * LANES)
        kcols = slice(0, LANES) if kv_shared else cols
        q2 = q_ref[:, cols]
        kk = jnp.concatenate([kp_ref[:, kcols], kc_ref[:, kcols]], axis=0)
        vv = jnp.concatenate([vp_ref[:, kcols], vc_ref[:, kcols]], axis=0)
        outs, lses = [], []
        for hh in range(2):
            slope = slope_ref[cg * (2 * n_pairs) + 2 * p + hh]
            qm = jnp.where(low if hh == 0 else ~low, q2, jnp.zeros_like(q2))
            s = lax.dot_general(qm, kk, (((1,), (1,)), ((), ())),
                                preferred_element_type=F32)
            s = s - slope * dist_m
            m = jnp.max(s, axis=-1, keepdims=True)
            e = jnp.exp(s - m)
            l = jnp.sum(e, axis=-1, keepdims=True)
            o = jnp.dot(e.astype(BF16), vv, preferred_element_type=F32)
            outs.append(o / l)
            lses.append(jnp.broadcast_to(m + jnp.log(l), (ATTN_BLOCK, LANES)))
        o2 = jnp.where(low, outs[0], outs[1])
        lse2 = jnp.where(low, lses[0], lses[1])
        if with_sink:
            o2 = o2 * jax.nn.sigmoid(lse2 - sink_ref[:, cols])
        else:
            lse_ref[:, cols] = lse2
        o_ref[:, cols] = o2.astype(o_ref.dtype)


def _band_attention(proj, slopes, *, batch, seq, dilation, q_col0, k_col0, v_col0, n_heads,
                    kv_shared, max_diff, sinks=None):
    t, c = proj.shape
    d = dilation
    lsub = seq // d
    nblk = lsub // ATTN_BLOCK
    w = ATTN_W
    n_pairs = w // LANES
    width = n_heads * HEAD_DIM
    ncg = width // w
    kw = LANES if kv_shared else w
    pv = proj.reshape(batch, lsub, d * c)

    def qmap(b, r, i, g, s):
        return (b, i, r * (c // w) + q_col0 // w + g)

    def kvmap(col0, prev):
        def f(b, r, i, g, s):
            blk = jnp.maximum(i - 1, 0) if prev else i
            return (b, blk, r * (c // kw) + col0 // kw + g)
        return f

    def omap(b, r, i, g, s):
        return (b, i, r * ncg + g)

    in_specs = [
        pl.BlockSpec((None, ATTN_BLOCK, w), qmap),
        pl.BlockSpec((None, ATTN_BLOCK, kw), kvmap(k_col0, True)),
        pl.BlockSpec((None, ATTN_BLOCK, kw), kvmap(k_col0, False)),
        pl.BlockSpec((None, ATTN_BLOCK, kw), kvmap(v_col0, True)),
        pl.BlockSpec((None, ATTN_BLOCK, kw), kvmap(v_col0, False)),
    ]
    args = [pv, pv, pv, pv, pv]
    with_sink = sinks is not None
    o_shape = jax.ShapeDtypeStruct((batch, lsub, d * width), BF16)
    o_spec = pl.BlockSpec((None, ATTN_BLOCK, w), omap)
    if with_sink:
        in_specs.append(pl.BlockSpec((1, w), lambda b, r, i, g, s: (0, g)))
        args.append(sinks)
        out_shape, out_specs = o_shape, o_spec
    else:
        out_shape = (o_shape, jax.ShapeDtypeStruct((batch, lsub, d * width), F32))
        out_specs = (o_spec, pl.BlockSpec((None, ATTN_BLOCK, w), omap))
    kern = functools.partial(_attn_kernel, n_pairs=n_pairs, kv_shared=kv_shared,
                             max_diff=max_diff, dist_scale=d, with_sink=with_sink)
    grid_spec = pltpu.PrefetchScalarGridSpec(
        num_scalar_prefetch=1, grid=(batch, d, nblk, ncg),
        in_specs=in_specs, out_specs=out_specs)
    res = pl.pallas_call(kern, grid_spec=grid_spec, out_shape=out_shape,
                         compiler_params=_cparams(4), name=f"band_attn_d{d}" + ("_sink" if with_sink else ""),
                         )(slopes, *args)
    if with_sink:
        return res.reshape(t, width)
    return res[0].reshape(t, width), res[1].reshape(t, width)


def _merge_kernel(o1, o2, o3, l1, l2, l3, out):
    a, b, c = l1[...], l2[...], l3[...]
    m = jnp.maximum(jnp.maximum(a, b), c)
    ea, eb, ec = jnp.exp(a - m), jnp.exp(b - m), jnp.exp(c - m)
    num = ea * o1[...].astype(F32) + eb * o2[...].astype(F32) + ec * o3[...].astype(F32)
    out[...] = (num / (ea + eb + ec)).astype(out.dtype)


def _merge(os_, ls_, tm=256):
    t, w = os_[0].shape
    spec = pl.BlockSpec((tm, w), lambda i: (i, 0))
    return pl.pallas_call(
        _merge_kernel, grid=(t // tm,), in_specs=[spec] * 6, out_specs=spec,
        out_shape=jax.ShapeDtypeStruct((t, w), BF16), compiler_params=_cparams(1), name="branch_merge",
    )(*os_, *ls_)


def _outproj_kernel(oa_ref, ob_ref, wt_ref, wb_ref, x_ref, o_ref):
    acc = jnp.dot(oa_ref[...], wt_ref[...], preferred_element_type=F32)
    acc = acc + jnp.dot(ob_ref[...], wb_ref[...], preferred_element_type=F32)
    o_ref[...] = x_ref[...] + acc


def _outproj(o_a, o_b, w_out, x2d, tm, tn):
    t, d = x2d.shape
    ha, hb = o_a.shape[1], o_b.shape[1]
    assert ha == hb
    return pl.pallas_call(
        _outproj_kernel, grid=(t // tm, d // tn),
        in_specs=[
            pl.BlockSpec((tm, ha), lambda i, j: (i, 0)),
            pl.BlockSpec((tm, hb), lambda i, j: (i, 0)),
            pl.BlockSpec((ha, tn), lambda i, j: (0, j)),
            pl.BlockSpec((hb, tn), lambda i, j: (1, j)),
            pl.BlockSpec((tm, tn), lambda i, j: (i, j)),
        ],
        out_specs=pl.BlockSpec((tm, tn), lambda i, j: (i, j)),
        out_shape=jax.ShapeDtypeStruct((t, d), F32), compiler_params=_cparams(2), name="outproj",
    )(o_a, o_b, w_out, w_out, x2d)


def _router_kernel(x_ref, g_ref, whi_ref, wlo_ref, b_ref, tri_ref,
                   h_ref, idx_ref, gate_ref, rank_ref, cnt_ref, carry):
    i = pl.program_id(0)

    @pl.when(i == 0)
    def _():
        carry[...] = jnp.zeros_like(carry)

    x = x_ref[...]
    ms = jnp.mean(x * x, axis=-1, keepdims=True)
    h = x * lax.rsqrt(ms + EPS) * g_ref[...]
    h_hi = h.astype(BF16)
    h_ref[...] = h_hi
    h_lo = (h - h_hi.astype(F32)).astype(BF16)
    logits = (jnp.dot(h_hi, whi_ref[...], preferred_element_type=F32)
              + jnp.dot(h_lo, whi_ref[...], preferred_element_type=F32)
              + jnp.dot(h_hi, wlo_ref[...], preferred_element_type=F32)) + b_ref[...]

    tm = x.shape[0]
    lane = lax.broadcasted_iota(jnp.int32, (tm, LANES), 1).astype(F32)
    work = logits
    multihot = jnp.zeros((tm, LANES), F32)
    vals, idxs = [], []
    for _ in range(TOP_K):
        m = jnp.max(work, axis=-1, keepdims=True)
        ik = jnp.min(jnp.where(work == m, lane, float(LANES)), axis=-1, keepdims=True)
        sel = lane == ik
        work = jnp.where(sel, -jnp.inf, work)
        multihot = jnp.where(sel, 1.0, multihot)
        vals.append(m)
        idxs.append(ik)
    es = [jnp.exp(v - vals[0]) for v in vals]
    denom = es[0] + es[1] + es[2] + es[3]

    cum = jnp.dot(tri_ref[...], multihot.astype(BF16), preferred_element_type=F32) + carry[0:1, :]
    idx_t = jnp.zeros((tm, LANES), F32)
    gate_t = jnp.zeros((tm, LANES), F32)
    rank_t = jnp.zeros((tm, LANES), F32)
    for k in range(TOP_K):
        rk = jnp.sum(jnp.where(lane == idxs[k], cum, 0.0), axis=-1, keepdims=True)
        here = lane == float(k)
        idx_t = jnp.where(here, idxs[k], idx_t)
        gate_t = jnp.where(here, es[k] / denom, gate_t)
        rank_t = jnp.where(here, rk, rank_t)
    idx_ref[...] = idx_t.astype(jnp.int32)
    gate_ref[...] = gate_t
    rank_ref[...] = rank_t.astype(jnp.int32)
    new_carry = carry[0:1, :] + jnp.sum(multihot, axis=0, keepdims=True)
    carry[...] = jnp.broadcast_to(new_carry, carry.shape)
    cnt_ref[...] = jnp.broadcast_to(new_carry, cnt_ref.shape)


def _router(x2d, g2, w_router, b_router, tm=256):
    t, d = x2d.shape
    ne = w_router.shape[1]
    w_pad = jnp.zeros((d, LANES), F32).at[:, :ne].set(w_router)
    w_hi = w_pad.astype(BF16)
    w_lo = (w_pad - w_hi.astype(F32)).astype(BF16)
    b_pad = jnp.full((1, LANES), NEG_BIG, F32).at[0, :ne].set(b_router)
    tri = jnp.asarray(np.tril(np.ones((tm, tm), np.float32), -1), dtype=BF16)
    tile = lambda dt: jax.ShapeDtypeStruct((t, LANES), dt)
    row = pl.BlockSpec((tm, LANES), lambda i: (i, 0))
    const = lambda shape: pl.BlockSpec(shape, lambda i: (0, 0))
    return pl.pallas_call(
        _router_kernel, grid=(t // tm,),
        in_specs=[pl.BlockSpec((tm, d), lambda i: (i, 0)), const((1, d)), const((d, LANES)),
                  const((d, LANES)), const((1, LANES)), const((tm, tm))],
        out_specs=(pl.BlockSpec((tm, d), lambda i: (i, 0)), row, row, row, const((8, LANES))),
        out_shape=(jax.ShapeDtypeStruct((t, d), BF16), tile(jnp.int32), tile(F32), tile(jnp.int32),
                   jax.ShapeDtypeStruct((8, LANES), F32)),
        scratch_shapes=[pltpu.VMEM((8, LANES), F32)],
        compiler_params=_cparams(1), name="router",
    )(x2d, g2, w_hi, w_lo, b_pad, tri)


def _index_copy(idx_hbm, idx_smem, sem, step, slot):
    return pltpu.make_async_copy(idx_hbm.at[step], idx_smem.at[slot], sem.at[slot])


def _stage_indices(idx_hbm, idx_smem, sem, step, n_steps):
    slot = lax.rem(step, 2)

    @pl.when(step == 0)
    def _():
        _index_copy(idx_hbm, idx_smem, sem, 0, 0).start()

    @pl.when(step + 1 < n_steps)
    def _():
        _index_copy(idx_hbm, idx_smem, sem, step + 1, 1 - slot).start()

    _index_copy(idx_hbm, idx_smem, sem, step, slot).wait()
    return slot


def _gather_kernel(tok_hbm, h_hbm, xs_hbm, tok_smem, isem, rsem):
    i = pl.program_id(0)
    n = pl.num_programs(0)
    slot = _stage_indices(tok_hbm, tok_smem, isem, i, n)
    base = i * GATHER_ROWS

    def row_copy(r):
        t = tok_smem[slot, r]
        return pltpu.make_async_copy(h_hbm.at[t], xs_hbm.at[base + r], rsem)

    def start(r, c):
        row_copy(r).start()
        return c
    lax.fori_loop(0, GATHER_ROWS, start, 0)

    def wait(r, c):
        row_copy(r).wait()
        return c
    lax.fori_loop(0, GATHER_ROWS, wait, 0)


def _gather_rows(h, row_tok):
    r_max = row_tok.shape[0]
    n_steps = r_max // GATHER_ROWS
    tok2d = row_tok.reshape(n_steps, GATHER_ROWS)
    return pl.pallas_call(
        _gather_kernel, grid=(n_steps,),
        in_specs=[pl.BlockSpec(memory_space=pl.ANY), pl.BlockSpec(memory_space=pl.ANY)],
        out_specs=pl.BlockSpec(memory_space=pl.ANY),
        scratch_shapes=[pltpu.SMEM((2, GATHER_ROWS), jnp.int32),
                        pltpu.SemaphoreType.DMA((2,)), pltpu.SemaphoreType.DMA(())],
        out_shape=jax.ShapeDtypeStruct((r_max,) + h.shape[1:], h.dtype),
        compiler_params=_cparams(1), name="row_gather",
    )(tok2d, h)


def _gateup_kernel(te_ref, na_ref, xs_ref, wg_ref, wu_ref, bg_ref, bu_ref, h_ref):
    i = pl.program_id(1)

    @pl.when(i < na_ref[0])
    def _():
        x = xs_ref[...]
        gate = jnp.dot(x, wg_ref[...], preferred_element_type=F32) + bg_ref[...]
        up = jnp.dot(x, wu_ref[...], preferred_element_type=F32) + bu_ref[...]
        gate = jnp.minimum(gate, SWIGLU_LIMIT)
        up = jnp.clip(up, -SWIGLU_LIMIT, SWIGLU_LIMIT)
        act = (up + 1.0) * gate * jax.nn.sigmoid(SWIGLU_ALPHA * gate)
        h_ref[...] = act.astype(h_ref.dtype)

    @pl.when(i >= na_ref[0])
    def _():
        h_ref[...] = jnp.zeros_like(h_ref)


def _down_kernel(te_ref, na_ref, h_ref, wd_ref, bd_ref, rw_ref, o_ref):
    i = pl.program_id(1)

    @pl.when(i < na_ref[0])
    def _():
        out = (jnp.dot(h_ref[...], wd_ref[...], preferred_element_type=F32) + bd_ref[...]) * rw_ref[...]
        for c in range(o_ref.shape[1]):
            o_ref[:, c, :] = out[:, c * LANES:(c + 1) * LANES]

    @pl.when(i >= na_ref[0])
    def _():
        o_ref[...] = jnp.zeros_like(o_ref)


def _tile_clamp(i, na):
    return jnp.minimum(i, na[0] - 1)


def _experts(xs, row_w, tile_e, n_active, wg, wu, bg, bu, wd, bd, tn_ff, tn_d):
    r_max, d = xs.shape
    ff = wg.shape[2]
    tm = EXPERT_TM
    n_tiles = r_max // tm
    gu_spec = pltpu.PrefetchScalarGridSpec(
        num_scalar_prefetch=2, grid=(ff // tn_ff, n_tiles),
        in_specs=[
            pl.BlockSpec((tm, d), lambda j, i, te, na: (_tile_clamp(i, na), 0)),
            pl.BlockSpec((None, d, tn_ff), lambda j, i, te, na: (te[_tile_clamp(i, na)], 0, j)),
            pl.BlockSpec((None, d, tn_ff), lambda j, i, te, na: (te[_tile_clamp(i, na)], 0, j)),
            pl.BlockSpec((None, 1, tn_ff), lambda j, i, te, na: (te[_tile_clamp(i, na)], 0, j)),
            pl.BlockSpec((None, 1, tn_ff), lambda j, i, te, na: (te[_tile_clamp(i, na)], 0, j)),
        ],
        out_specs=pl.BlockSpec((tm, tn_ff), lambda j, i, te, na: (i, j)))
    hidden = pl.pallas_call(
        _gateup_kernel, grid_spec=gu_spec, out_shape=jax.ShapeDtypeStruct((r_max, ff), BF16),
        compiler_params=_cparams(2), name="expert_gate_up",
    )(tile_e, n_active, xs, wg, wu, bg, bu)
    dn_spec = pltpu.PrefetchScalarGridSpec(
        num_scalar_prefetch=2, grid=(d // tn_d, n_tiles),
        in_specs=[
            pl.BlockSpec((tm, ff), lambda j, i, te, na: (_tile_clamp(i, na), 0)),
            pl.BlockSpec((None, ff, tn_d), lambda j, i, te, na: (te[_tile_clamp(i, na)], 0, j)),
            pl.BlockSpec((None, 1, tn_d), lambda j, i, te, na: (te[_tile_clamp(i, na)], 0, j)),
            pl.BlockSpec((tm, 1), lambda j, i, te, na: (_tile_clamp(i, na), 0)),
        ],
        out_specs=pl.BlockSpec((tm, tn_d // LANES, LANES), lambda j, i, te, na: (i, j, 0)))
    return pl.pallas_call(
        _down_kernel, grid_spec=dn_spec, out_shape=jax.ShapeDtypeStruct((r_max, d // LANES, LANES), F32),
        compiler_params=_cparams(2), name="expert_down",
    )(tile_e, n_active, hidden, wd, bd, row_w)


def _combine_kernel(dest_hbm, rows_hbm, x_ref, o_ref, dest_smem, buf, isem, rsem):
    i = pl.program_id(0)
    n = pl.num_programs(0)
    slot = _stage_indices(dest_hbm, dest_smem, isem, i, n)
    tm = COMBINE_TM

    def row_copy(k, r):
        src = dest_smem[slot, r * TOP_K + k]
        return pltpu.make_async_copy(rows_hbm.at[src], buf.at[k, r], rsem)

    def start(r, c):
        for k in range(TOP_K):
            row_copy(k, r).start()
        return c
    lax.fori_loop(0, tm, start, 0)

    def wait(r, c):
        for k in range(TOP_K):
            row_copy(k, r).wait()
        return c
    lax.fori_loop(0, tm, wait, 0)

    acc = x_ref[...]
    for k in range(TOP_K):
        acc = acc + buf[k]
    o_ref[...] = acc


def _combine(rows, dest, x3d):
    t, s, l = x3d.shape
    tm = COMBINE_TM
    n_steps = t // tm
    dest2d = dest.reshape(n_steps, tm * TOP_K)
    blk = pl.BlockSpec((tm, s, l), lambda i: (i, 0, 0))
    return pl.pallas_call(
        _combine_kernel, grid=(n_steps,),
        in_specs=[pl.BlockSpec(memory_space=pl.ANY), pl.BlockSpec(memory_space=pl.ANY), blk],
        out_specs=blk,
        out_shape=jax.ShapeDtypeStruct((t, s, l), F32),
        scratch_shapes=[pltpu.SMEM((2, tm * TOP_K), jnp.int32), pltpu.VMEM((TOP_K, tm, s, l), F32),
                        pltpu.SemaphoreType.DMA((2,)), pltpu.SemaphoreType.DMA(())],
        compiler_params=_cparams(1), name="combine",
    )(dest2d, rows, x3d)


def _moe(x_mid, g2, w_router, b_router, w_gate_up, b_gate_up, w_down, b_down, tn_ff, tn_d):
    t, d = x_mid.shape
    ne = w_router.shape[1]
    h2, idx_t, gate_t, rank_t, cnt = _router(x_mid, g2, w_router, b_router)

    tm = EXPERT_TM
    r_max = t * TOP_K + ne * tm
    n_tiles = r_max // tm
    counts = cnt[0, :ne].astype(jnp.int32)
    padded = (counts + tm - 1) // tm * tm
    pend = jnp.cumsum(padded)
    pstart = pend - padded
    idx = idx_t[:, :TOP_K]
    dest = (pstart[idx] + rank_t[:, :TOP_K]).astype(jnp.int32)
    n_rows_used = pend[-1:].astype(jnp.int32)
    n_active = n_rows_used // tm
    tok = jnp.repeat(jnp.arange(t, dtype=jnp.int32), TOP_K)
    row_tok = jnp.zeros((r_max,), jnp.int32).at[dest.reshape(-1)].set(tok)
    row_w = jnp.zeros((r_max,), F32).at[dest.reshape(-1)].set(gate_t[:, :TOP_K].reshape(-1))[:, None]
    tile_e = jnp.minimum(jnp.searchsorted(pend, jnp.arange(n_tiles, dtype=jnp.int32) * tm, side='right'),
                         ne - 1).astype(jnp.int32)

    xs = _gather_rows(h2.reshape(t, d // LANES, LANES), row_tok).reshape(r_max, d)

    wg = w_gate_up[:, :, 0::2].astype(BF16)
    wu = w_gate_up[:, :, 1::2].astype(BF16)
    bg = b_gate_up[:, None, 0::2]
    bu = b_gate_up[:, None, 1::2]
    wd = w_down.astype(BF16)
    bd = b_down[:, None, :]
    rows = _experts(xs, row_w, tile_e, n_active, wg, wu, bg, bu, wd, bd, tn_ff, tn_d)
    return _combine(rows, dest, x_mid.reshape(t, d // LANES, LANES)).reshape(t, d)


def _alibi_slopes(n):
    return (2.0 ** (-ALIBI_MAX_BIAS * (np.arange(n, dtype=np.float32) + 1.0) / n)).astype(np.float32)


def _mixer(x2d, batch, seq, norm1_g, w_in, q_norm_swa, k_norm_swa, q_norm_dil, k_norm_dil, sinks, w_out,
           n_swa, n_kv, n_dil, tm_proj, tn_out):
    t, d = x2d.shape
    swa_q, swa_kv, dil_w = n_swa * HEAD_DIM, n_kv * HEAD_DIM, n_dil * HEAD_DIM
    assert n_swa // n_kv == GQA_GROUP and seq % (16 * ATTN_BLOCK) == 0
    s1, s2, s3 = swa_q, swa_q + swa_kv, swa_q + 2 * swa_kv

    def dup(wc):
        wc = wc.reshape(d, n_kv, HEAD_DIM)
        return jnp.concatenate([wc, wc], axis=-1).reshape(d, 2 * swa_kv)

    w = jnp.concatenate([w_in[:, :s1], dup(w_in[:, s1:s2]), dup(w_in[:, s2:s3]), w_in[:, s3:]],
                        axis=1).astype(BF16)
    scale = HEAD_DIM ** -0.5
    ones = lambda n: jnp.ones((n,), F32)
    gains = jnp.concatenate([
        jnp.tile(q_norm_swa * scale, n_swa), jnp.tile(k_norm_swa, 2 * n_kv), ones(2 * swa_kv),
        jnp.tile(q_norm_dil * scale, n_dil), jnp.tile(k_norm_dil, n_dil), ones(dil_w)])[None, :]
    c = w.shape[1]
    bounds = np.cumsum([0, swa_q, 2 * swa_kv, 2 * swa_kv, dil_w, dil_w, dil_w])
    assert all(b % PROJ_TN == 0 for b in bounds)
    seg_norm = [1, 1, 0, 1, 1, 0]
    flags = np.zeros((c // PROJ_TN,), np.int32)
    for sidx in range(6):
        flags[bounds[sidx] // PROJ_TN:bounds[sidx + 1] // PROJ_TN] = seg_norm[sidx]
    proj = _inproj(x2d, norm1_g[None, :], w, gains, jnp.asarray(flags), tm_proj)

    qa0, ka0, va0, qb0, kb0, vb0 = (int(b) for b in bounds[:6])
    sink_row = jnp.repeat(sinks.astype(F32), HEAD_DIM)[None, :]
    o_a = _band_attention(proj, jnp.asarray(_alibi_slopes(n_swa)), batch=batch, seq=seq, dilation=1,
                          q_col0=qa0, k_col0=ka0, v_col0=va0, n_heads=n_swa, kv_shared=True,
                          max_diff=SWA_WINDOW - 1, sinks=sink_row)
    slopes_dil = jnp.asarray(_alibi_slopes(n_dil))
    outs, lses = [], []
    for window, dil in DILATED_BRANCHES:
        o_i, lse_i = _band_attention(proj, slopes_dil, batch=batch, seq=seq, dilation=dil,
                                     q_col0=qb0, k_col0=kb0, v_col0=vb0, n_heads=n_dil, kv_shared=False,
                                     max_diff=window // dil)
        outs.append(o_i)
        lses.append(lse_i)
    o_b = _merge(outs, lses)
    return _outproj(o_a, o_b, w_out.astype(BF16), x2d, tm_proj, tn_out)


def kernel(x, norm1_g, w_in, q_norm_swa, k_norm_swa, q_norm_dil, k_norm_dil, sinks, w_out, norm2_g,
           w_router, b_router, w_gate_up, b_gate_up, w_down, b_down):
    b, s, d = x.shape
    depth = norm1_g.shape[0]
    n_heads = d // HEAD_DIM
    n_swa = n_heads // 2
    n_kv = n_swa // GQA_GROUP
    n_dil = n_heads - n_swa
    x2d = x.reshape(b * s, d)
    for l in range(depth):
        x_mid = _mixer(x2d, b, s, norm1_g[l], w_in[l], q_norm_swa[l], k_norm_swa[l], q_norm_dil[l],
                       k_norm_dil[l], sinks[l], w_out[l], n_swa, n_kv, n_dil, tm_proj=512, tn_out=512)
        x2d = _moe(x_mid, norm2_g[l][None, :], w_router[l], b_router[l], w_gate_up[l], b_gate_up[l],
                   w_down[l], b_down[l], tn_ff=512, tn_d=1024)
    return x2d.reshape(b, s, d)
```

```python
import functools

import jax
import jax.numpy as jnp
import numpy as np
from jax import lax
from jax.experimental import pallas as pl
from jax.experimental.pallas import tpu as pltpu

F32 = jnp.float32
BF16 = jnp.bfloat16

HEAD_DIM = 64
LANES = 128
ATTN_BLOCK = 128
GQA_GROUP = 8
SWA_WINDOW = 128
DILATED_BRANCHES = ((128, 1), (512, 4), (2048, 16))
N_EXPERTS = 32
TOP_K = 4
SWIGLU_LIMIT = 7.0
SWIGLU_ALPHA = 1.702
ALIBI_MAX_BIAS = 8.0
EPS = 1e-6
MASK_DIST = 1e30
NEG_BIG = -1e30

VMEM_LIMIT = 52 * 1024 * 1024

PROJ_TN = 512
ATTN_W = 512
EXPERT_TM = 512
GATHER_ROWS = 256
COMBINE_TM = 128


def _cparams(n_axes):
    return pltpu.CompilerParams(dimension_semantics=("arbitrary",) * n_axes,
                                vmem_limit_bytes=VMEM_LIMIT)


def _inproj_kernel(flag_ref, x_ref, g_ref, w_ref, gain_ref, ones_ref, o_ref, h_scr):
    j = pl.program_id(1)

    @pl.when(j == 0)
    def _():
        x = x_ref[...]
        ms = jnp.mean(x * x, axis=-1, keepdims=True)
        h_scr[...] = (x * lax.rsqrt(ms + EPS) * g_ref[...]).astype(BF16)

    acc = jnp.dot(h_scr[...], w_ref[...], preferred_element_type=F32)

    @pl.when(flag_ref[j] == 1)
    def _():
        ss = jnp.dot((acc * acc).astype(BF16), ones_ref[...], preferred_element_type=F32)
        o_ref[...] = (acc * lax.rsqrt(ss * (1.0 / HEAD_DIM) + EPS) * gain_ref[...]).astype(BF16)

    @pl.when(flag_ref[j] == 0)
    def _():
        o_ref[...] = acc.astype(BF16)


def _inproj(x2d, g1, w, gains, flags, tm):
    t, d = x2d.shape
    c = w.shape[1]
    tn = PROJ_TN
    head_id = np.arange(tn) // HEAD_DIM
    ones_bd = jnp.asarray(head_id[:, None] == head_id[None, :], dtype=BF16)
    grid_spec = pltpu.PrefetchScalarGridSpec(
        num_scalar_prefetch=1,
        grid=(t // tm, c // tn),
        in_specs=[
            pl.BlockSpec((tm, d), lambda i, j, f: (i, 0)),
            pl.BlockSpec((1, d), lambda i, j, f: (0, 0)),
            pl.BlockSpec((d, tn), lambda i, j, f: (0, j)),
            pl.BlockSpec((1, tn), lambda i, j, f: (0, j)),
            pl.BlockSpec((tn, tn), lambda i, j, f: (0, 0)),
        ],
        out_specs=pl.BlockSpec((tm, tn), lambda i, j, f: (i, j)),
        scratch_shapes=[pltpu.VMEM((tm, d), BF16)],
    )
    return pl.pallas_call(
        _inproj_kernel, grid_spec=grid_spec,
        out_shape=jax.ShapeDtypeStruct((t, c), BF16),
        compiler_params=_cparams(2), name="inproj",
    )(flags, x2d, g1, w, gains, ones_bd)


def _attn_kernel(slope_ref, q_ref, kp_ref, kc_ref, vp_ref, vc_ref, *rest,
                 n_pairs, kv_shared, max_diff, dist_scale, with_sink):
    if with_sink:
        sink_ref, o_ref = rest
        lse_ref = None
    else:
        o_ref, lse_ref = rest
    blk = pl.program_id(2)
    cg = pl.program_id(3)

    qi = lax.broadcasted_iota(jnp.int32, (ATTN_BLOCK, 2 * ATTN_BLOCK), 0)
    kj = lax.broadcasted_iota(jnp.int32, (ATTN_BLOCK, 2 * ATTN_BLOCK), 1)
    dist = qi + ATTN_BLOCK - kj
    valid = (dist >= 0) & (dist <= max_diff) & ((blk > 0) | (kj >= ATTN_BLOCK))
    dist_m = jnp.where(valid, (dist * dist_scale).astype(F32), MASK_DIST)

    lane = lax.broadcasted_iota(jnp.int32, (ATTN_BLOCK, LANES), 1)
    low = lane < HEAD_DIM

    for p in range(n_pairs):
        cols = slice(p * LANES, (p + 1) * LANES)
        kcols = slice(0, LANES) if kv_shared else cols
        q2 = q_ref[:, cols]
        kk = jnp.concatenate([kp_ref[:, kcols], kc_ref[:, kcols]], axis=0)
        vv = jnp.concatenate([vp_ref[:, kcols], vc_ref[:, kcols]], axis=0)
        outs, lses = [], []
        for hh in range(2):
            slope = slope_ref[cg * (2 * n_pairs) + 2 * p + hh]
            qm = jnp.where(low if hh == 0 else ~low, q2, jnp.zeros_like(q2))
            s = lax.dot_general(qm, kk, (((1,), (1,)), ((), ())),
                                preferred_element_type=F32)
            s = s - slope * dist_m
            m = jnp.max(s, axis=-1, keepdims=True)
            e = jnp.exp(s - m)
            l = jnp.sum(e, axis=-1, keepdims=True)
            o = jnp.dot(e.astype(BF16), vv, preferred_element_type=F32)
            outs.append(o / l)
            lses.append(jnp.broadcast_to(m + jnp.log(l), (ATTN_BLOCK, LANES)))
        o2 = jnp.where(low, outs[0], outs[1])
        lse2 = jnp.where(low, lses[0], lses[1])
        if with_sink:
            o2 = o2 * jax.nn.sigmoid(lse2 - sink_ref[:, cols])
        else:
            lse_ref[:, cols] = lse2
        o_ref[:, cols] = o2.astype(o_ref.dtype)


def _band_attention(proj, slopes, *, batch, seq, dilation, q_col0, k_col0, v_col0, n_heads,
                    kv_shared, max_diff, sinks=None):
    t, c = proj.shape
    d = dilation
    lsub = seq // d
    nblk = lsub // ATTN_BLOCK
    w = ATTN_W
    n_pairs = w // LANES
    width = n_heads * HEAD_DIM
    ncg = width // w
    kw = LANES if kv_shared else w
    pv = proj.reshape(batch, lsub, d * c)

    def qmap(b, r, i, g, s):
        return (b, i, r * (c // w) + q_col0 // w + g)

    def kvmap(col0, prev):
        def f(b, r, i, g, s):
            blk = jnp.maximum(i - 1, 0) if prev else i
            return (b, blk, r * (c // kw) + col0 // kw + g)
        return f

    def omap(b, r, i, g, s):
        return (b, i, r * ncg + g)

    in_specs = [
        pl.BlockSpec((None, ATTN_BLOCK, w), qmap),
        pl.BlockSpec((None, ATTN_BLOCK, kw), kvmap(k_col0, True)),
        pl.BlockSpec((None, ATTN_BLOCK, kw), kvmap(k_col0, False)),
        pl.BlockSpec((None, ATTN_BLOCK, kw), kvmap(v_col0, True)),
        pl.BlockSpec((None, ATTN_BLOCK, kw), kvmap(v_col0, False)),
    ]
    args = [pv, pv, pv, pv, pv]
    with_sink = sinks is not None
    o_shape = jax.ShapeDtypeStruct((batch, lsub, d * width), BF16)
    o_spec = pl.BlockSpec((None, ATTN_BLOCK, w), omap)
    if with_sink:
        in_specs.append(pl.BlockSpec((1, w), lambda b, r, i, g, s: (0, g)))
        args.append(sinks)
        out_shape, out_specs = o_shape, o_spec
    else:
        out_shape = (o_shape, jax.ShapeDtypeStruct((batch, lsub, d * width), F32))
        out_specs = (o_spec, pl.BlockSpec((None, ATTN_BLOCK, w), omap))
    kern = functools.partial(_attn_kernel, n_pairs=n_pairs, kv_shared=kv_shared,
                             max_diff=max_diff, dist_scale=d, with_sink=with_sink)
    grid_spec = pltpu.PrefetchScalarGridSpec(
        num_scalar_prefetch=1, grid=(batch, d, nblk, ncg),
        in_specs=in_specs, out_specs=out_specs)
    res = pl.pallas_call(kern, grid_spec=grid_spec, out_shape=out_shape,
                         compiler_params=_cparams(4), name=f"band_attn_d{d}" + ("_sink" if with_sink else ""),
                         )(slopes, *args)
    if with_sink:
        return res.reshape(t, width)
    return res[0].reshape(t, width), res[1].reshape(t, width)


def _merge_kernel(o1, o2, o3, l1, l2, l3, out):
    a, b, c = l1[...], l2[...], l3[...]
    m = jnp.maximum(jnp.maximum(a, b), c)
    ea, eb, ec = jnp.exp(a - m), jnp.exp(b - m), jnp.exp(c - m)
    num = ea * o1[...].astype(F32) + eb * o2[...].astype(F32) + ec * o3[...].astype(F32)
    out[...] = (num / (ea + eb + ec)).astype(out.dtype)


def _merge(os_, ls_, tm=256):
    t, w = os_[0].shape
    spec = pl.BlockSpec((tm, w), lambda i: (i, 0))
    return pl.pallas_call(
        _merge_kernel, grid=(t // tm,), in_specs=[spec] * 6, out_specs=spec,
        out_shape=jax.ShapeDtypeStruct((t, w), BF16), compiler_params=_cparams(1), name="branch_merge",
    )(*os_, *ls_)


def _outproj_kernel(oa_ref, ob_ref, wt_ref, wb_ref, x_ref, o_ref):
    acc = jnp.dot(oa_ref[...], wt_ref[...], preferred_element_type=F32)
    acc = acc + jnp.dot(ob_ref[...], wb_ref[...], preferred_element_type=F32)
    o_ref[...] = x_ref[...] + acc


def _outproj(o_a, o_b, w_out, x2d, tm, tn):
    t, d = x2d.shape
    ha, hb = o_a.shape[1], o_b.shape[1]
    assert ha == hb
    return pl.pallas_call(
        _outproj_kernel, grid=(t // tm, d // tn),
        in_specs=[
            pl.BlockSpec((tm, ha), lambda i, j: (i, 0)),
            pl.BlockSpec((tm, hb), lambda i, j: (i, 0)),
            pl.BlockSpec((ha, tn), lambda i, j: (0, j)),
            pl.BlockSpec((hb, tn), lambda i, j: (1, j)),
            pl.BlockSpec((tm, tn), lambda i, j: (i, j)),
        ],
        out_specs=pl.BlockSpec((tm, tn), lambda i, j: (i, j)),
        out_shape=jax.ShapeDtypeStruct((t, d), F32), compiler_params=_cparams(2), name="outproj",
    )(o_a, o_b, w_out, w_out, x2d)


def _router_kernel(x_ref, g_ref, whi_ref, wlo_ref, b_ref, tri_ref,
                   h_ref, idx_ref, gate_ref, rank_ref, cnt_ref, carry):
    i = pl.program_id(0)

    @pl.when(i == 0)
    def _():
        carry[...] = jnp.zeros_like(carry)

    x = x_ref[...]
    ms = jnp.mean(x * x, axis=-1, keepdims=True)
    h = x * lax.rsqrt(ms + EPS) * g_ref[...]
    h_hi = h.astype(BF16)
    h_lo = (h - h_hi.astype(F32)).astype(BF16)
    u = pltpu.bitcast(h, jnp.uint32)
    r = (u + jnp.uint32(0x7FFF) + ((u >> 16) & jnp.uint32(1))) >> 16
    for s in range(h_ref.shape[1]):
        lo = r[:, 2 * s * LANES:(2 * s + 1) * LANES]
        hi = r[:, (2 * s + 1) * LANES:(2 * s + 2) * LANES]
        h_ref[:, s, :] = lo | (hi << 16)
    logits = (jnp.dot(h_hi, whi_ref[...], preferred_element_type=F32)
              + jnp.dot(h_lo, whi_ref[...], preferred_element_type=F32)
              + jnp.dot(h_hi, wlo_ref[...], preferred_element_type=F32)) + b_ref[...]

    tm = x.shape[0]
    lane = lax.broadcasted_iota(jnp.int32, (tm, LANES), 1).astype(F32)
    work = logits
    multihot = jnp.zeros((tm, LANES), F32)
    vals, idxs = [], []
    for _ in range(TOP_K):
        m = jnp.max(work, axis=-1, keepdims=True)
        ik = jnp.min(jnp.where(work == m, lane, float(LANES)), axis=-1, keepdims=True)
        sel = lane == ik
        work = jnp.where(sel, -jnp.inf, work)
        multihot = jnp.where(sel, 1.0, multihot)
        vals.append(m)
        idxs.append(ik)
    es = [jnp.exp(v - vals[0]) for v in vals]
    denom = es[0] + es[1] + es[2] + es[3]

    cum = jnp.dot(tri_ref[...], multihot.astype(BF16), preferred_element_type=F32) + carry[0:1, :]
    idx_t = jnp.zeros((tm, LANES), F32)
    gate_t = jnp.zeros((tm, LANES), F32)
    rank_t = jnp.zeros((tm, LANES), F32)
    for k in range(TOP_K):
        rk = jnp.sum(jnp.where(lane == idxs[k], cum, 0.0), axis=-1, keepdims=True)
        here = lane == float(k)
        idx_t = jnp.where(here, idxs[k], idx_t)
        gate_t = jnp.where(here, es[k] / denom, gate_t)
        rank_t = jnp.where(here, rk, rank_t)
    idx_ref[...] = idx_t.astype(jnp.int32)
    gate_ref[...] = gate_t
    rank_ref[...] = rank_t.astype(jnp.int32)
    new_carry = carry[0:1, :] + jnp.sum(multihot, axis=0, keepdims=True)
    carry[...] = jnp.broadcast_to(new_carry, carry.shape)
    cnt_ref[...] = jnp.broadcast_to(new_carry, cnt_ref.shape)


def _router(x2d, g2, w_router, b_router, tm=256):
    t, d = x2d.shape
    ne = w_router.shape[1]
    w_pad = jnp.zeros((d, LANES), F32).at[:, :ne].set(w_router)
    w_hi = w_pad.astype(BF16)
    w_lo = (w_pad - w_hi.astype(F32)).astype(BF16)
    b_pad = jnp.full((1, LANES), NEG_BIG, F32).at[0, :ne].set(b_router)
    tri = jnp.asarray(np.tril(np.ones((tm, tm), np.float32), -1), dtype=BF16)
    tile = lambda dt: jax.ShapeDtypeStruct((t, LANES), dt)
    row = pl.BlockSpec((tm, LANES), lambda i: (i, 0))
    const = lambda shape: pl.BlockSpec(shape, lambda i: (0, 0))
    return pl.pallas_call(
        _router_kernel, grid=(t // tm,),
        in_specs=[pl.BlockSpec((tm, d), lambda i: (i, 0)), const((1, d)), const((d, LANES)),
                  const((d, LANES)), const((1, LANES)), const((tm, tm))],
        out_specs=(pl.BlockSpec((tm, d // (2 * LANES), LANES), lambda i: (i, 0, 0)), row, row, row,
                   const((8, LANES))),
        out_shape=(jax.ShapeDtypeStruct((t, d // (2 * LANES), LANES), jnp.uint32),
                   tile(jnp.int32), tile(F32), tile(jnp.int32),
                   jax.ShapeDtypeStruct((8, LANES), F32)),
        scratch_shapes=[pltpu.VMEM((8, LANES), F32)],
        compiler_params=_cparams(1), name="router",
    )(x2d, g2, w_hi, w_lo, b_pad, tri)


INDEX_SLOTS = 3
ROW_SLOTS = 2


def _row_prefetch_ring(idx_hbm, idx_smem, isem, start_rows, wait_rows):
    i = pl.program_id(0)
    n = pl.num_programs(0)

    def idx_copy(step):
        slot = lax.rem(step, INDEX_SLOTS)
        return pltpu.make_async_copy(idx_hbm.at[step], idx_smem.at[slot], isem.at[slot])

    @pl.when(i == 0)
    def _():
        idx_copy(0).start()
        idx_copy(0).wait()
        start_rows(0, 0, 0)

        @pl.when(n > 1)
        def _():
            idx_copy(1).start()

    @pl.when(i + 1 < n)
    def _():
        idx_copy(i + 1).wait()
        start_rows(i + 1, lax.rem(i + 1, INDEX_SLOTS), lax.rem(i + 1, ROW_SLOTS))

    @pl.when(i + 2 < n)
    def _():
        idx_copy(i + 2).start()

    row_slot = lax.rem(i, ROW_SLOTS)
    wait_rows(row_slot)
    return row_slot


def _gather_kernel(tok_hbm, h_hbm, xs_ref, tok_smem, buf, isem, rsem):
    g = GATHER_ROWS
    words = h_hbm.shape[1]

    def start_rows(step, idx_slot, row_slot):
        def body(r, c):
            t = tok_smem[idx_slot, r]
            dst = buf.at[row_slot, pl.ds(pl.multiple_of(r * words, words), words)]
            pltpu.make_async_copy(h_hbm.at[t], dst, rsem.at[row_slot]).start()
            return c
        lax.fori_loop(0, g, body, 0)

    def wait_rows(row_slot):
        pltpu.make_async_copy(buf.at[row_slot], buf.at[row_slot], rsem.at[row_slot]).wait()

    row_slot = _row_prefetch_ring(tok_hbm, tok_smem, isem, start_rows, wait_rows)

    for s in range(words):
        x = buf[row_slot, pl.ds(s, g, stride=words), :]
        lo = pltpu.bitcast(x << 16, F32)
        hi = pltpu.bitcast(x & jnp.uint32(0xFFFF0000), F32)
        xs_ref[:, 2 * s * LANES:(2 * s + 1) * LANES] = lo.astype(BF16)
        xs_ref[:, (2 * s + 1) * LANES:(2 * s + 2) * LANES] = hi.astype(BF16)


def _gather_rows(h_packed, row_tok):
    t, words, _ = h_packed.shape
    r_max = row_tok.shape[0]
    g = GATHER_ROWS
    n_steps = r_max // g
    tok2d = row_tok.reshape(n_steps, g)
    return pl.pallas_call(
        _gather_kernel, grid=(n_steps,),
        in_specs=[pl.BlockSpec(memory_space=pl.ANY), pl.BlockSpec(memory_space=pl.ANY)],
        out_specs=pl.BlockSpec((g, 2 * words * LANES), lambda i: (i, 0)),
        scratch_shapes=[pltpu.SMEM((INDEX_SLOTS, g), jnp.int32),
                        pltpu.VMEM((ROW_SLOTS, g * words, LANES), jnp.uint32),
                        pltpu.SemaphoreType.DMA((INDEX_SLOTS,)), pltpu.SemaphoreType.DMA((ROW_SLOTS,))],
        out_shape=jax.ShapeDtypeStruct((r_max, 2 * words * LANES), BF16),
        compiler_params=_cparams(1), name="row_gather",
    )(tok2d, h_packed)


DEINT_COLS = 512


def _deint_kernel(w_ref, pe_ref, po_ref, g_ref, u_ref):
    w = w_ref[...].astype(BF16)
    g_ref[...] = jnp.dot(w, pe_ref[...], preferred_element_type=F32).astype(BF16)
    u_ref[...] = jnp.dot(w, po_ref[...], preferred_element_type=F32).astype(BF16)


def _deinterleave(w, tr=1024):
    ne, d, c2 = w.shape
    tr = min(tr, d)
    tc = min(DEINT_COLS, c2)
    src = np.arange(tc)[:, None]
    dst = np.arange(tc // 2)[None, :]
    p_even = jnp.asarray(src == 2 * dst, dtype=BF16)
    p_odd = jnp.asarray(src == 2 * dst + 1, dtype=BF16)
    const = pl.BlockSpec((tc, tc // 2), lambda e, i, j: (0, 0))
    out = pl.BlockSpec((None, tr, tc // 2), lambda e, i, j: (e, i, j))
    shape = jax.ShapeDtypeStruct((ne, d, c2 // 2), BF16)
    return pl.pallas_call(
        _deint_kernel, grid=(ne, d // tr, c2 // tc),
        in_specs=[pl.BlockSpec((None, tr, tc), lambda e, i, j: (e, i, j)), const, const],
        out_specs=(out, out), out_shape=(shape, shape),
        compiler_params=_cparams(3), name="gate_up_deinterleave",
    )(w, p_even, p_odd)


def _gateup_kernel(te_ref, na_ref, xs_ref, wg_ref, wu_ref, bg_ref, bu_ref, h_ref):
    i = pl.program_id(1)

    @pl.when(i < na_ref[0])
    def _():
        x = xs_ref[...]
        gate = jnp.dot(x, wg_ref[...], preferred_element_type=F32) + bg_ref[...]
        up = jnp.dot(x, wu_ref[...], preferred_element_type=F32) + bu_ref[...]
        gate = jnp.minimum(gate, SWIGLU_LIMIT)
        up = jnp.clip(up, -SWIGLU_LIMIT, SWIGLU_LIMIT)
        act = (up + 1.0) * gate * jax.nn.sigmoid(SWIGLU_ALPHA * gate)
        h_ref[...] = act.astype(h_ref.dtype)

    @pl.when(i >= na_ref[0])
    def _():
        h_ref[...] = jnp.zeros_like(h_ref)


def _down_kernel(te_ref, na_ref, h_ref, wd_ref, bd_ref, rw_ref, o_ref):
    i = pl.program_id(1)

    @pl.when(i < na_ref[0])
    def _():
        out = (jnp.dot(h_ref[...], wd_ref[...], preferred_element_type=F32) + bd_ref[...]) * rw_ref[...]
        for c in range(o_ref.shape[1]):
            o_ref[:, c, :] = out[:, c * LANES:(c + 1) * LANES]

    @pl.when(i >= na_ref[0])
    def _():
        o_ref[...] = jnp.zeros_like(o_ref)


def _tile_clamp(i, na):
    return jnp.minimum(i, na[0] - 1)


def _experts(xs, row_w, tile_e, n_active, wg, wu, bg, bu, wd, bd, tn_ff, tn_d):
    r_max, d = xs.shape
    ff = wg.shape[2]
    tm = EXPERT_TM
    n_tiles = r_max // tm
    gu_spec = pltpu.PrefetchScalarGridSpec(
        num_scalar_prefetch=2, grid=(ff // tn_ff, n_tiles),
        in_specs=[
            pl.BlockSpec((tm, d), lambda j, i, te, na: (_tile_clamp(i, na), 0)),
            pl.BlockSpec((None, d, tn_ff), lambda j, i, te, na: (te[_tile_clamp(i, na)], 0, j)),
            pl.BlockSpec((None, d, tn_ff), lambda j, i, te, na: (te[_tile_clamp(i, na)], 0, j)),
            pl.BlockSpec((None, 1, tn_ff), lambda j, i, te, na: (te[_tile_clamp(i, na)], 0, j)),
            pl.BlockSpec((None, 1, tn_ff), lambda j, i, te, na: (te[_tile_clamp(i, na)], 0, j)),
        ],
        out_specs=pl.BlockSpec((tm, tn_ff), lambda j, i, te, na: (i, j)))
    hidden = pl.pallas_call(
        _gateup_kernel, grid_spec=gu_spec, out_shape=jax.ShapeDtypeStruct((r_max, ff), BF16),
        compiler_params=_cparams(2), name="expert_gate_up",
    )(tile_e, n_active, xs, wg, wu, bg, bu)
    dn_spec = pltpu.PrefetchScalarGridSpec(
        num_scalar_prefetch=2, grid=(d // tn_d, n_tiles),
        in_specs=[
            pl.BlockSpec((tm, ff), lambda j, i, te, na: (_tile_clamp(i, na), 0)),
            pl.BlockSpec((None, ff, tn_d), lambda j, i, te, na: (te[_tile_clamp(i, na)], 0, j)),
            pl.BlockSpec((None, 1, tn_d), lambda j, i, te, na: (te[_tile_clamp(i, na)], 0, j)),
            pl.BlockSpec((tm, 1), lambda j, i, te, na: (_tile_clamp(i, na), 0)),
        ],
        out_specs=pl.BlockSpec((tm, tn_d // LANES, LANES), lambda j, i, te, na: (i, j, 0)))
    return pl.pallas_call(
        _down_kernel, grid_spec=dn_spec, out_shape=jax.ShapeDtypeStruct((r_max, d // LANES, LANES), F32),
        compiler_params=_cparams(2), name="expert_down",
    )(tile_e, n_active, hidden, wd, bd, row_w)


def _combine_kernel(dest_hbm, rows_hbm, x_ref, o_ref, dest_smem, buf, sum_scr, isem, rsem):
    tm = COMBINE_TM
    chunks = rows_hbm.shape[1]
    seg = tm * chunks

    def start_rows(step, idx_slot, row_slot):
        def body(r, c):
            for k in range(TOP_K):
                src = dest_smem[idx_slot, r * TOP_K + k]
                off = pl.multiple_of((k * tm + r) * chunks, chunks)
                pltpu.make_async_copy(rows_hbm.at[src], buf.at[row_slot, pl.ds(off, chunks)],
                                      rsem.at[row_slot]).start()
            return c
        lax.fori_loop(0, tm, body, 0)

    def wait_rows(row_slot):
        pltpu.make_async_copy(buf.at[row_slot], buf.at[row_slot], rsem.at[row_slot]).wait()

    row_slot = _row_prefetch_ring(dest_hbm, dest_smem, isem, start_rows, wait_rows)

    acc = buf[row_slot, pl.ds(0, seg), :]
    for k in range(1, TOP_K):
        acc = acc + buf[row_slot, pl.ds(k * seg, seg), :]
    sum_scr[...] = acc
    for c in range(chunks):
        cols = slice(c * LANES, (c + 1) * LANES)
        o_ref[:, cols] = x_ref[:, cols] + sum_scr[pl.ds(c, tm, stride=chunks), :]


def _combine(rows, dest, x2d):
    t, d = x2d.shape
    chunks = rows.shape[1]
    tm = COMBINE_TM
    n_steps = t // tm
    dest2d = dest.reshape(n_steps, tm * TOP_K)
    blk = pl.BlockSpec((tm, d), lambda i: (i, 0))
    return pl.pallas_call(
        _combine_kernel, grid=(n_steps,),
        in_specs=[pl.BlockSpec(memory_space=pl.ANY), pl.BlockSpec(memory_space=pl.ANY), blk],
        out_specs=blk,
        out_shape=jax.ShapeDtypeStruct((t, d), F32),
        scratch_shapes=[pltpu.SMEM((INDEX_SLOTS, tm * TOP_K), jnp.int32),
                        pltpu.VMEM((ROW_SLOTS, TOP_K * tm * chunks, LANES), F32),
                        pltpu.VMEM((tm * chunks, LANES), F32),
                        pltpu.SemaphoreType.DMA((INDEX_SLOTS,)), pltpu.SemaphoreType.DMA((ROW_SLOTS,))],
        compiler_params=_cparams(1), name="combine",
    )(dest2d, rows, x2d)


def _moe(x_mid, g2, w_router, b_router, w_gate_up, b_gate_up, w_down, b_down, tn_ff, tn_d):
    t, d = x_mid.shape
    ne = w_router.shape[1]
    h2, idx_t, gate_t, rank_t, cnt = _router(x_mid, g2, w_router, b_router)

    tm = EXPERT_TM
    r_max = t * TOP_K + ne * tm
    n_tiles = r_max // tm
    counts = cnt[0, :ne].astype(jnp.int32)
    padded = (counts + tm - 1) // tm * tm
    pend = jnp.cumsum(padded)
    pstart = pend - padded
    idx = idx_t[:, :TOP_K]
    dest = (pstart[idx] + rank_t[:, :TOP_K]).astype(jnp.int32)
    n_rows_used = pend[-1:].astype(jnp.int32)
    n_active = n_rows_used // tm
    tok = jnp.repeat(jnp.arange(t, dtype=jnp.int32), TOP_K)
    row_tok = jnp.zeros((r_max,), jnp.int32).at[dest.reshape(-1)].set(tok)
    row_w = jnp.zeros((r_max,), F32).at[dest.reshape(-1)].set(gate_t[:, :TOP_K].reshape(-1))[:, None]
    tile_e = jnp.minimum(jnp.searchsorted(pend, jnp.arange(n_tiles, dtype=jnp.int32) * tm, side='right'),
                         ne - 1).astype(jnp.int32)

    xs = _gather_rows(h2, row_tok)

    wg, wu = _deinterleave(w_gate_up)
    bg = b_gate_up[:, None, 0::2]
    bu = b_gate_up[:, None, 1::2]
    wd = w_down.astype(BF16)
    bd = b_down[:, None, :]
    rows = _experts(xs, row_w, tile_e, n_active, wg, wu, bg, bu, wd, bd, tn_ff, tn_d)
    return _combine(rows, dest, x_mid)


def _alibi_slopes(n):
    return (2.0 ** (-ALIBI_MAX_BIAS * (np.arange(n, dtype=np.float32) + 1.0) / n)).astype(np.float32)


def _mixer(x2d, batch, seq, norm1_g, w_in, q_norm_swa, k_norm_swa, q_norm_dil, k_norm_dil, sinks, w_out,
           n_swa, n_kv, n_dil, tm_proj, tn_out):
    t, d = x2d.shape
    swa_q, swa_kv, dil_w = n_swa * HEAD_DIM, n_kv * HEAD_DIM, n_dil * HEAD_DIM
    assert n_swa // n_kv == GQA_GROUP and seq % (16 * ATTN_BLOCK) == 0
    s1, s2, s3 = swa_q, swa_q + swa_kv, swa_q + 2 * swa_kv

    def dup(wc):
        wc = wc.reshape(d, n_kv, HEAD_DIM)
        return jnp.concatenate([wc, wc], axis=-1).reshape(d, 2 * swa_kv)

    w = jnp.concatenate([w_in[:, :s1], dup(w_in[:, s1:s2]), dup(w_in[:, s2:s3]), w_in[:, s3:]],
                        axis=1).astype(BF16)
    scale = HEAD_DIM ** -0.5
    ones = lambda n: jnp.ones((n,), F32)
    gains = jnp.concatenate([
        jnp.tile(q_norm_swa * scale, n_swa), jnp.tile(k_norm_swa, 2 * n_kv), ones(2 * swa_kv),
        jnp.tile(q_norm_dil * scale, n_dil), jnp.tile(k_norm_dil, n_dil), ones(dil_w)])[None, :]
    c = w.shape[1]
    bounds = np.cumsum([0, swa_q, 2 * swa_kv, 2 * swa_kv, dil_w, dil_w, dil_w])
    assert all(b % PROJ_TN == 0 for b in bounds)
    seg_norm = [1, 1, 0, 1, 1, 0]
    flags = np.zeros((c // PROJ_TN,), np.int32)
    for sidx in range(6):
        flags[bounds[sidx] // PROJ_TN:bounds[sidx + 1] // PROJ_TN] = seg_norm[sidx]
    proj = _inproj(x2d, norm1_g[None, :], w, gains, jnp.asarray(flags), tm_proj)

    qa0, ka0, va0, qb0, kb0, vb0 = (int(b) for b in bounds[:6])
    sink_row = jnp.repeat(sinks.astype(F32), HEAD_DIM)[None, :]
    o_a = _band_attention(proj, jnp.asarray(_alibi_slopes(n_swa)), batch=batch, seq=seq, dilation=1,
                          q_col0=qa0, k_col0=ka0, v_col0=va0, n_heads=n_swa, kv_shared=True,
                          max_diff=SWA_WINDOW - 1, sinks=sink_row)
    slopes_dil = jnp.asarray(_alibi_slopes(n_dil))
    outs, lses = [], []
    for window, dil in DILATED_BRANCHES:
        o_i, lse_i = _band_attention(proj, slopes_dil, batch=batch, seq=seq, dilation=dil,
                                     q_col0=qb0, k_col0=kb0, v_col0=vb0, n_heads=n_dil, kv_shared=False,
                                     max_diff=window // dil)
        outs.append(o_i)
        lses.append(lse_i)
    o_b = _merge(outs, lses)
    return _outproj(o_a, o_b, w_out.astype(BF16), x2d, tm_proj, tn_out)


def kernel(x, norm1_g, w_in, q_norm_swa, k_norm_swa, q_norm_dil, k_norm_dil, sinks, w_out, norm2_g,
           w_router, b_router, w_gate_up, b_gate_up, w_down, b_down):
    b, s, d = x.shape
    depth = norm1_g.shape[0]
    n_heads = d // HEAD_DIM
    n_swa = n_heads // 2
    n_kv = n_swa // GQA_GROUP
    n_dil = n_heads - n_swa
    x2d = x.reshape(b * s, d)
    for l in range(depth):
        x_mid = _mixer(x2d, b, s, norm1_g[l], w_in[l], q_norm_swa[l], k_norm_swa[l], q_norm_dil[l],
                       k_norm_dil[l], sinks[l], w_out[l], n_swa, n_kv, n_dil, tm_proj=512, tn_out=512)
        x2d = _moe(x_mid, norm2_g[l][None, :], w_router[l], b_router[l], w_gate_up[l], b_gate_up[l],
                   w_down[l], b_down[l], tn_ff=512, tn_d=1024)
    return x2d.reshape(b, s, d)
```

```python
import functools

import jax
import jax.numpy as jnp
import numpy as np
from jax import lax
from jax.experimental import pallas as pl
from jax.experimental.pallas import tpu as pltpu

F32 = jnp.float32
BF16 = jnp.bfloat16

HEAD_DIM = 64
LANES = 128
ATTN_BLOCK = 128
GQA_GROUP = 8
SWA_WINDOW = 128
DILATED_BRANCHES = ((128, 1), (512, 4), (2048, 16))
N_EXPERTS = 32
TOP_K = 4
SWIGLU_LIMIT = 7.0
SWIGLU_ALPHA = 1.702
ALIBI_MAX_BIAS = 8.0
EPS = 1e-6
MASK_DIST = 1e30
NEG_BIG = -1e30

VMEM_LIMIT = 52 * 1024 * 1024

PROJ_TN = 512
ATTN_W = 512
EXPERT_TM = 512
GATHER_ROWS = 256
COMBINE_TM = 128


def _cparams(n_axes):
    return pltpu.CompilerParams(dimension_semantics=("arbitrary",) * n_axes,
                                vmem_limit_bytes=VMEM_LIMIT)


RESIDUE_DILATIONS = tuple(d for _, d in DILATED_BRANCHES if d > 1)


def _inproj_kernel(flag_ref, x_ref, g_ref, w_ref, gain_ref, ones_ref, o_ref, *rest, dil_tile0):
    res_refs, (h_scr, y_scr) = rest[:len(RESIDUE_DILATIONS)], rest[len(RESIDUE_DILATIONS):]
    j = pl.program_id(1)
    tm, tn = o_ref.shape

    @pl.when(j == 0)
    def _():
        x = x_ref[...]
        ms = jnp.mean(x * x, axis=-1, keepdims=True)
        h_scr[...] = (x * lax.rsqrt(ms + EPS) * g_ref[...]).astype(BF16)

    acc = jnp.dot(h_scr[...], w_ref[...], preferred_element_type=F32)

    def emit(y):
        o_ref[...] = y.astype(BF16)
        for ch in range(tn // LANES):
            y_scr[ch] = y[:, ch * LANES:(ch + 1) * LANES]

    @pl.when(flag_ref[j] == 1)
    def _():
        ss = jnp.dot((acc * acc).astype(BF16), ones_ref[...], preferred_element_type=F32)
        emit(acc * lax.rsqrt(ss * (1.0 / HEAD_DIM) + EPS) * gain_ref[...])

    @pl.when(flag_ref[j] == 0)
    def _():
        emit(acc)

    @pl.when(j >= dil_tile0)
    def _():
        for d, ref in zip(RESIDUE_DILATIONS, res_refs):
            for r in range(d):
                for ch in range(tn // LANES):
                    rows = y_scr[ch, pl.ds(r, tm // d, stride=d), :]
                    ref[r, :, ch * LANES:(ch + 1) * LANES] = rows.astype(BF16)


def _inproj(x2d, g1, w, gains, flags, tm, dil_col0):
    t, d = x2d.shape
    c = w.shape[1]
    tn = PROJ_TN
    dil_tile0 = dil_col0 // tn
    cb = c - dil_col0
    head_id = np.arange(tn) // HEAD_DIM
    ones_bd = jnp.asarray(head_id[:, None] == head_id[None, :], dtype=BF16)
    res_specs = [pl.BlockSpec((dd, tm // dd, tn), lambda i, j, f: (0, i, jnp.maximum(j - dil_tile0, 0)))
                 for dd in RESIDUE_DILATIONS]
    res_shapes = [jax.ShapeDtypeStruct((dd, t // dd, cb), BF16) for dd in RESIDUE_DILATIONS]
    grid_spec = pltpu.PrefetchScalarGridSpec(
        num_scalar_prefetch=1,
        grid=(t // tm, c // tn),
        in_specs=[
            pl.BlockSpec((tm, d), lambda i, j, f: (i, 0)),
            pl.BlockSpec((1, d), lambda i, j, f: (0, 0)),
            pl.BlockSpec((d, tn), lambda i, j, f: (0, j)),
            pl.BlockSpec((1, tn), lambda i, j, f: (0, j)),
            pl.BlockSpec((tn, tn), lambda i, j, f: (0, 0)),
        ],
        out_specs=[pl.BlockSpec((tm, tn), lambda i, j, f: (i, j))] + res_specs,
        scratch_shapes=[pltpu.VMEM((tm, d), BF16), pltpu.VMEM((tn // LANES, tm, LANES), F32)],
    )
    return pl.pallas_call(
        functools.partial(_inproj_kernel, dil_tile0=dil_tile0), grid_spec=grid_spec,
        out_shape=[jax.ShapeDtypeStruct((t, c), BF16)] + res_shapes,
        compiler_params=_cparams(2), name="inproj",
    )(flags, x2d, g1, w, gains, ones_bd)


def _attn_kernel(slope_ref, q_ref, kp_ref, kc_ref, vp_ref, vc_ref, *rest,
                 n_pairs, kv_shared, max_diff, dist_scale, with_sink):
    if with_sink:
        sink_ref, o_ref = rest
        lse_ref = None
    else:
        o_ref, lse_ref = rest
    blk = pl.program_id(2)
    cg = pl.program_id(3)

    qi = lax.broadcasted_iota(jnp.int32, (ATTN_BLOCK, 2 * ATTN_BLOCK), 0)
    kj = lax.broadcasted_iota(jnp.int32, (ATTN_BLOCK, 2 * ATTN_BLOCK), 1)
    dist = qi + ATTN_BLOCK - kj
    valid = (dist >= 0) & (dist <= max_diff) & ((blk > 0) | (kj >= ATTN_BLOCK))
    dist_m = jnp.where(valid, (dist * dist_scale).astype(F32), MASK_DIST)

    lane = lax.broadcasted_iota(jnp.int32, (ATTN_BLOCK, LANES), 1)
    low = lane < HEAD_DIM

    for p in range(n_pairs):
        cols = slice(p * LANES, (p + 1) * LANES)
        kcols = slice(0, LANES) if kv_shared else cols
        q2 = q_ref[:, cols]
        kk = jnp.concatenate([kp_ref[:, kcols], kc_ref[:, kcols]], axis=0)
        vv = jnp.concatenate([vp_ref[:, kcols], vc_ref[:, kcols]], axis=0)
        outs, lses = [], []
        for hh in range(2):
            slope = slope_ref[cg * (2 * n_pairs) + 2 * p + hh]
            qm = jnp.where(low if hh == 0 else ~low, q2, jnp.zeros_like(q2))
            s = lax.dot_general(qm, kk, (((1,), (1,)), ((), ())),
                                preferred_element_type=F32)
            s = s - slope * dist_m
            m = jnp.max(s, axis=-1, keepdims=True)
            e = jnp.exp(s - m)
            l = jnp.sum(e, axis=-1, keepdims=True)
            o = jnp.dot(e.astype(BF16), vv, preferred_element_type=F32)
            outs.append(o / l)
            lses.append(jnp.broadcast_to(m + jnp.log(l), (ATTN_BLOCK, LANES)))
        o2 = jnp.where(low, outs[0], outs[1])
        lse2 = jnp.where(low, lses[0], lses[1])
        if with_sink:
            o2 = o2 * jax.nn.sigmoid(lse2 - sink_ref[:, cols])
        else:
            lse_ref[:, cols] = lse2
        o_ref[:, cols] = o2.astype(o_ref.dtype)


def _band_attention(proj, slopes, *, batch, seq, dilation, q_col0, k_col0, v_col0, n_heads,
                    kv_shared, max_diff, sinks=None):
    d = dilation
    assert proj.shape[0] == d
    lsub = seq // d
    nblk = lsub // ATTN_BLOCK
    w = ATTN_W
    n_pairs = w // LANES
    width = n_heads * HEAD_DIM
    ncg = width // w
    kw = LANES if kv_shared else w
    pv = proj

    def qmap(b, r, i, g, s):
        return (r, b * nblk + i, q_col0 // w + g)

    def kvmap(col0, prev):
        def f(b, r, i, g, s):
            blk = jnp.maximum(i - 1, 0) if prev else i
            return (r, b * nblk + blk, col0 // kw + g)
        return f

    def omap(b, r, i, g, s):
        return (r, b * nblk + i, g)

    in_specs = [
        pl.BlockSpec((None, ATTN_BLOCK, w), qmap),
        pl.BlockSpec((None, ATTN_BLOCK, kw), kvmap(k_col0, True)),
        pl.BlockSpec((None, ATTN_BLOCK, kw), kvmap(k_col0, False)),
        pl.BlockSpec((None, ATTN_BLOCK, kw), kvmap(v_col0, True)),
        pl.BlockSpec((None, ATTN_BLOCK, kw), kvmap(v_col0, False)),
    ]
    args = [pv, pv, pv, pv, pv]
    with_sink = sinks is not None
    o_shape = jax.ShapeDtypeStruct((d, batch * lsub, width), BF16)
    o_spec = pl.BlockSpec((None, ATTN_BLOCK, w), omap)
    if with_sink:
        in_specs.append(pl.BlockSpec((1, w), lambda b, r, i, g, s: (0, g)))
        args.append(sinks)
        out_shape, out_specs = o_shape, o_spec
    else:
        out_shape = (o_shape, jax.ShapeDtypeStruct((d, batch * lsub, width), F32))
        out_specs = (o_spec, pl.BlockSpec((None, ATTN_BLOCK, w), omap))
    kern = functools.partial(_attn_kernel, n_pairs=n_pairs, kv_shared=kv_shared,
                             max_diff=max_diff, dist_scale=d, with_sink=with_sink)
    grid_spec = pltpu.PrefetchScalarGridSpec(
        num_scalar_prefetch=1, grid=(batch, d, nblk, ncg),
        in_specs=in_specs, out_specs=out_specs)
    res = pl.pallas_call(kern, grid_spec=grid_spec, out_shape=out_shape,
                         compiler_params=_cparams(4), name=f"band_attn_d{d}" + ("_sink" if with_sink else ""),
                         )(slopes, *args)
    return res


def _merge_kernel(*refs, dilations):
    n = len(dilations)
    o_refs, l_refs, out, scr = refs[:n], refs[n:2 * n], refs[2 * n], refs[2 * n + 1]
    tm, w = out.shape
    nch = w // LANES
    for a, d in enumerate(dilations):
        if d == 1:
            continue
        for b, ref in enumerate((o_refs[a], l_refs[a])):
            for r in range(d):
                for ch in range(nch):
                    scr[2 * a + b, ch, pl.ds(r, tm // d, stride=d), :] = (
                        ref[r, :, ch * LANES:(ch + 1) * LANES].astype(F32))
    for ch in range(nch):
        cols = slice(ch * LANES, (ch + 1) * LANES)
        os_, ls_ = [], []
        for a, d in enumerate(dilations):
            if d == 1:
                os_.append(o_refs[a][0, :, cols].astype(F32))
                ls_.append(l_refs[a][0, :, cols])
            else:
                os_.append(scr[2 * a, ch])
                ls_.append(scr[2 * a + 1, ch])
        m = functools.reduce(jnp.maximum, ls_)
        es = [jnp.exp(l - m) for l in ls_]
        num = functools.reduce(lambda p, q: p + q, [e * o for e, o in zip(es, os_)])
        den = functools.reduce(lambda p, q: p + q, es)
        out[:, cols] = (num / den).astype(out.dtype)


def _merge(os_, ls_, dilations, tm=256):
    w = os_[0].shape[2]
    t = os_[0].shape[0] * os_[0].shape[1]
    specs = [pl.BlockSpec((d, tm // d, w), lambda i: (0, i, 0)) for d in dilations]
    return pl.pallas_call(
        functools.partial(_merge_kernel, dilations=tuple(dilations)), grid=(t // tm,),
        in_specs=specs + specs, out_specs=pl.BlockSpec((tm, w), lambda i: (i, 0)),
        out_shape=jax.ShapeDtypeStruct((t, w), BF16),
        scratch_shapes=[pltpu.VMEM((2 * len(dilations), w // LANES, tm, LANES), F32)],
        compiler_params=_cparams(1), name="branch_merge",
    )(*os_, *ls_)


def _outproj_kernel(oa_ref, ob_ref, wt_ref, wb_ref, x_ref, o_ref):
    acc = jnp.dot(oa_ref[...], wt_ref[...], preferred_element_type=F32)
    acc = acc + jnp.dot(ob_ref[...], wb_ref[...], preferred_element_type=F32)
    o_ref[...] = x_ref[...] + acc


def _outproj(o_a, o_b, w_out, x2d, tm, tn):
    t, d = x2d.shape
    ha, hb = o_a.shape[1], o_b.shape[1]
    assert ha == hb
    return pl.pallas_call(
        _outproj_kernel, grid=(t // tm, d // tn),
        in_specs=[
            pl.BlockSpec((tm, ha), lambda i, j: (i, 0)),
            pl.BlockSpec((tm, hb), lambda i, j: (i, 0)),
            pl.BlockSpec((ha, tn), lambda i, j: (0, j)),
            pl.BlockSpec((hb, tn), lambda i, j: (1, j)),
            pl.BlockSpec((tm, tn), lambda i, j: (i, j)),
        ],
        out_specs=pl.BlockSpec((tm, tn), lambda i, j: (i, j)),
        out_shape=jax.ShapeDtypeStruct((t, d), F32), compiler_params=_cparams(2), name="outproj",
    )(o_a, o_b, w_out, w_out, x2d)


def _router_kernel(x_ref, g_ref, whi_ref, wlo_ref, b_ref, tri_ref,
                   h_ref, idx_ref, gate_ref, rank_ref, cnt_ref, carry):
    i = pl.program_id(0)

    @pl.when(i == 0)
    def _():
        carry[...] = jnp.zeros_like(carry)

    x = x_ref[...]
    ms = jnp.mean(x * x, axis=-1, keepdims=True)
    h = x * lax.rsqrt(ms + EPS) * g_ref[...]
    h_hi = h.astype(BF16)
    h_lo = (h - h_hi.astype(F32)).astype(BF16)
    u = pltpu.bitcast(h, jnp.uint32)
    r = (u + jnp.uint32(0x7FFF) + ((u >> 16) & jnp.uint32(1))) >> 16
    for s in range(h_ref.shape[1]):
        lo = r[:, 2 * s * LANES:(2 * s + 1) * LANES]
        hi = r[:, (2 * s + 1) * LANES:(2 * s + 2) * LANES]
        h_ref[:, s, :] = lo | (hi << 16)
    logits = (jnp.dot(h_hi, whi_ref[...], preferred_element_type=F32)
              + jnp.dot(h_lo, whi_ref[...], preferred_element_type=F32)
              + jnp.dot(h_hi, wlo_ref[...], preferred_element_type=F32)) + b_ref[...]

    tm = x.shape[0]
    lane = lax.broadcasted_iota(jnp.int32, (tm, LANES), 1).astype(F32)
    work = logits
    multihot = jnp.zeros((tm, LANES), F32)
    vals, idxs = [], []
    for _ in range(TOP_K):
        m = jnp.max(work, axis=-1, keepdims=True)
        ik = jnp.min(jnp.where(work == m, lane, float(LANES)), axis=-1, keepdims=True)
        sel = lane == ik
        work = jnp.where(sel, -jnp.inf, work)
        multihot = jnp.where(sel, 1.0, multihot)
        vals.append(m)
        idxs.append(ik)
    es = [jnp.exp(v - vals[0]) for v in vals]
    denom = es[0] + es[1] + es[2] + es[3]

    cum = jnp.dot(tri_ref[...], multihot.astype(BF16), preferred_element_type=F32) + carry[0:1, :]
    idx_t = jnp.zeros((tm, LANES), F32)
    gate_t = jnp.zeros((tm, LANES), F32)
    rank_t = jnp.zeros((tm, LANES), F32)
    for k in range(TOP_K):
        rk = jnp.sum(jnp.where(lane == idxs[k], cum, 0.0), axis=-1, keepdims=True)
        here = lane == float(k)
        idx_t = jnp.where(here, idxs[k], idx_t)
        gate_t = jnp.where(here, es[k] / denom, gate_t)
        rank_t = jnp.where(here, rk, rank_t)
    idx_ref[...] = idx_t.astype(jnp.int32)
    gate_ref[...] = gate_t
    rank_ref[...] = rank_t.astype(jnp.int32)
    new_carry = carry[0:1, :] + jnp.sum(multihot, axis=0, keepdims=True)
    carry[...] = jnp.broadcast_to(new_carry, carry.shape)
    cnt_ref[...] = jnp.broadcast_to(new_carry, cnt_ref.shape)


def _router(x2d, g2, w_router, b_router, tm=256):
    t, d = x2d.shape
    ne = w_router.shape[1]
    w_pad = jnp.zeros((d, LANES), F32).at[:, :ne].set(w_router)
    w_hi = w_pad.astype(BF16)
    w_lo = (w_pad - w_hi.astype(F32)).astype(BF16)
    b_pad = jnp.full((1, LANES), NEG_BIG, F32).at[0, :ne].set(b_router)
    tri = jnp.asarray(np.tril(np.ones((tm, tm), np.float32), -1), dtype=BF16)
    tile = lambda dt: jax.ShapeDtypeStruct((t, LANES), dt)
    row = pl.BlockSpec((tm, LANES), lambda i: (i, 0))
    const = lambda shape: pl.BlockSpec(shape, lambda i: (0, 0))
    return pl.pallas_call(
        _router_kernel, grid=(t // tm,),
        in_specs=[pl.BlockSpec((tm, d), lambda i: (i, 0)), const((1, d)), const((d, LANES)),
                  const((d, LANES)), const((1, LANES)), const((tm, tm))],
        out_specs=(pl.BlockSpec((tm, d // (2 * LANES), LANES), lambda i: (i, 0, 0)), row, row, row,
                   const((8, LANES))),
        out_shape=(jax.ShapeDtypeStruct((t, d // (2 * LANES), LANES), jnp.uint32),
                   tile(jnp.int32), tile(F32), tile(jnp.int32),
                   jax.ShapeDtypeStruct((8, LANES), F32)),
        scratch_shapes=[pltpu.VMEM((8, LANES), F32)],
        compiler_params=_cparams(1), name="router",
    )(x2d, g2, w_hi, w_lo, b_pad, tri)


INDEX_SLOTS = 3
ROW_SLOTS = 2


def _row_prefetch_ring(idx_hbm, idx_smem, isem, start_rows, wait_rows):
    i = pl.program_id(0)
    n = pl.num_programs(0)

    def idx_copy(step):
        slot = lax.rem(step, INDEX_SLOTS)
        return pltpu.make_async_copy(idx_hbm.at[step], idx_smem.at[slot], isem.at[slot])

    @pl.when(i == 0)
    def _():
        idx_copy(0).start()
        idx_copy(0).wait()
        start_rows(0, 0, 0)

        @pl.when(n > 1)
        def _():
            idx_copy(1).start()

    @pl.when(i + 1 < n)
    def _():
        idx_copy(i + 1).wait()
        start_rows(i + 1, lax.rem(i + 1, INDEX_SLOTS), lax.rem(i + 1, ROW_SLOTS))

    @pl.when(i + 2 < n)
    def _():
        idx_copy(i + 2).start()

    row_slot = lax.rem(i, ROW_SLOTS)
    wait_rows(row_slot)
    return row_slot


def _gather_kernel(tok_hbm, h_hbm, xs_ref, tok_smem, buf, isem, rsem):
    g = GATHER_ROWS
    words = h_hbm.shape[1]

    def start_rows(step, idx_slot, row_slot):
        def body(r, c):
            t = tok_smem[idx_slot, r]
            dst = buf.at[row_slot, pl.ds(pl.multiple_of(r * words, words), words)]
            pltpu.make_async_copy(h_hbm.at[t], dst, rsem.at[row_slot]).start()
            return c
        lax.fori_loop(0, g, body, 0)

    def wait_rows(row_slot):
        pltpu.make_async_copy(buf.at[row_slot], buf.at[row_slot], rsem.at[row_slot]).wait()

    row_slot = _row_prefetch_ring(tok_hbm, tok_smem, isem, start_rows, wait_rows)

    for s in range(words):
        x = buf[row_slot, pl.ds(s, g, stride=words), :]
        lo = pltpu.bitcast(x << 16, F32)
        hi = pltpu.bitcast(x & jnp.uint32(0xFFFF0000), F32)
        xs_ref[:, 2 * s * LANES:(2 * s + 1) * LANES] = lo.astype(BF16)
        xs_ref[:, (2 * s + 1) * LANES:(2 * s + 2) * LANES] = hi.astype(BF16)


def _gather_rows(h_packed, row_tok):
    t, words, _ = h_packed.shape
    r_max = row_tok.shape[0]
    g = GATHER_ROWS
    n_steps = r_max // g
    tok2d = row_tok.reshape(n_steps, g)
    return pl.pallas_call(
        _gather_kernel, grid=(n_steps,),
        in_specs=[pl.BlockSpec(memory_space=pl.ANY), pl.BlockSpec(memory_space=pl.ANY)],
        out_specs=pl.BlockSpec((g, 2 * words * LANES), lambda i: (i, 0)),
        scratch_shapes=[pltpu.SMEM((INDEX_SLOTS, g), jnp.int32),
                        pltpu.VMEM((ROW_SLOTS, g * words, LANES), jnp.uint32),
                        pltpu.SemaphoreType.DMA((INDEX_SLOTS,)), pltpu.SemaphoreType.DMA((ROW_SLOTS,))],
        out_shape=jax.ShapeDtypeStruct((r_max, 2 * words * LANES), BF16),
        compiler_params=_cparams(1), name="row_gather",
    )(tok2d, h_packed)


DEINT_COLS = 512


def _deint_kernel(w_ref, pe_ref, po_ref, g_ref, u_ref):
    w = w_ref[...].astype(BF16)
    g_ref[...] = jnp.dot(w, pe_ref[...], preferred_element_type=F32).astype(BF16)
    u_ref[...] = jnp.dot(w, po_ref[...], preferred_element_type=F32).astype(BF16)


def _deinterleave(w, tr=1024):
    ne, d, c2 = w.shape
    tr = min(tr, d)
    tc = min(DEINT_COLS, c2)
    src = np.arange(tc)[:, None]
    dst = np.arange(tc // 2)[None, :]
    p_even = jnp.asarray(src == 2 * dst, dtype=BF16)
    p_odd = jnp.asarray(src == 2 * dst + 1, dtype=BF16)
    const = pl.BlockSpec((tc, tc // 2), lambda e, i, j: (0, 0))
    out = pl.BlockSpec((None, tr, tc // 2), lambda e, i, j: (e, i, j))
    shape = jax.ShapeDtypeStruct((ne, d, c2 // 2), BF16)
    return pl.pallas_call(
        _deint_kernel, grid=(ne, d // tr, c2 // tc),
        in_specs=[pl.BlockSpec((None, tr, tc), lambda e, i, j: (e, i, j)), const, const],
        out_specs=(out, out), out_shape=(shape, shape),
        compiler_params=_cparams(3), name="gate_up_deinterleave",
    )(w, p_even, p_odd)


def _gateup_kernel(te_ref, na_ref, xs_ref, wg_ref, wu_ref, bg_ref, bu_ref, h_ref):
    i = pl.program_id(1)

    @pl.when(i < na_ref[0])
    def _():
        x = xs_ref[...]
        gate = jnp.dot(x, wg_ref[...], preferred_element_type=F32) + bg_ref[...]
        up = jnp.dot(x, wu_ref[...], preferred_element_type=F32) + bu_ref[...]
        gate = jnp.minimum(gate, SWIGLU_LIMIT)
        up = jnp.clip(up, -SWIGLU_LIMIT, SWIGLU_LIMIT)
        act = (up + 1.0) * gate * jax.nn.sigmoid(SWIGLU_ALPHA * gate)
        h_ref[...] = act.astype(h_ref.dtype)

    @pl.when(i >= na_ref[0])
    def _():
        h_ref[...] = jnp.zeros_like(h_ref)


def _down_kernel(te_ref, na_ref, h_ref, wd_ref, bd_ref, rw_ref, o_ref):
    i = pl.program_id(1)

    @pl.when(i < na_ref[0])
    def _():
        out = (jnp.dot(h_ref[...], wd_ref[...], preferred_element_type=F32) + bd_ref[...]) * rw_ref[...]
        for c in range(o_ref.shape[1]):
            o_ref[:, c, :] = out[:, c * LANES:(c + 1) * LANES]

    @pl.when(i >= na_ref[0])
    def _():
        o_ref[...] = jnp.zeros_like(o_ref)


def _tile_clamp(i, na):
    return jnp.minimum(i, na[0] - 1)


def _experts(xs, row_w, tile_e, n_active, wg, wu, bg, bu, wd, bd, tn_ff, tn_d):
    r_max, d = xs.shape
    ff = wg.shape[2]
    tm = EXPERT_TM
    n_tiles = r_max // tm
    gu_spec = pltpu.PrefetchScalarGridSpec(
        num_scalar_prefetch=2, grid=(ff // tn_ff, n_tiles),
        in_specs=[
            pl.BlockSpec((tm, d), lambda j, i, te, na: (_tile_clamp(i, na), 0)),
            pl.BlockSpec((None, d, tn_ff), lambda j, i, te, na: (te[_tile_clamp(i, na)], 0, j)),
            pl.BlockSpec((None, d, tn_ff), lambda j, i, te, na: (te[_tile_clamp(i, na)], 0, j)),
            pl.BlockSpec((None, 1, tn_ff), lambda j, i, te, na: (te[_tile_clamp(i, na)], 0, j)),
            pl.BlockSpec((None, 1, tn_ff), lambda j, i, te, na: (te[_tile_clamp(i, na)], 0, j)),
        ],
        out_specs=pl.BlockSpec((tm, tn_ff), lambda j, i, te, na: (i, j)))
    hidden = pl.pallas_call(
        _gateup_kernel, grid_spec=gu_spec, out_shape=jax.ShapeDtypeStruct((r_max, ff), BF16),
        compiler_params=_cparams(2), name="expert_gate_up",
    )(tile_e, n_active, xs, wg, wu, bg, bu)
    dn_spec = pltpu.PrefetchScalarGridSpec(
        num_scalar_prefetch=2, grid=(d // tn_d, n_tiles),
        in_specs=[
            pl.BlockSpec((tm, ff), lambda j, i, te, na: (_tile_clamp(i, na), 0)),
            pl.BlockSpec((None, ff, tn_d), lambda j, i, te, na: (te[_tile_clamp(i, na)], 0, j)),
            pl.BlockSpec((None, 1, tn_d), lambda j, i, te, na: (te[_tile_clamp(i, na)], 0, j)),
            pl.BlockSpec((tm, 1), lambda j, i, te, na: (_tile_clamp(i, na), 0)),
        ],
        out_specs=pl.BlockSpec((tm, tn_d // LANES, LANES), lambda j, i, te, na: (i, j, 0)))
    return pl.pallas_call(
        _down_kernel, grid_spec=dn_spec, out_shape=jax.ShapeDtypeStruct((r_max, d // LANES, LANES), F32),
        compiler_params=_cparams(2), name="expert_down",
    )(tile_e, n_active, hidden, wd, bd, row_w)


def _combine_kernel(dest_hbm, rows_hbm, x_ref, o_ref, dest_smem, buf, sum_scr, isem, rsem):
    tm = COMBINE_TM
    chunks = rows_hbm.shape[1]
    seg = tm * chunks

    def start_rows(step, idx_slot, row_slot):
        def body(r, c):
            for k in range(TOP_K):
                src = dest_smem[idx_slot, r * TOP_K + k]
                off = pl.multiple_of((k * tm + r) * chunks, chunks)
                pltpu.make_async_copy(rows_hbm.at[src], buf.at[row_slot, pl.ds(off, chunks)],
                                      rsem.at[row_slot]).start()
            return c
        lax.fori_loop(0, tm, body, 0)

    def wait_rows(row_slot):
        pltpu.make_async_copy(buf.at[row_slot], buf.at[row_slot], rsem.at[row_slot]).wait()

    row_slot = _row_prefetch_ring(dest_hbm, dest_smem, isem, start_rows, wait_rows)

    acc = buf[row_slot, pl.ds(0, seg), :]
    for k in range(1, TOP_K):
        acc = acc + buf[row_slot, pl.ds(k * seg, seg), :]
    sum_scr[...] = acc
    for c in range(chunks):
        cols = slice(c * LANES, (c + 1) * LANES)
        o_ref[:, cols] = x_ref[:, cols] + sum_scr[pl.ds(c, tm, stride=chunks), :]


def _combine(rows, dest, x2d):
    t, d = x2d.shape
    chunks = rows.shape[1]
    tm = COMBINE_TM
    n_steps = t // tm
    dest2d = dest.reshape(n_steps, tm * TOP_K)
    blk = pl.BlockSpec((tm, d), lambda i: (i, 0))
    return pl.pallas_call(
        _combine_kernel, grid=(n_steps,),
        in_specs=[pl.BlockSpec(memory_space=pl.ANY), pl.BlockSpec(memory_space=pl.ANY), blk],
        out_specs=blk,
        out_shape=jax.ShapeDtypeStruct((t, d), F32),
        scratch_shapes=[pltpu.SMEM((INDEX_SLOTS, tm * TOP_K), jnp.int32),
                        pltpu.VMEM((ROW_SLOTS, TOP_K * tm * chunks, LANES), F32),
                        pltpu.VMEM((tm * chunks, LANES), F32),
                        pltpu.SemaphoreType.DMA((INDEX_SLOTS,)), pltpu.SemaphoreType.DMA((ROW_SLOTS,))],
        compiler_params=_cparams(1), name="combine",
    )(dest2d, rows, x2d)


def _moe(x_mid, g2, w_router, b_router, w_gate_up, b_gate_up, w_down, b_down, tn_ff, tn_d):
    t, d = x_mid.shape
    ne = w_router.shape[1]
    h2, idx_t, gate_t, rank_t, cnt = _router(x_mid, g2, w_router, b_router)

    tm = EXPERT_TM
    r_max = t * TOP_K + ne * tm
    n_tiles = r_max // tm
    counts = cnt[0, :ne].astype(jnp.int32)
    padded = (counts + tm - 1) // tm * tm
    pend = jnp.cumsum(padded)
    pstart = pend - padded
    idx = idx_t[:, :TOP_K]
    dest = (pstart[idx] + rank_t[:, :TOP_K]).astype(jnp.int32)
    n_rows_used = pend[-1:].astype(jnp.int32)
    n_active = n_rows_used // tm
    tok = jnp.repeat(jnp.arange(t, dtype=jnp.int32), TOP_K)
    gate_bits = lax.bitcast_convert_type(gate_t[:, :TOP_K].reshape(-1), jnp.int32)
    table = jnp.zeros((r_max, 2), jnp.int32).at[dest.reshape(-1)].set(
        jnp.stack([tok, gate_bits], axis=1), unique_indices=True)
    row_tok = table[:, 0]
    row_w = lax.bitcast_convert_type(table[:, 1], F32)[:, None]
    tile_e = jnp.minimum(jnp.searchsorted(pend, jnp.arange(n_tiles, dtype=jnp.int32) * tm, side='right'),
                         ne - 1).astype(jnp.int32)

    xs = _gather_rows(h2, row_tok)

    wg, wu = _deinterleave(w_gate_up)
    bg = b_gate_up[:, None, 0::2]
    bu = b_gate_up[:, None, 1::2]
    wd = w_down.astype(BF16)
    bd = b_down[:, None, :]
    rows = _experts(xs, row_w, tile_e, n_active, wg, wu, bg, bu, wd, bd, tn_ff, tn_d)
    return _combine(rows, dest, x_mid)


def _alibi_slopes(n):
    return (2.0 ** (-ALIBI_MAX_BIAS * (np.arange(n, dtype=np.float32) + 1.0) / n)).astype(np.float32)


def _mixer(x2d, batch, seq, norm1_g, w_in, q_norm_swa, k_norm_swa, q_norm_dil, k_norm_dil, sinks, w_out,
           n_swa, n_kv, n_dil, tm_proj, tn_out):
    t, d = x2d.shape
    swa_q, swa_kv, dil_w = n_swa * HEAD_DIM, n_kv * HEAD_DIM, n_dil * HEAD_DIM
    assert n_swa // n_kv == GQA_GROUP and seq % (16 * ATTN_BLOCK) == 0
    s1, s2, s3 = swa_q, swa_q + swa_kv, swa_q + 2 * swa_kv

    def dup(wc):
        wc = wc.reshape(d, n_kv, HEAD_DIM)
        return jnp.concatenate([wc, wc], axis=-1).reshape(d, 2 * swa_kv)

    w = jnp.concatenate([w_in[:, :s1], dup(w_in[:, s1:s2]), dup(w_in[:, s2:s3]), w_in[:, s3:]],
                        axis=1).astype(BF16)
    scale = HEAD_DIM ** -0.5
    ones = lambda n: jnp.ones((n,), F32)
    gains = jnp.concatenate([
        jnp.tile(q_norm_swa * scale, n_swa), jnp.tile(k_norm_swa, 2 * n_kv), ones(2 * swa_kv),
        jnp.tile(q_norm_dil * scale, n_dil), jnp.tile(k_norm_dil, n_dil), ones(dil_w)])[None, :]
    c = w.shape[1]
    bounds = np.cumsum([0, swa_q, 2 * swa_kv, 2 * swa_kv, dil_w, dil_w, dil_w])
    assert all(b % PROJ_TN == 0 for b in bounds)
    seg_norm = [1, 1, 0, 1, 1, 0]
    flags = np.zeros((c // PROJ_TN,), np.int32)
    for sidx in range(6):
        flags[bounds[sidx] // PROJ_TN:bounds[sidx + 1] // PROJ_TN] = seg_norm[sidx]
    qa0, ka0, va0, qb0, kb0, vb0 = (int(b) for b in bounds[:6])
    proj, *residue_major = _inproj(x2d, norm1_g[None, :], w, gains, jnp.asarray(flags), tm_proj, dil_col0=qb0)
    by_dilation = dict(zip(RESIDUE_DILATIONS, residue_major))

    sink_row = jnp.repeat(sinks.astype(F32), HEAD_DIM)[None, :]
    o_a = _band_attention(proj[None], jnp.asarray(_alibi_slopes(n_swa)), batch=batch, seq=seq, dilation=1,
                          q_col0=qa0, k_col0=ka0, v_col0=va0, n_heads=n_swa, kv_shared=True,
                          max_diff=SWA_WINDOW - 1, sinks=sink_row)[0]
    slopes_dil = jnp.asarray(_alibi_slopes(n_dil))
    outs, lses = [], []
    for window, dil in DILATED_BRANCHES:
        src, col0 = (proj[None], qb0) if dil == 1 else (by_dilation[dil], 0)
        o_i, lse_i = _band_attention(src, slopes_dil, batch=batch, seq=seq, dilation=dil,
                                     q_col0=col0, k_col0=col0 + kb0 - qb0, v_col0=col0 + vb0 - qb0,
                                     n_heads=n_dil, kv_shared=False, max_diff=window // dil)
        outs.append(o_i)
        lses.append(lse_i)
    o_b = _merge(outs, lses, [dil for _, dil in DILATED_BRANCHES])
    return _outproj(o_a, o_b, w_out.astype(BF16), x2d, tm_proj, tn_out)


def kernel(x, norm1_g, w_in, q_norm_swa, k_norm_swa, q_norm_dil, k_norm_dil, sinks, w_out, norm2_g,
           w_router, b_router, w_gate_up, b_gate_up, w_down, b_down):
    b, s, d = x.shape
    depth = norm1_g.shape[0]
    n_heads = d // HEAD_DIM
    n_swa = n_heads // 2
    n_kv = n_swa // GQA_GROUP
    n_dil = n_heads - n_swa
    x2d = x.reshape(b * s, d)
    for l in range(depth):
        x_mid = _mixer(x2d, b, s, norm1_g[l], w_in[l], q_norm_swa[l], k_norm_swa[l], q_norm_dil[l],
                       k_norm_dil[l], sinks[l], w_out[l], n_swa, n_kv, n_dil, tm_proj=512, tn_out=512)
        x2d = _moe(x_mid, norm2_g[l][None, :], w_router[l], b_router[l], w_gate_up[l], b_gate_up[l],
                   w_down[l], b_down[l], tn_ff=512, tn_d=1024)
    return x2d.reshape(b, s, d)
```

```python
import functools

import jax
import jax.numpy as jnp
import numpy as np
from jax import lax
from jax.experimental import pallas as pl
from jax.experimental.pallas import tpu as pltpu

F32 = jnp.float32
BF16 = jnp.bfloat16

HEAD_DIM = 64
LANES = 128
ATTN_BLOCK = 128
GQA_GROUP = 8
SWA_WINDOW = 128
DILATED_BRANCHES = ((128, 1), (512, 4), (2048, 16))
N_EXPERTS = 32
TOP_K = 4
SWIGLU_LIMIT = 7.0
SWIGLU_ALPHA = 1.702
ALIBI_MAX_BIAS = 8.0
EPS = 1e-6
MASK_DIST = 1e30
NEG_BIG = -1e30

VMEM_LIMIT = 52 * 1024 * 1024

PROJ_TN = 512
ATTN_W = 512
EXPERT_TM = 512
GATHER_ROWS = 256
COMBINE_TM = 128


def _cparams(n_axes):
    return pltpu.CompilerParams(dimension_semantics=("arbitrary",) * n_axes,
                                vmem_limit_bytes=VMEM_LIMIT)


RESIDUE_DILATIONS = tuple(d for _, d in DILATED_BRANCHES if d > 1)


def _inproj_kernel(flag_ref, x_ref, g_ref, w_ref, gain_ref, ones_ref, o_ref, *rest, dil_tile0):
    res_refs, (h_scr, y_scr) = rest[:len(RESIDUE_DILATIONS)], rest[len(RESIDUE_DILATIONS):]
    j = pl.program_id(1)
    tm, tn = o_ref.shape

    @pl.when(j == 0)
    def _():
        x = x_ref[...]
        ms = jnp.mean(x * x, axis=-1, keepdims=True)
        h_scr[...] = (x * lax.rsqrt(ms + EPS) * g_ref[...]).astype(BF16)

    acc = jnp.dot(h_scr[...], w_ref[...], preferred_element_type=F32)

    def emit(y):
        o_ref[...] = y.astype(BF16)
        for ch in range(tn // LANES):
            y_scr[ch] = y[:, ch * LANES:(ch + 1) * LANES]

    @pl.when(flag_ref[j] == 1)
    def _():
        ss = jnp.dot((acc * acc).astype(BF16), ones_ref[...], preferred_element_type=F32)
        emit(acc * lax.rsqrt(ss * (1.0 / HEAD_DIM) + EPS) * gain_ref[...])

    @pl.when(flag_ref[j] == 0)
    def _():
        emit(acc)

    @pl.when(j >= dil_tile0)
    def _():
        for d, ref in zip(RESIDUE_DILATIONS, res_refs):
            for r in range(d):
                for ch in range(tn // LANES):
                    rows = y_scr[ch, pl.ds(r, tm // d, stride=d), :]
                    ref[r, :, ch * LANES:(ch + 1) * LANES] = rows.astype(BF16)


def _inproj(x2d, g1, w, gains, flags, tm, dil_col0):
    t, d = x2d.shape
    c = w.shape[1]
    tn = PROJ_TN
    dil_tile0 = dil_col0 // tn
    cb = c - dil_col0
    head_id = np.arange(tn) // HEAD_DIM
    ones_bd = jnp.asarray(head_id[:, None] == head_id[None, :], dtype=BF16)
    res_specs = [pl.BlockSpec((dd, tm // dd, tn), lambda i, j, f: (0, i, jnp.maximum(j - dil_tile0, 0)))
                 for dd in RESIDUE_DILATIONS]
    res_shapes = [jax.ShapeDtypeStruct((dd, t // dd, cb), BF16) for dd in RESIDUE_DILATIONS]
    grid_spec = pltpu.PrefetchScalarGridSpec(
        num_scalar_prefetch=1,
        grid=(t // tm, c // tn),
        in_specs=[
            pl.BlockSpec((tm, d), lambda i, j, f: (i, 0)),
            pl.BlockSpec((1, d), lambda i, j, f: (0, 0)),
            pl.BlockSpec((d, tn), lambda i, j, f: (0, j)),
            pl.BlockSpec((1, tn), lambda i, j, f: (0, j)),
            pl.BlockSpec((tn, tn), lambda i, j, f: (0, 0)),
        ],
        out_specs=[pl.BlockSpec((tm, tn), lambda i, j, f: (i, j))] + res_specs,
        scratch_shapes=[pltpu.VMEM((tm, d), BF16), pltpu.VMEM((tn // LANES, tm, LANES), F32)],
    )
    return pl.pallas_call(
        functools.partial(_inproj_kernel, dil_tile0=dil_tile0), grid_spec=grid_spec,
        out_shape=[jax.ShapeDtypeStruct((t, c), BF16)] + res_shapes,
        compiler_params=_cparams(2), name="inproj",
    )(flags, x2d, g1, w, gains, ones_bd)


def _attn_kernel(slope_ref, q_ref, kp_ref, kc_ref, vp_ref, vc_ref, *rest,
                 n_pairs, kv_shared, max_diff, dist_scale, with_sink):
    if with_sink:
        sink_ref, o_ref = rest
        lse_ref = None
    else:
        o_ref, lse_ref = rest
    blk = pl.program_id(2)
    cg = pl.program_id(3)

    qi = lax.broadcasted_iota(jnp.int32, (ATTN_BLOCK, 2 * ATTN_BLOCK), 0)
    kj = lax.broadcasted_iota(jnp.int32, (ATTN_BLOCK, 2 * ATTN_BLOCK), 1)
    dist = qi + ATTN_BLOCK - kj
    valid = (dist >= 0) & (dist <= max_diff) & ((blk > 0) | (kj >= ATTN_BLOCK))
    dist_m = jnp.where(valid, (dist * dist_scale).astype(F32), MASK_DIST)

    lane = lax.broadcasted_iota(jnp.int32, (ATTN_BLOCK, LANES), 1)
    low = lane < HEAD_DIM

    for p in range(n_pairs):
        cols = slice(p * LANES, (p + 1) * LANES)
        kcols = slice(0, LANES) if kv_shared else cols
        q2 = q_ref[:, cols]
        kk = jnp.concatenate([kp_ref[:, kcols], kc_ref[:, kcols]], axis=0)
        vv = jnp.concatenate([vp_ref[:, kcols], vc_ref[:, kcols]], axis=0)
        outs, lses = [], []
        for hh in range(2):
            slope = slope_ref[cg * (2 * n_pairs) + 2 * p + hh]
            qm = jnp.where(low if hh == 0 else ~low, q2, jnp.zeros_like(q2))
            s = lax.dot_general(qm, kk, (((1,), (1,)), ((), ())),
                                preferred_element_type=F32)
            s = s - slope * dist_m
            m = jnp.max(s, axis=-1, keepdims=True)
            e = jnp.exp(s - m)
            l = jnp.sum(e, axis=-1, keepdims=True)
            o = jnp.dot(e.astype(BF16), vv, preferred_element_type=F32)
            outs.append(o / l)
            lses.append(jnp.broadcast_to(m + jnp.log(l), (ATTN_BLOCK, LANES)))
        o2 = jnp.where(low, outs[0], outs[1])
        lse2 = jnp.where(low, lses[0], lses[1])
        if with_sink:
            o2 = o2 * jax.nn.sigmoid(lse2 - sink_ref[:, cols])
        else:
            lse_ref[:, cols] = lse2
        o_ref[:, cols] = o2.astype(o_ref.dtype)


def _band_attention(proj, slopes, *, batch, seq, dilation, q_col0, k_col0, v_col0, n_heads,
                    kv_shared, max_diff, sinks=None):
    d = dilation
    assert proj.shape[0] == d
    lsub = seq // d
    nblk = lsub // ATTN_BLOCK
    w = ATTN_W
    n_pairs = w // LANES
    width = n_heads * HEAD_DIM
    ncg = width // w
    kw = LANES if kv_shared else w
    pv = proj

    def qmap(b, r, i, g, s):
        return (r, b * nblk + i, q_col0 // w + g)

    def kvmap(col0, prev):
        def f(b, r, i, g, s):
            blk = jnp.maximum(i - 1, 0) if prev else i
            return (r, b * nblk + blk, col0 // kw + g)
        return f

    def omap(b, r, i, g, s):
        return (r, b * nblk + i, g)

    in_specs = [
        pl.BlockSpec((None, ATTN_BLOCK, w), qmap),
        pl.BlockSpec((None, ATTN_BLOCK, kw), kvmap(k_col0, True)),
        pl.BlockSpec((None, ATTN_BLOCK, kw), kvmap(k_col0, False)),
        pl.BlockSpec((None, ATTN_BLOCK, kw), kvmap(v_col0, True)),
        pl.BlockSpec((None, ATTN_BLOCK, kw), kvmap(v_col0, False)),
    ]
    args = [pv, pv, pv, pv, pv]
    with_sink = sinks is not None
    o_shape = jax.ShapeDtypeStruct((d, batch * lsub, width), BF16)
    o_spec = pl.BlockSpec((None, ATTN_BLOCK, w), omap)
    if with_sink:
        in_specs.append(pl.BlockSpec((1, w), lambda b, r, i, g, s: (0, g)))
        args.append(sinks)
        out_shape, out_specs = o_shape, o_spec
    else:
        out_shape = (o_shape, jax.ShapeDtypeStruct((d, batch * lsub, width), F32))
        out_specs = (o_spec, pl.BlockSpec((None, ATTN_BLOCK, w), omap))
    kern = functools.partial(_attn_kernel, n_pairs=n_pairs, kv_shared=kv_shared,
                             max_diff=max_diff, dist_scale=d, with_sink=with_sink)
    grid_spec = pltpu.PrefetchScalarGridSpec(
        num_scalar_prefetch=1, grid=(batch, d, nblk, ncg),
        in_specs=in_specs, out_specs=out_specs)
    res = pl.pallas_call(kern, grid_spec=grid_spec, out_shape=out_shape,
                         compiler_params=_cparams(4), name=f"band_attn_d{d}" + ("_sink" if with_sink else ""),
                         )(slopes, *args)
    return res


def _merge_kernel(*refs, dilations):
    n = len(dilations)
    o_refs, l_refs, out, scr = refs[:n], refs[n:2 * n], refs[2 * n], refs[2 * n + 1]
    tm, w = out.shape
    nch = w // LANES
    for a, d in enumerate(dilations):
        if d == 1:
            continue
        for b, ref in enumerate((o_refs[a], l_refs[a])):
            for r in range(d):
                for ch in range(nch):
                    scr[2 * a + b, ch, pl.ds(r, tm // d, stride=d), :] = (
                        ref[r, :, ch * LANES:(ch + 1) * LANES].astype(F32))
    for ch in range(nch):
        cols = slice(ch * LANES, (ch + 1) * LANES)
        os_, ls_ = [], []
        for a, d in enumerate(dilations):
            if d == 1:
                os_.append(o_refs[a][0, :, cols].astype(F32))
                ls_.append(l_refs[a][0, :, cols])
            else:
                os_.append(scr[2 * a, ch])
                ls_.append(scr[2 * a + 1, ch])
        m = functools.reduce(jnp.maximum, ls_)
        es = [jnp.exp(l - m) for l in ls_]
        num = functools.reduce(lambda p, q: p + q, [e * o for e, o in zip(es, os_)])
        den = functools.reduce(lambda p, q: p + q, es)
        out[:, cols] = (num / den).astype(out.dtype)


def _merge(os_, ls_, dilations, tm=256):
    w = os_[0].shape[2]
    t = os_[0].shape[0] * os_[0].shape[1]
    specs = [pl.BlockSpec((d, tm // d, w), lambda i: (0, i, 0)) for d in dilations]
    return pl.pallas_call(
        functools.partial(_merge_kernel, dilations=tuple(dilations)), grid=(t // tm,),
        in_specs=specs + specs, out_specs=pl.BlockSpec((tm, w), lambda i: (i, 0)),
        out_shape=jax.ShapeDtypeStruct((t, w), BF16),
        scratch_shapes=[pltpu.VMEM((2 * len(dilations), w // LANES, tm, LANES), F32)],
        compiler_params=_cparams(1), name="branch_merge",
    )(*os_, *ls_)


def _outproj_kernel(oa_ref, ob_ref, wt_ref, wb_ref, x_ref, o_ref):
    acc = jnp.dot(oa_ref[...], wt_ref[...], preferred_element_type=F32)
    acc = acc + jnp.dot(ob_ref[...], wb_ref[...], preferred_element_type=F32)
    o_ref[...] = x_ref[...] + acc


def _outproj(o_a, o_b, w_out, x2d, tm, tn):
    t, d = x2d.shape
    ha, hb = o_a.shape[1], o_b.shape[1]
    assert ha == hb
    return pl.pallas_call(
        _outproj_kernel, grid=(t // tm, d // tn),
        in_specs=[
            pl.BlockSpec((tm, ha), lambda i, j: (i, 0)),
            pl.BlockSpec((tm, hb), lambda i, j: (i, 0)),
            pl.BlockSpec((ha, tn), lambda i, j: (0, j)),
            pl.BlockSpec((hb, tn), lambda i, j: (1, j)),
            pl.BlockSpec((tm, tn), lambda i, j: (i, j)),
        ],
        out_specs=pl.BlockSpec((tm, tn), lambda i, j: (i, j)),
        out_shape=jax.ShapeDtypeStruct((t, d), F32), compiler_params=_cparams(2), name="outproj",
    )(o_a, o_b, w_out, w_out, x2d)


def _router_kernel(x_ref, g_ref, whi_ref, wlo_ref, b_ref, tri_ref,
                   h_ref, idx_ref, gate_ref, rank_ref, cnt_ref, carry):
    i = pl.program_id(0)

    @pl.when(i == 0)
    def _():
        carry[...] = jnp.zeros_like(carry)

    x = x_ref[...]
    ms = jnp.mean(x * x, axis=-1, keepdims=True)
    h = x * lax.rsqrt(ms + EPS) * g_ref[...]
    h_hi = h.astype(BF16)
    h_lo = (h - h_hi.astype(F32)).astype(BF16)
    u = pltpu.bitcast(h, jnp.uint32)
    r = (u + jnp.uint32(0x7FFF) + ((u >> 16) & jnp.uint32(1))) >> 16
    for s in range(h_ref.shape[1]):
        lo = r[:, 2 * s * LANES:(2 * s + 1) * LANES]
        hi = r[:, (2 * s + 1) * LANES:(2 * s + 2) * LANES]
        h_ref[:, s, :] = lo | (hi << 16)
    logits = (jnp.dot(h_hi, whi_ref[...], preferred_element_type=F32)
              + jnp.dot(h_lo, whi_ref[...], preferred_element_type=F32)
              + jnp.dot(h_hi, wlo_ref[...], preferred_element_type=F32)) + b_ref[...]

    tm = x.shape[0]
    lane = lax.broadcasted_iota(jnp.int32, (tm, LANES), 1).astype(F32)
    work = logits
    multihot = jnp.zeros((tm, LANES), F32)
    vals, idxs = [], []
    for _ in range(TOP_K):
        m = jnp.max(work, axis=-1, keepdims=True)
        ik = jnp.min(jnp.where(work == m, lane, float(LANES)), axis=-1, keepdims=True)
        sel = lane == ik
        work = jnp.where(sel, -jnp.inf, work)
        multihot = jnp.where(sel, 1.0, multihot)
        vals.append(m)
        idxs.append(ik)
    es = [jnp.exp(v - vals[0]) for v in vals]
    denom = es[0] + es[1] + es[2] + es[3]

    cum = jnp.dot(tri_ref[...], multihot.astype(BF16), preferred_element_type=F32) + carry[0:1, :]
    idx_t = jnp.zeros((tm, LANES), F32)
    gate_t = jnp.zeros((tm, LANES), F32)
    rank_t = jnp.zeros((tm, LANES), F32)
    for k in range(TOP_K):
        rk = jnp.sum(jnp.where(lane == idxs[k], cum, 0.0), axis=-1, keepdims=True)
        here = lane == float(k)
        idx_t = jnp.where(here, idxs[k], idx_t)
        gate_t = jnp.where(here, es[k] / denom, gate_t)
        rank_t = jnp.where(here, rk, rank_t)
    idx_ref[...] = idx_t.astype(jnp.int32)
    gate_ref[...] = gate_t
    rank_ref[...] = rank_t.astype(jnp.int32)
    new_carry = carry[0:1, :] + jnp.sum(multihot, axis=0, keepdims=True)
    carry[...] = jnp.broadcast_to(new_carry, carry.shape)
    cnt_ref[...] = jnp.broadcast_to(new_carry, cnt_ref.shape)


def _router(x2d, g2, w_router, b_router, tm=256):
    t, d = x2d.shape
    ne = w_router.shape[1]
    w_pad = jnp.zeros((d, LANES), F32).at[:, :ne].set(w_router)
    w_hi = w_pad.astype(BF16)
    w_lo = (w_pad - w_hi.astype(F32)).astype(BF16)
    b_pad = jnp.full((1, LANES), NEG_BIG, F32).at[0, :ne].set(b_router)
    tri = jnp.asarray(np.tril(np.ones((tm, tm), np.float32), -1), dtype=BF16)
    tile = lambda dt: jax.ShapeDtypeStruct((t, LANES), dt)
    row = pl.BlockSpec((tm, LANES), lambda i: (i, 0))
    const = lambda shape: pl.BlockSpec(shape, lambda i: (0, 0))
    return pl.pallas_call(
        _router_kernel, grid=(t // tm,),
        in_specs=[pl.BlockSpec((tm, d), lambda i: (i, 0)), const((1, d)), const((d, LANES)),
                  const((d, LANES)), const((1, LANES)), const((tm, tm))],
        out_specs=(pl.BlockSpec((tm, d // (2 * LANES), LANES), lambda i: (i, 0, 0)), row, row, row,
                   const((8, LANES))),
        out_shape=(jax.ShapeDtypeStruct((t, d // (2 * LANES), LANES), jnp.uint32),
                   tile(jnp.int32), tile(F32), tile(jnp.int32),
                   jax.ShapeDtypeStruct((8, LANES), F32)),
        scratch_shapes=[pltpu.VMEM((8, LANES), F32)],
        compiler_params=_cparams(1), name="router",
    )(x2d, g2, w_hi, w_lo, b_pad, tri)


INDEX_SLOTS = 3
ROW_SLOTS = 2


def _row_prefetch_ring(idx_hbm, idx_smem, isem, start_rows, wait_rows):
    i = pl.program_id(0)
    n = pl.num_programs(0)

    def idx_copy(step):
        slot = lax.rem(step, INDEX_SLOTS)
        return pltpu.make_async_copy(idx_hbm.at[step], idx_smem.at[slot], isem.at[slot])

    @pl.when(i == 0)
    def _():
        idx_copy(0).start()
        idx_copy(0).wait()
        start_rows(0, 0, 0)

        @pl.when(n > 1)
        def _():
            idx_copy(1).start()

    @pl.when(i + 1 < n)
    def _():
        idx_copy(i + 1).wait()
        start_rows(i + 1, lax.rem(i + 1, INDEX_SLOTS), lax.rem(i + 1, ROW_SLOTS))

    @pl.when(i + 2 < n)
    def _():
        idx_copy(i + 2).start()

    row_slot = lax.rem(i, ROW_SLOTS)
    wait_rows(row_slot)
    return row_slot


def _gather_kernel(tok_hbm, h_hbm, xs_ref, tok_smem, buf, isem, rsem):
    g = GATHER_ROWS
    words = h_hbm.shape[1]

    def start_rows(step, idx_slot, row_slot):
        def body(r2, c):
            for prio in range(2):
                r = 2 * r2 + prio
                t = tok_smem[idx_slot, r]
                dst = buf.at[row_slot, pl.ds(pl.multiple_of(r * words, words), words)]
                pltpu.make_async_copy(h_hbm.at[t], dst, rsem.at[row_slot]).start(priority=prio)
            return c
        lax.fori_loop(0, g // 2, body, 0)

    def wait_rows(row_slot):
        pltpu.make_async_copy(buf.at[row_slot], buf.at[row_slot], rsem.at[row_slot]).wait()

    row_slot = _row_prefetch_ring(tok_hbm, tok_smem, isem, start_rows, wait_rows)

    for s in range(words):
        x = buf[row_slot, pl.ds(s, g, stride=words), :]
        lo = pltpu.bitcast(x << 16, F32)
        hi = pltpu.bitcast(x & jnp.uint32(0xFFFF0000), F32)
        xs_ref[:, 2 * s * LANES:(2 * s + 1) * LANES] = lo.astype(BF16)
        xs_ref[:, (2 * s + 1) * LANES:(2 * s + 2) * LANES] = hi.astype(BF16)


def _gather_rows(h_packed, row_tok):
    t, words, _ = h_packed.shape
    r_max = row_tok.shape[0]
    g = GATHER_ROWS
    n_steps = r_max // g
    tok2d = row_tok.reshape(n_steps, g)
    return pl.pallas_call(
        _gather_kernel, grid=(n_steps,),
        in_specs=[pl.BlockSpec(memory_space=pl.ANY), pl.BlockSpec(memory_space=pl.ANY)],
        out_specs=pl.BlockSpec((g, 2 * words * LANES), lambda i: (i, 0)),
        scratch_shapes=[pltpu.SMEM((INDEX_SLOTS, g), jnp.int32),
                        pltpu.VMEM((ROW_SLOTS, g * words, LANES), jnp.uint32),
                        pltpu.SemaphoreType.DMA((INDEX_SLOTS,)), pltpu.SemaphoreType.DMA((ROW_SLOTS,))],
        out_shape=jax.ShapeDtypeStruct((r_max, 2 * words * LANES), BF16),
        compiler_params=_cparams(1), name="row_gather",
    )(tok2d, h_packed)


DEINT_COLS = 512
PREP_ROWS = 1024


def _tile_clamp(i, na):
    return jnp.minimum(i, na[0] - 1)


def _weights_changed(te_ref, na_ref, i):
    cur = _tile_clamp(i, na_ref)
    prev = jnp.maximum(cur - 1, 0)
    return (i < na_ref[0]) & ((i == 0) | (te_ref[cur] != te_ref[prev]))


def _gateup_kernel(te_ref, na_ref, xs_ref, w_ref, pe_ref, po_ref, bg_ref, bu_ref, h_ref, wg_s, wu_s):
    i = pl.program_id(1)
    d, tn = wg_s.shape

    @pl.when(_weights_changed(te_ref, na_ref, i))
    def _():
        for p in range(w_ref.shape[1] // DEINT_COLS):
            for r0 in range(0, d, PREP_ROWS):
                rows = slice(r0, min(r0 + PREP_ROWS, d))
                w = w_ref[rows, p * DEINT_COLS:(p + 1) * DEINT_COLS].astype(BF16)
                cols = slice(p * DEINT_COLS // 2, (p + 1) * DEINT_COLS // 2)
                wg_s[rows, cols] = jnp.dot(w, pe_ref[...], preferred_element_type=F32).astype(BF16)
                wu_s[rows, cols] = jnp.dot(w, po_ref[...], preferred_element_type=F32).astype(BF16)

    @pl.when(i < na_ref[0])
    def _():
        x = xs_ref[...]
        gate = jnp.dot(x, wg_s[...], preferred_element_type=F32) + bg_ref[...]
        up = jnp.dot(x, wu_s[...], preferred_element_type=F32) + bu_ref[...]
        gate = jnp.minimum(gate, SWIGLU_LIMIT)
        up = jnp.clip(up, -SWIGLU_LIMIT, SWIGLU_LIMIT)
        act = (up + 1.0) * gate * jax.nn.sigmoid(SWIGLU_ALPHA * gate)
        h_ref[...] = act.astype(h_ref.dtype)

    @pl.when(i >= na_ref[0])
    def _():
        h_ref[...] = jnp.zeros_like(h_ref)


def _down_kernel(te_ref, na_ref, h_ref, w_ref, bd_ref, rw_ref, o_ref, wd_s):
    i = pl.program_id(1)

    @pl.when(_weights_changed(te_ref, na_ref, i))
    def _():
        ff = wd_s.shape[0]
        for r0 in range(0, ff, PREP_ROWS):
            rows = slice(r0, min(r0 + PREP_ROWS, ff))
            wd_s[rows, :] = w_ref[rows, :].astype(BF16)

    @pl.when(i < na_ref[0])
    def _():
        out = (jnp.dot(h_ref[...], wd_s[...], preferred_element_type=F32) + bd_ref[...]) * rw_ref[...]
        for c in range(o_ref.shape[1]):
            o_ref[:, c, :] = out[:, c * LANES:(c + 1) * LANES]

    @pl.when(i >= na_ref[0])
    def _():
        o_ref[...] = jnp.zeros_like(o_ref)


def _experts(xs, row_w, tile_e, n_active, w_gate_up, bg, bu, w_down, bd, tn_ff, tn_d):
    r_max, d = xs.shape
    ff = w_down.shape[1]
    tm = EXPERT_TM
    n_tiles = r_max // tm
    dc = min(DEINT_COLS, 2 * tn_ff)
    src = np.arange(dc)[:, None]
    dst = np.arange(dc // 2)[None, :]
    p_even = jnp.asarray(src == 2 * dst, dtype=BF16)
    p_odd = jnp.asarray(src == 2 * dst + 1, dtype=BF16)
    sel_spec = pl.BlockSpec((dc, dc // 2), lambda j, i, te, na: (0, 0))
    gu_spec = pltpu.PrefetchScalarGridSpec(
        num_scalar_prefetch=2, grid=(ff // tn_ff, n_tiles),
        in_specs=[
            pl.BlockSpec((tm, d), lambda j, i, te, na: (_tile_clamp(i, na), 0)),
            pl.BlockSpec((None, d, 2 * tn_ff), lambda j, i, te, na: (te[_tile_clamp(i, na)], 0, j)),
            sel_spec, sel_spec,
            pl.BlockSpec((None, 1, tn_ff), lambda j, i, te, na: (te[_tile_clamp(i, na)], 0, j)),
            pl.BlockSpec((None, 1, tn_ff), lambda j, i, te, na: (te[_tile_clamp(i, na)], 0, j)),
        ],
        out_specs=pl.BlockSpec((tm, tn_ff), lambda j, i, te, na: (i, j)),
        scratch_shapes=[pltpu.VMEM((d, tn_ff), BF16), pltpu.VMEM((d, tn_ff), BF16)])
    hidden = pl.pallas_call(
        _gateup_kernel, grid_spec=gu_spec, out_shape=jax.ShapeDtypeStruct((r_max, ff), BF16),
        compiler_params=_cparams(2), name="expert_gate_up",
    )(tile_e, n_active, xs, w_gate_up, p_even, p_odd, bg, bu)
    dn_spec = pltpu.PrefetchScalarGridSpec(
        num_scalar_prefetch=2, grid=(d // tn_d, n_tiles),
        in_specs=[
            pl.BlockSpec((tm, ff), lambda j, i, te, na: (_tile_clamp(i, na), 0)),
            pl.BlockSpec((None, ff, tn_d), lambda j, i, te, na: (te[_tile_clamp(i, na)], 0, j)),
            pl.BlockSpec((None, 1, tn_d), lambda j, i, te, na: (te[_tile_clamp(i, na)], 0, j)),
            pl.BlockSpec((tm, 1), lambda j, i, te, na: (_tile_clamp(i, na), 0)),
        ],
        out_specs=pl.BlockSpec((tm, tn_d // LANES, LANES), lambda j, i, te, na: (i, j, 0)),
        scratch_shapes=[pltpu.VMEM((ff, tn_d), BF16)])
    return pl.pallas_call(
        _down_kernel, grid_spec=dn_spec, out_shape=jax.ShapeDtypeStruct((r_max, d // LANES, LANES), F32),
        compiler_params=_cparams(2), name="expert_down",
    )(tile_e, n_active, hidden, w_down, bd, row_w)


def _combine_kernel(dest_hbm, rows_hbm, x_ref, o_ref, dest_smem, buf, sum_scr, isem, rsem):
    tm = COMBINE_TM
    chunks = rows_hbm.shape[1]
    seg = tm * chunks

    def start_rows(step, idx_slot, row_slot):
        def body(r, c):
            for k in range(TOP_K):
                src = dest_smem[idx_slot, r * TOP_K + k]
                off = pl.multiple_of((k * tm + r) * chunks, chunks)
                pltpu.make_async_copy(rows_hbm.at[src], buf.at[row_slot, pl.ds(off, chunks)],
                                      rsem.at[row_slot]).start()
            return c
        lax.fori_loop(0, tm, body, 0)

    def wait_rows(row_slot):
        pltpu.make_async_copy(buf.at[row_slot], buf.at[row_slot], rsem.at[row_slot]).wait()

    row_slot = _row_prefetch_ring(dest_hbm, dest_smem, isem, start_rows, wait_rows)

    acc = buf[row_slot, pl.ds(0, seg), :]
    for k in range(1, TOP_K):
        acc = acc + buf[row_slot, pl.ds(k * seg, seg), :]
    sum_scr[...] = acc
    for c in range(chunks):
        cols = slice(c * LANES, (c + 1) * LANES)
        o_ref[:, cols] = x_ref[:, cols] + sum_scr[pl.ds(c, tm, stride=chunks), :]


def _combine(rows, dest, x2d):
    t, d = x2d.shape
    chunks = rows.shape[1]
    tm = COMBINE_TM
    n_steps = t // tm
    dest2d = dest.reshape(n_steps, tm * TOP_K)
    blk = pl.BlockSpec((tm, d), lambda i: (i, 0))
    return pl.pallas_call(
        _combine_kernel, grid=(n_steps,),
        in_specs=[pl.BlockSpec(memory_space=pl.ANY), pl.BlockSpec(memory_space=pl.ANY), blk],
        out_specs=blk,
        out_shape=jax.ShapeDtypeStruct((t, d), F32),
        scratch_shapes=[pltpu.SMEM((INDEX_SLOTS, tm * TOP_K), jnp.int32),
                        pltpu.VMEM((ROW_SLOTS, TOP_K * tm * chunks, LANES), F32),
                        pltpu.VMEM((tm * chunks, LANES), F32),
                        pltpu.SemaphoreType.DMA((INDEX_SLOTS,)), pltpu.SemaphoreType.DMA((ROW_SLOTS,))],
        compiler_params=_cparams(1), name="combine",
    )(dest2d, rows, x2d)


def _moe(x_mid, g2, w_router, b_router, w_gate_up, b_gate_up, w_down, b_down, tn_ff, tn_d):
    t, d = x_mid.shape
    ne = w_router.shape[1]
    h2, idx_t, gate_t, rank_t, cnt = _router(x_mid, g2, w_router, b_router)

    tm = EXPERT_TM
    r_max = t * TOP_K + ne * tm
    n_tiles = r_max // tm
    counts = cnt[0, :ne].astype(jnp.int32)
    padded = (counts + tm - 1) // tm * tm
    pend = jnp.cumsum(padded)
    pstart = pend - padded
    idx = idx_t[:, :TOP_K]
    dest = (pstart[idx] + rank_t[:, :TOP_K]).astype(jnp.int32)
    n_rows_used = pend[-1:].astype(jnp.int32)
    n_active = n_rows_used // tm
    tok = jnp.repeat(jnp.arange(t, dtype=jnp.int32), TOP_K)
    gate_bits = lax.bitcast_convert_type(gate_t[:, :TOP_K].reshape(-1), jnp.int32)
    table = jnp.zeros((r_max, 2), jnp.int32).at[dest.reshape(-1)].set(
        jnp.stack([tok, gate_bits], axis=1), unique_indices=True)
    row_tok = table[:, 0]
    row_w = lax.bitcast_convert_type(table[:, 1], F32)[:, None]
    tile_e = jnp.minimum(jnp.searchsorted(pend, jnp.arange(n_tiles, dtype=jnp.int32) * tm, side='right'),
                         ne - 1).astype(jnp.int32)

    xs = _gather_rows(h2, row_tok)

    bg = b_gate_up[:, None, 0::2]
    bu = b_gate_up[:, None, 1::2]
    bd = b_down[:, None, :]
    rows = _experts(xs, row_w, tile_e, n_active, w_gate_up, bg, bu, w_down, bd, tn_ff, tn_d)
    return _combine(rows, dest, x_mid)


def _alibi_slopes(n):
    return (2.0 ** (-ALIBI_MAX_BIAS * (np.arange(n, dtype=np.float32) + 1.0) / n)).astype(np.float32)


def _mixer(x2d, batch, seq, norm1_g, w_in, q_norm_swa, k_norm_swa, q_norm_dil, k_norm_dil, sinks, w_out,
           n_swa, n_kv, n_dil, tm_proj, tn_out):
    t, d = x2d.shape
    swa_q, swa_kv, dil_w = n_swa * HEAD_DIM, n_kv * HEAD_DIM, n_dil * HEAD_DIM
    assert n_swa // n_kv == GQA_GROUP and seq % (16 * ATTN_BLOCK) == 0
    s1, s2, s3 = swa_q, swa_q + swa_kv, swa_q + 2 * swa_kv

    def dup(wc):
        wc = wc.reshape(d, n_kv, HEAD_DIM)
        return jnp.concatenate([wc, wc], axis=-1).reshape(d, 2 * swa_kv)

    w = jnp.concatenate([w_in[:, :s1], dup(w_in[:, s1:s2]), dup(w_in[:, s2:s3]), w_in[:, s3:]],
                        axis=1).astype(BF16)
    scale = HEAD_DIM ** -0.5
    ones = lambda n: jnp.ones((n,), F32)
    gains = jnp.concatenate([
        jnp.tile(q_norm_swa * scale, n_swa), jnp.tile(k_norm_swa, 2 * n_kv), ones(2 * swa_kv),
        jnp.tile(q_norm_dil * scale, n_dil), jnp.tile(k_norm_dil, n_dil), ones(dil_w)])[None, :]
    c = w.shape[1]
    bounds = np.cumsum([0, swa_q, 2 * swa_kv, 2 * swa_kv, dil_w, dil_w, dil_w])
    assert all(b % PROJ_TN == 0 for b in bounds)
    seg_norm = [1, 1, 0, 1, 1, 0]
    flags = np.zeros((c // PROJ_TN,), np.int32)
    for sidx in range(6):
        flags[bounds[sidx] // PROJ_TN:bounds[sidx + 1] // PROJ_TN] = seg_norm[sidx]
    qa0, ka0, va0, qb0, kb0, vb0 = (int(b) for b in bounds[:6])
    proj, *residue_major = _inproj(x2d, norm1_g[None, :], w, gains, jnp.asarray(flags), tm_proj, dil_col0=qb0)
    by_dilation = dict(zip(RESIDUE_DILATIONS, residue_major))

    sink_row = jnp.repeat(sinks.astype(F32), HEAD_DIM)[None, :]
    o_a = _band_attention(proj[None], jnp.asarray(_alibi_slopes(n_swa)), batch=batch, seq=seq, dilation=1,
                          q_col0=qa0, k_col0=ka0, v_col0=va0, n_heads=n_swa, kv_shared=True,
                          max_diff=SWA_WINDOW - 1, sinks=sink_row)[0]
    slopes_dil = jnp.asarray(_alibi_slopes(n_dil))
    outs, lses = [], []
    for window, dil in DILATED_BRANCHES:
        src, col0 = (proj[None], qb0) if dil == 1 else (by_dilation[dil], 0)
        o_i, lse_i = _band_attention(src, slopes_dil, batch=batch, seq=seq, dilation=dil,
                                     q_col0=col0, k_col0=col0 + kb0 - qb0, v_col0=col0 + vb0 - qb0,
                                     n_heads=n_dil, kv_shared=False, max_diff=window // dil)
        outs.append(o_i)
        lses.append(lse_i)
    o_b = _merge(outs, lses, [dil for _, dil in DILATED_BRANCHES])
    return _outproj(o_a, o_b, w_out.astype(BF16), x2d, tm_proj, tn_out)


def kernel(x, norm1_g, w_in, q_norm_swa, k_norm_swa, q_norm_dil, k_norm_dil, sinks, w_out, norm2_g,
           w_router, b_router, w_gate_up, b_gate_up, w_down, b_down):
    b, s, d = x.shape
    depth = norm1_g.shape[0]
    n_heads = d // HEAD_DIM
    n_swa = n_heads // 2
    n_kv = n_swa // GQA_GROUP
    n_dil = n_heads - n_swa
    x2d = x.reshape(b * s, d)
    for l in range(depth):
        x_mid = _mixer(x2d, b, s, norm1_g[l], w_in[l], q_norm_swa[l], k_norm_swa[l], q_norm_dil[l],
                       k_norm_dil[l], sinks[l], w_out[l], n_swa, n_kv, n_dil, tm_proj=512, tn_out=512)
        x2d = _moe(x_mid, norm2_g[l][None, :], w_router[l], b_router[l], w_gate_up[l], b_gate_up[l],
                   w_down[l], b_down[l], tn_ff=256, tn_d=1024)
    return x2d.reshape(b, s, d)
```

```python
import functools

import jax
import jax.numpy as jnp
import numpy as np
from jax import lax
from jax.experimental import pallas as pl
from jax.experimental.pallas import tpu as pltpu

F32 = jnp.float32
BF16 = jnp.bfloat16

HEAD_DIM = 64
LANES = 128
ATTN_BLOCK = 128
GQA_GROUP = 8
SWA_WINDOW = 128
DILATED_BRANCHES = ((128, 1), (512, 4), (2048, 16))
N_EXPERTS = 32
TOP_K = 4
SWIGLU_LIMIT = 7.0
SWIGLU_ALPHA = 1.702
ALIBI_MAX_BIAS = 8.0
EPS = 1e-6
MASK_DIST = 1e30
NEG_BIG = -1e30

VMEM_LIMIT = 52 * 1024 * 1024

PROJ_TN = 512
ATTN_W = 512
EXPERT_TM = 512
GATHER_ROWS = 256
COMBINE_TM = 128


def _cparams(n_axes):
    return pltpu.CompilerParams(dimension_semantics=("arbitrary",) * n_axes,
                                vmem_limit_bytes=VMEM_LIMIT)


RESIDUE_DILATIONS = tuple(d for _, d in DILATED_BRANCHES if d > 1)


def _inproj_kernel(flag_ref, x_ref, g_ref, w_ref, gain_ref, ones_ref, o_ref, *rest, dil_tile0):
    res_refs, (h_scr, y_scr) = rest[:len(RESIDUE_DILATIONS)], rest[len(RESIDUE_DILATIONS):]
    j = pl.program_id(1)
    tm, tn = o_ref.shape

    @pl.when(j == 0)
    def _():
        x = x_ref[...]
        ms = jnp.mean(x * x, axis=-1, keepdims=True)
        h_scr[...] = (x * lax.rsqrt(ms + EPS) * g_ref[...]).astype(BF16)

    acc = jnp.dot(h_scr[...], w_ref[...], preferred_element_type=F32)

    def emit(y):
        o_ref[...] = y.astype(BF16)
        for ch in range(tn // LANES):
            y_scr[ch] = y[:, ch * LANES:(ch + 1) * LANES]

    @pl.when(flag_ref[j] == 1)
    def _():
        ss = jnp.dot((acc * acc).astype(BF16), ones_ref[...], preferred_element_type=F32)
        emit(acc * lax.rsqrt(ss * (1.0 / HEAD_DIM) + EPS) * gain_ref[...])

    @pl.when(flag_ref[j] == 0)
    def _():
        emit(acc)

    @pl.when(j >= dil_tile0)
    def _():
        for d, ref in zip(RESIDUE_DILATIONS, res_refs):
            for r in range(d):
                for ch in range(tn // LANES):
                    rows = y_scr[ch, pl.ds(r, tm // d, stride=d), :]
                    ref[r, :, ch * LANES:(ch + 1) * LANES] = rows.astype(BF16)


def _inproj(x2d, g1, w, gains, flags, tm, dil_col0):
    t, d = x2d.shape
    c = w.shape[1]
    tn = PROJ_TN
    dil_tile0 = dil_col0 // tn
    cb = c - dil_col0
    head_id = np.arange(tn) // HEAD_DIM
    ones_bd = jnp.asarray(head_id[:, None] == head_id[None, :], dtype=BF16)
    res_specs = [pl.BlockSpec((dd, tm // dd, tn), lambda i, j, f: (0, i, jnp.maximum(j - dil_tile0, 0)))
                 for dd in RESIDUE_DILATIONS]
    res_shapes = [jax.ShapeDtypeStruct((dd, t // dd, cb), BF16) for dd in RESIDUE_DILATIONS]
    grid_spec = pltpu.PrefetchScalarGridSpec(
        num_scalar_prefetch=1,
        grid=(t // tm, c // tn),
        in_specs=[
            pl.BlockSpec((tm, d), lambda i, j, f: (i, 0)),
            pl.BlockSpec((1, d), lambda i, j, f: (0, 0)),
            pl.BlockSpec((d, tn), lambda i, j, f: (0, j)),
            pl.BlockSpec((1, tn), lambda i, j, f: (0, j)),
            pl.BlockSpec((tn, tn), lambda i, j, f: (0, 0)),
        ],
        out_specs=[pl.BlockSpec((tm, tn), lambda i, j, f: (i, j))] + res_specs,
        scratch_shapes=[pltpu.VMEM((tm, d), BF16), pltpu.VMEM((tn // LANES, tm, LANES), F32)],
    )
    return pl.pallas_call(
        functools.partial(_inproj_kernel, dil_tile0=dil_tile0), grid_spec=grid_spec,
        out_shape=[jax.ShapeDtypeStruct((t, c), BF16)] + res_shapes,
        compiler_params=_cparams(2), name="inproj",
    )(flags, x2d, g1, w, gains, ones_bd)


def _attn_kernel(slope_ref, q_ref, kp_ref, kc_ref, vp_ref, vc_ref, *rest,
                 n_pairs, kv_shared, max_diff, dist_scale, with_sink):
    if with_sink:
        sink_ref, o_ref = rest
        lse_ref = None
    else:
        o_ref, lse_ref = rest
    blk = pl.program_id(2)
    cg = pl.program_id(3)

    qi = lax.broadcasted_iota(jnp.int32, (ATTN_BLOCK, 2 * ATTN_BLOCK), 0)
    kj = lax.broadcasted_iota(jnp.int32, (ATTN_BLOCK, 2 * ATTN_BLOCK), 1)
    dist = qi + ATTN_BLOCK - kj
    valid = (dist >= 0) & (dist <= max_diff) & ((blk > 0) | (kj >= ATTN_BLOCK))
    dist_m = jnp.where(valid, (dist * dist_scale).astype(F32), MASK_DIST)

    lane = lax.broadcasted_iota(jnp.int32, (ATTN_BLOCK, LANES), 1)
    low = lane < HEAD_DIM

    for p in range(n_pairs):
        cols = slice(p * LANES, (p + 1) * LANES)
        kcols = slice(0, LANES) if kv_shared else cols
        q2 = q_ref[:, cols]
        kk = jnp.concatenate([kp_ref[:, kcols], kc_ref[:, kcols]], axis=0)
        vv = jnp.concatenate([vp_ref[:, kcols], vc_ref[:, kcols]], axis=0)
        outs, lses = [], []
        for hh in range(2):
            slope = slope_ref[cg * (2 * n_pairs) + 2 * p + hh]
            qm = jnp.where(low if hh == 0 else ~low, q2, jnp.zeros_like(q2))
            s = lax.dot_general(qm, kk, (((1,), (1,)), ((), ())),
                                preferred_element_type=F32)
            s = s - slope * dist_m
            m = jnp.max(s, axis=-1, keepdims=True)
            e = jnp.exp(s - m)
            l = jnp.sum(e, axis=-1, keepdims=True)
            o = jnp.dot(e.astype(BF16), vv, preferred_element_type=F32)
            outs.append(o / l)
            lses.append(jnp.broadcast_to(m + jnp.log(l), (ATTN_BLOCK, LANES)))
        o2 = jnp.where(low, outs[0], outs[1])
        lse2 = jnp.where(low, lses[0], lses[1])
        if with_sink:
            o2 = o2 * jax.nn.sigmoid(lse2 - sink_ref[:, cols])
        else:
            lse_ref[:, cols] = lse2
        o_ref[:, cols] = o2.astype(o_ref.dtype)


def _band_attention(proj, slopes, *, batch, seq, dilation, q_col0, k_col0, v_col0, n_heads,
                    kv_shared, max_diff, sinks=None):
    d = dilation
    assert proj.shape[0] == d
    lsub = seq // d
    nblk = lsub // ATTN_BLOCK
    w = ATTN_W
    n_pairs = w // LANES
    width = n_heads * HEAD_DIM
    ncg = width // w
    kw = LANES if kv_shared else w
    pv = proj

    def qmap(b, r, i, g, s):
        return (r, b * nblk + i, q_col0 // w + g)

    def kvmap(col0, prev):
        def f(b, r, i, g, s):
            blk = jnp.maximum(i - 1, 0) if prev else i
            return (r, b * nblk + blk, col0 // kw + g)
        return f

    def omap(b, r, i, g, s):
        return (r, b * nblk + i, g)

    in_specs = [
        pl.BlockSpec((None, ATTN_BLOCK, w), qmap),
        pl.BlockSpec((None, ATTN_BLOCK, kw), kvmap(k_col0, True)),
        pl.BlockSpec((None, ATTN_BLOCK, kw), kvmap(k_col0, False)),
        pl.BlockSpec((None, ATTN_BLOCK, kw), kvmap(v_col0, True)),
        pl.BlockSpec((None, ATTN_BLOCK, kw), kvmap(v_col0, False)),
    ]
    args = [pv, pv, pv, pv, pv]
    with_sink = sinks is not None
    o_shape = jax.ShapeDtypeStruct((d, batch * lsub, width), BF16)
    o_spec = pl.BlockSpec((None, ATTN_BLOCK, w), omap)
    if with_sink:
        in_specs.append(pl.BlockSpec((1, w), lambda b, r, i, g, s: (0, g)))
        args.append(sinks)
        out_shape, out_specs = o_shape, o_spec
    else:
        out_shape = (o_shape, jax.ShapeDtypeStruct((d, batch * lsub, width), F32))
        out_specs = (o_spec, pl.BlockSpec((None, ATTN_BLOCK, w), omap))
    kern = functools.partial(_attn_kernel, n_pairs=n_pairs, kv_shared=kv_shared,
                             max_diff=max_diff, dist_scale=d, with_sink=with_sink)
    grid_spec = pltpu.PrefetchScalarGridSpec(
        num_scalar_prefetch=1, grid=(batch, d, nblk, ncg),
        in_specs=in_specs, out_specs=out_specs)
    res = pl.pallas_call(kern, grid_spec=grid_spec, out_shape=out_shape,
                         compiler_params=_cparams(4), name=f"band_attn_d{d}" + ("_sink" if with_sink else ""),
                         )(slopes, *args)
    return res


def _merge_kernel(*refs, dilations):
    n = len(dilations)
    o_refs, l_refs, out, scr = refs[:n], refs[n:2 * n], refs[2 * n], refs[2 * n + 1]
    tm, w = out.shape
    nch = w // LANES
    for a, d in enumerate(dilations):
        if d == 1:
            continue
        for b, ref in enumerate((o_refs[a], l_refs[a])):
            for r in range(d):
                for ch in range(nch):
                    scr[2 * a + b, ch, pl.ds(r, tm // d, stride=d), :] = (
                        ref[r, :, ch * LANES:(ch + 1) * LANES].astype(F32))
    for ch in range(nch):
        cols = slice(ch * LANES, (ch + 1) * LANES)
        os_, ls_ = [], []
        for a, d in enumerate(dilations):
            if d == 1:
                os_.append(o_refs[a][0, :, cols].astype(F32))
                ls_.append(l_refs[a][0, :, cols])
            else:
                os_.append(scr[2 * a, ch])
                ls_.append(scr[2 * a + 1, ch])
        m = functools.reduce(jnp.maximum, ls_)
        es = [jnp.exp(l - m) for l in ls_]
        num = functools.reduce(lambda p, q: p + q, [e * o for e, o in zip(es, os_)])
        den = functools.reduce(lambda p, q: p + q, es)
        out[:, cols] = (num / den).astype(out.dtype)


def _merge(os_, ls_, dilations, tm=256):
    w = os_[0].shape[2]
    t = os_[0].shape[0] * os_[0].shape[1]
    specs = [pl.BlockSpec((d, tm // d, w), lambda i: (0, i, 0)) for d in dilations]
    return pl.pallas_call(
        functools.partial(_merge_kernel, dilations=tuple(dilations)), grid=(t // tm,),
        in_specs=specs + specs, out_specs=pl.BlockSpec((tm, w), lambda i: (i, 0)),
        out_shape=jax.ShapeDtypeStruct((t, w), BF16),
        scratch_shapes=[pltpu.VMEM((2 * len(dilations), w // LANES, tm, LANES), F32)],
        compiler_params=_cparams(1), name="branch_merge",
    )(*os_, *ls_)


def _outproj_kernel(oa_ref, ob_ref, wt_ref, wb_ref, x_ref, o_ref):
    acc = jnp.dot(oa_ref[...], wt_ref[...], preferred_element_type=F32)
    acc = acc + jnp.dot(ob_ref[...], wb_ref[...], preferred_element_type=F32)
    o_ref[...] = x_ref[...] + acc


def _outproj(o_a, o_b, w_out, x2d, tm, tn):
    t, d = x2d.shape
    ha, hb = o_a.shape[1], o_b.shape[1]
    assert ha == hb
    return pl.pallas_call(
        _outproj_kernel, grid=(t // tm, d // tn),
        in_specs=[
            pl.BlockSpec((tm, ha), lambda i, j: (i, 0)),
            pl.BlockSpec((tm, hb), lambda i, j: (i, 0)),
            pl.BlockSpec((ha, tn), lambda i, j: (0, j)),
            pl.BlockSpec((hb, tn), lambda i, j: (1, j)),
            pl.BlockSpec((tm, tn), lambda i, j: (i, j)),
        ],
        out_specs=pl.BlockSpec((tm, tn), lambda i, j: (i, j)),
        out_shape=jax.ShapeDtypeStruct((t, d), F32), compiler_params=_cparams(2), name="outproj",
    )(o_a, o_b, w_out, w_out, x2d)


def _router_kernel(x_ref, g_ref, whi_ref, wlo_ref, b_ref, tri_ref,
                   h_ref, idx_ref, gate_ref, rank_ref, cnt_ref, carry, h_scr):
    i = pl.program_id(0)

    @pl.when(i == 0)
    def _():
        carry[...] = jnp.zeros_like(carry)

    x = x_ref[...]
    ms = jnp.mean(x * x, axis=-1, keepdims=True)
    h = x * lax.rsqrt(ms + EPS) * g_ref[...]
    h_hi = h.astype(BF16)
    h_lo = (h - h_hi.astype(F32)).astype(BF16)
    u = pltpu.bitcast(h, jnp.uint32)
    r = (u + jnp.uint32(0x7FFF) + ((u >> 16) & jnp.uint32(1))) >> 16
    tok, words, _ = h_ref.shape
    for s in range(words):
        lo = r[:, 2 * s * LANES:(2 * s + 1) * LANES]
        hi = r[:, (2 * s + 1) * LANES:(2 * s + 2) * LANES]
        h_scr[pl.ds(s, tok, stride=words), :] = lo | (hi << 16)
    h_ref[...] = h_scr[...].reshape(tok, words, LANES)
    logits = (jnp.dot(h_hi, whi_ref[...], preferred_element_type=F32)
              + jnp.dot(h_lo, whi_ref[...], preferred_element_type=F32)
              + jnp.dot(h_hi, wlo_ref[...], preferred_element_type=F32)) + b_ref[...]

    tm = x.shape[0]
    lane = lax.broadcasted_iota(jnp.int32, (tm, LANES), 1).astype(F32)
    work = logits
    multihot = jnp.zeros((tm, LANES), F32)
    vals, idxs = [], []
    for _ in range(TOP_K):
        m = jnp.max(work, axis=-1, keepdims=True)
        ik = jnp.min(jnp.where(work == m, lane, float(LANES)), axis=-1, keepdims=True)
        sel = lane == ik
        work = jnp.where(sel, -jnp.inf, work)
        multihot = jnp.where(sel, 1.0, multihot)
        vals.append(m)
        idxs.append(ik)
    es = [jnp.exp(v - vals[0]) for v in vals]
    denom = es[0] + es[1] + es[2] + es[3]

    cum = jnp.dot(tri_ref[...], multihot.astype(BF16), preferred_element_type=F32) + carry[0:1, :]
    idx_t = jnp.zeros((tm, LANES), F32)
    gate_t = jnp.zeros((tm, LANES), F32)
    rank_t = jnp.zeros((tm, LANES), F32)
    for k in range(TOP_K):
        rk = jnp.sum(jnp.where(lane == idxs[k], cum, 0.0), axis=-1, keepdims=True)
        here = lane == float(k)
        idx_t = jnp.where(here, idxs[k], idx_t)
        gate_t = jnp.where(here, es[k] / denom, gate_t)
        rank_t = jnp.where(here, rk, rank_t)
    idx_ref[...] = idx_t.astype(jnp.int32)
    gate_ref[...] = gate_t
    rank_ref[...] = rank_t.astype(jnp.int32)
    new_carry = carry[0:1, :] + jnp.sum(multihot, axis=0, keepdims=True)
    carry[...] = jnp.broadcast_to(new_carry, carry.shape)
    cnt_ref[...] = jnp.broadcast_to(new_carry, cnt_ref.shape)


def _router(x2d, g2, w_router, b_router, tm=256):
    t, d = x2d.shape
    ne = w_router.shape[1]
    w_pad = jnp.zeros((d, LANES), F32).at[:, :ne].set(w_router)
    w_hi = w_pad.astype(BF16)
    w_lo = (w_pad - w_hi.astype(F32)).astype(BF16)
    b_pad = jnp.full((1, LANES), NEG_BIG, F32).at[0, :ne].set(b_router)
    tri = jnp.asarray(np.tril(np.ones((tm, tm), np.float32), -1), dtype=BF16)
    tile = lambda dt: jax.ShapeDtypeStruct((t, LANES), dt)
    row = pl.BlockSpec((tm, LANES), lambda i: (i, 0))
    const = lambda shape: pl.BlockSpec(shape, lambda i: (0, 0))
    return pl.pallas_call(
        _router_kernel, grid=(t // tm,),
        in_specs=[pl.BlockSpec((tm, d), lambda i: (i, 0)), const((1, d)), const((d, LANES)),
                  const((d, LANES)), const((1, LANES)), const((tm, tm))],
        out_specs=(pl.BlockSpec((tm, d // (2 * LANES), LANES), lambda i: (i, 0, 0)), row, row, row,
                   const((8, LANES))),
        out_shape=(jax.ShapeDtypeStruct((t, d // (2 * LANES), LANES), jnp.uint32),
                   tile(jnp.int32), tile(F32), tile(jnp.int32),
                   jax.ShapeDtypeStruct((8, LANES), F32)),
        scratch_shapes=[pltpu.VMEM((8, LANES), F32), pltpu.VMEM((tm * (d // (2 * LANES)), LANES), jnp.uint32)],
        compiler_params=_cparams(1), name="router",
    )(x2d, g2, w_hi, w_lo, b_pad, tri)


INDEX_SLOTS = 3
ROW_SLOTS = 2


def _row_prefetch_ring(idx_hbm, idx_smem, isem, start_rows, wait_rows):
    i = pl.program_id(0)
    n = pl.num_programs(0)

    def idx_copy(step):
        slot = lax.rem(step, INDEX_SLOTS)
        return pltpu.make_async_copy(idx_hbm.at[step], idx_smem.at[slot], isem.at[slot])

    @pl.when(i == 0)
    def _():
        idx_copy(0).start()
        idx_copy(0).wait()
        start_rows(0, 0, 0)

        @pl.when(n > 1)
        def _():
            idx_copy(1).start()

    @pl.when(i + 1 < n)
    def _():
        idx_copy(i + 1).wait()
        start_rows(i + 1, lax.rem(i + 1, INDEX_SLOTS), lax.rem(i + 1, ROW_SLOTS))

    @pl.when(i + 2 < n)
    def _():
        idx_copy(i + 2).start()

    row_slot = lax.rem(i, ROW_SLOTS)
    wait_rows(row_slot)
    return row_slot


def _gather_kernel(tok_hbm, h_hbm, xs_ref, tok_smem, buf, isem, rsem):
    g = GATHER_ROWS
    words = h_hbm.shape[1]

    def start_rows(step, idx_slot, row_slot):
        def body(r2, c):
            for prio in range(2):
                r = 2 * r2 + prio
                t = tok_smem[idx_slot, r]
                dst = buf.at[row_slot, pl.ds(pl.multiple_of(r * words, words), words)]
                pltpu.make_async_copy(h_hbm.at[t], dst, rsem.at[row_slot]).start(priority=prio)
            return c
        lax.fori_loop(0, g // 2, body, 0)

    def wait_rows(row_slot):
        pltpu.make_async_copy(buf.at[row_slot], buf.at[row_slot], rsem.at[row_slot]).wait()

    row_slot = _row_prefetch_ring(tok_hbm, tok_smem, isem, start_rows, wait_rows)

    for s in range(words):
        x = buf[row_slot, pl.ds(s, g, stride=words), :]
        lo = pltpu.bitcast(x << 16, F32)
        hi = pltpu.bitcast(x & jnp.uint32(0xFFFF0000), F32)
        xs_ref[:, 2 * s * LANES:(2 * s + 1) * LANES] = lo.astype(BF16)
        xs_ref[:, (2 * s + 1) * LANES:(2 * s + 2) * LANES] = hi.astype(BF16)


def _gather_rows(h_packed, row_tok):
    t, words, _ = h_packed.shape
    r_max = row_tok.shape[0]
    g = GATHER_ROWS
    n_steps = r_max // g
    tok2d = row_tok.reshape(n_steps, g)
    return pl.pallas_call(
        _gather_kernel, grid=(n_steps,),
        in_specs=[pl.BlockSpec(memory_space=pl.ANY), pl.BlockSpec(memory_space=pl.ANY)],
        out_specs=pl.BlockSpec((g, 2 * words * LANES), lambda i: (i, 0)),
        scratch_shapes=[pltpu.SMEM((INDEX_SLOTS, g), jnp.int32),
                        pltpu.VMEM((ROW_SLOTS, g * words, LANES), jnp.uint32),
                        pltpu.SemaphoreType.DMA((INDEX_SLOTS,)), pltpu.SemaphoreType.DMA((ROW_SLOTS,))],
        out_shape=jax.ShapeDtypeStruct((r_max, 2 * words * LANES), BF16),
        compiler_params=_cparams(1), name="row_gather",
    )(tok2d, h_packed)


PREP_ROWS = 1024


def _tile_clamp(i, na):
    return jnp.minimum(i, na[0] - 1)


def _weights_changed(te_ref, na_ref, i):
    cur = _tile_clamp(i, na_ref)
    prev = jnp.maximum(cur - 1, 0)
    return (i < na_ref[0]) & ((i == 0) | (te_ref[cur] != te_ref[prev]))


def _gateup_kernel(te_ref, na_ref, xs_ref, w_ref, b_ref, pe_ref, h_ref, w_s):
    i = pl.program_id(1)
    d, tc = w_s.shape

    @pl.when(_weights_changed(te_ref, na_ref, i))
    def _():
        for r0 in range(0, d, PREP_ROWS):
            rows = slice(r0, min(r0 + PREP_ROWS, d))
            w_s[rows, :] = w_ref[rows, :].astype(BF16)

    @pl.when(i < na_ref[0])
    def _():
        gu = jnp.dot(xs_ref[...], w_s[...], preferred_element_type=F32) + b_ref[...]
        acts = []
        for c in range(tc // LANES):
            g = gu[:, c * LANES:(c + 1) * LANES]
            u = pltpu.roll(g, LANES - 1, axis=1)
            gate = jnp.minimum(g, SWIGLU_LIMIT)
            up = jnp.clip(u, -SWIGLU_LIMIT, SWIGLU_LIMIT)
            acts.append(((up + 1.0) * gate * jax.nn.sigmoid(SWIGLU_ALPHA * gate)).astype(BF16))
        act = jnp.concatenate(acts, axis=1)
        h_ref[...] = jnp.dot(act, pe_ref[...], preferred_element_type=F32).astype(h_ref.dtype)

    @pl.when(i >= na_ref[0])
    def _():
        h_ref[...] = jnp.zeros_like(h_ref)


def _down_kernel(te_ref, na_ref, h_ref, w_ref, bd_ref, rw_ref, o_ref, wd_s, o_scr):
    i = pl.program_id(1)

    @pl.when(_weights_changed(te_ref, na_ref, i))
    def _():
        ff = wd_s.shape[0]
        for r0 in range(0, ff, PREP_ROWS):
            rows = slice(r0, min(r0 + PREP_ROWS, ff))
            wd_s[rows, :] = w_ref[rows, :].astype(BF16)

    @pl.when(i < na_ref[0])
    def _():
        out = (jnp.dot(h_ref[...], wd_s[...], preferred_element_type=F32) + bd_ref[...]) * rw_ref[...]
        tm, chunks, _ = o_ref.shape
        for c in range(chunks):
            o_scr[pl.ds(c, tm, stride=chunks), :] = out[:, c * LANES:(c + 1) * LANES]
        o_ref[...] = o_scr[...].reshape(tm, chunks, LANES)

    @pl.when(i >= na_ref[0])
    def _():
        o_ref[...] = jnp.zeros_like(o_ref)


def _experts(xs, row_w, tile_e, n_active, w_gate_up, b_gate_up, w_down, bd, tn_ff, tn_d):
    r_max, d = xs.shape
    ff = w_down.shape[1]
    tm = EXPERT_TM
    n_tiles = r_max // tm
    tc = 2 * tn_ff
    p_even = jnp.asarray(np.arange(tc)[:, None] == 2 * np.arange(tn_ff)[None, :], dtype=BF16)
    gu_spec = pltpu.PrefetchScalarGridSpec(
        num_scalar_prefetch=2, grid=(ff // tn_ff, n_tiles),
        in_specs=[
            pl.BlockSpec((tm, d), lambda j, i, te, na: (_tile_clamp(i, na), 0)),
            pl.BlockSpec((None, d, tc), lambda j, i, te, na: (te[_tile_clamp(i, na)], 0, j)),
            pl.BlockSpec((None, 1, tc), lambda j, i, te, na: (te[_tile_clamp(i, na)], 0, j)),
            pl.BlockSpec((tc, tn_ff), lambda j, i, te, na: (0, 0)),
        ],
        out_specs=pl.BlockSpec((tm, tn_ff), lambda j, i, te, na: (i, j)),
        scratch_shapes=[pltpu.VMEM((d, tc), BF16)])
    hidden = pl.pallas_call(
        _gateup_kernel, grid_spec=gu_spec, out_shape=jax.ShapeDtypeStruct((r_max, ff), BF16),
        compiler_params=_cparams(2), name="expert_gate_up",
    )(tile_e, n_active, xs, w_gate_up, b_gate_up, p_even)
    dn_spec = pltpu.PrefetchScalarGridSpec(
        num_scalar_prefetch=2, grid=(d // tn_d, n_tiles),
        in_specs=[
            pl.BlockSpec((tm, ff), lambda j, i, te, na: (_tile_clamp(i, na), 0)),
            pl.BlockSpec((None, ff, tn_d), lambda j, i, te, na: (te[_tile_clamp(i, na)], 0, j)),
            pl.BlockSpec((None, 1, tn_d), lambda j, i, te, na: (te[_tile_clamp(i, na)], 0, j)),
            pl.BlockSpec((tm, 1), lambda j, i, te, na: (_tile_clamp(i, na), 0)),
        ],
        out_specs=pl.BlockSpec((tm, tn_d // LANES, LANES), lambda j, i, te, na: (i, j, 0)),
        scratch_shapes=[pltpu.VMEM((ff, tn_d), BF16), pltpu.VMEM((tm * (tn_d // LANES), LANES), F32)])
    return pl.pallas_call(
        _down_kernel, grid_spec=dn_spec, out_shape=jax.ShapeDtypeStruct((r_max, d // LANES, LANES), F32),
        compiler_params=_cparams(2), name="expert_down",
    )(tile_e, n_active, hidden, w_down, bd, row_w)


def _combine_kernel(dest_hbm, rows_hbm, x_ref, o_ref, dest_smem, buf, sum_scr, isem, rsem):
    tm = COMBINE_TM
    chunks = rows_hbm.shape[1]
    seg = tm * chunks

    def start_rows(step, idx_slot, row_slot):
        def body(r, c):
            for k in range(TOP_K):
                src = dest_smem[idx_slot, r * TOP_K + k]
                off = pl.multiple_of((k * tm + r) * chunks, chunks)
                pltpu.make_async_copy(rows_hbm.at[src], buf.at[row_slot, pl.ds(off, chunks)],
                                      rsem.at[row_slot]).start()
            return c
        lax.fori_loop(0, tm, body, 0)

    def wait_rows(row_slot):
        pltpu.make_async_copy(buf.at[row_slot], buf.at[row_slot], rsem.at[row_slot]).wait()

    row_slot = _row_prefetch_ring(dest_hbm, dest_smem, isem, start_rows, wait_rows)

    acc = buf[row_slot, pl.ds(0, seg), :]
    for k in range(1, TOP_K):
        acc = acc + buf[row_slot, pl.ds(k * seg, seg), :]
    sum_scr[...] = acc
    for c in range(chunks):
        cols = slice(c * LANES, (c + 1) * LANES)
        o_ref[:, cols] = x_ref[:, cols] + sum_scr[pl.ds(c, tm, stride=chunks), :]


def _combine(rows, dest, x2d):
    t, d = x2d.shape
    chunks = rows.shape[1]
    tm = COMBINE_TM
    n_steps = t // tm
    dest2d = dest.reshape(n_steps, tm * TOP_K)
    blk = pl.BlockSpec((tm, d), lambda i: (i, 0))
    return pl.pallas_call(
        _combine_kernel, grid=(n_steps,),
        in_specs=[pl.BlockSpec(memory_space=pl.ANY), pl.BlockSpec(memory_space=pl.ANY), blk],
        out_specs=blk,
        out_shape=jax.ShapeDtypeStruct((t, d), F32),
        scratch_shapes=[pltpu.SMEM((INDEX_SLOTS, tm * TOP_K), jnp.int32),
                        pltpu.VMEM((ROW_SLOTS, TOP_K * tm * chunks, LANES), F32),
                        pltpu.VMEM((tm * chunks, LANES), F32),
                        pltpu.SemaphoreType.DMA((INDEX_SLOTS,)), pltpu.SemaphoreType.DMA((ROW_SLOTS,))],
        compiler_params=_cparams(1), name="combine",
    )(dest2d, rows, x2d)


def _moe(x_mid, g2, w_router, b_router, w_gate_up, b_gate_up, w_down, b_down, tn_ff, tn_d):
    t, d = x_mid.shape
    ne = w_router.shape[1]
    h2, idx_t, gate_t, rank_t, cnt = _router(x_mid, g2, w_router, b_router)

    tm = EXPERT_TM
    r_max = t * TOP_K + ne * tm
    n_tiles = r_max // tm
    counts = cnt[0, :ne].astype(jnp.int32)
    padded = (counts + tm - 1) // tm * tm
    pend = jnp.cumsum(padded)
    pstart = pend - padded
    idx = idx_t[:, :TOP_K]
    dest = (pstart[idx] + rank_t[:, :TOP_K]).astype(jnp.int32)
    n_rows_used = pend[-1:].astype(jnp.int32)
    n_active = n_rows_used // tm
    tok = jnp.repeat(jnp.arange(t, dtype=jnp.int32), TOP_K)
    gate_bits = lax.bitcast_convert_type(gate_t[:, :TOP_K].reshape(-1), jnp.int32)
    table = jnp.zeros((r_max, 2), jnp.int32).at[dest.reshape(-1)].set(
        jnp.stack([tok, gate_bits], axis=1), unique_indices=True)
    row_tok = table[:, 0]
    row_w = lax.bitcast_convert_type(table[:, 1], F32)[:, None]
    tile_e = jnp.minimum(jnp.searchsorted(pend, jnp.arange(n_tiles, dtype=jnp.int32) * tm, side='right'),
                         ne - 1).astype(jnp.int32)

    xs = _gather_rows(h2, row_tok)

    rows = _experts(xs, row_w, tile_e, n_active, w_gate_up, b_gate_up[:, None, :], w_down, b_down[:, None, :],
                    tn_ff, tn_d)
    return _combine(rows, dest, x_mid)


def _alibi_slopes(n):
    return (2.0 ** (-ALIBI_MAX_BIAS * (np.arange(n, dtype=np.float32) + 1.0) / n)).astype(np.float32)


def _mixer(x2d, batch, seq, norm1_g, w_in, q_norm_swa, k_norm_swa, q_norm_dil, k_norm_dil, sinks, w_out,
           n_swa, n_kv, n_dil, tm_proj, tn_out):
    t, d = x2d.shape
    swa_q, swa_kv, dil_w = n_swa * HEAD_DIM, n_kv * HEAD_DIM, n_dil * HEAD_DIM
    assert n_swa // n_kv == GQA_GROUP and seq % (16 * ATTN_BLOCK) == 0
    s1, s2, s3 = swa_q, swa_q + swa_kv, swa_q + 2 * swa_kv

    def dup(wc):
        wc = wc.reshape(d, n_kv, HEAD_DIM)
        return jnp.concatenate([wc, wc], axis=-1).reshape(d, 2 * swa_kv)

    w = jnp.concatenate([w_in[:, :s1], dup(w_in[:, s1:s2]), dup(w_in[:, s2:s3]), w_in[:, s3:]],
                        axis=1).astype(BF16)
    scale = HEAD_DIM ** -0.5
    ones = lambda n: jnp.ones((n,), F32)
    gains = jnp.concatenate([
        jnp.tile(q_norm_swa * scale, n_swa), jnp.tile(k_norm_swa, 2 * n_kv), ones(2 * swa_kv),
        jnp.tile(q_norm_dil * scale, n_dil), jnp.tile(k_norm_dil, n_dil), ones(dil_w)])[None, :]
    c = w.shape[1]
    bounds = np.cumsum([0, swa_q, 2 * swa_kv, 2 * swa_kv, dil_w, dil_w, dil_w])
    assert all(b % PROJ_TN == 0 for b in bounds)
    seg_norm = [1, 1, 0, 1, 1, 0]
    flags = np.zeros((c // PROJ_TN,), np.int32)
    for sidx in range(6):
        flags[bounds[sidx] // PROJ_TN:bounds[sidx + 1] // PROJ_TN] = seg_norm[sidx]
    qa0, ka0, va0, qb0, kb0, vb0 = (int(b) for b in bounds[:6])
    proj, *residue_major = _inproj(x2d, norm1_g[None, :], w, gains, jnp.asarray(flags), tm_proj, dil_col0=qb0)
    by_dilation = dict(zip(RESIDUE_DILATIONS, residue_major))

    sink_row = jnp.repeat(sinks.astype(F32), HEAD_DIM)[None, :]
    o_a = _band_attention(proj[None], jnp.asarray(_alibi_slopes(n_swa)), batch=batch, seq=seq, dilation=1,
                          q_col0=qa0, k_col0=ka0, v_col0=va0, n_heads=n_swa, kv_shared=True,
                          max_diff=SWA_WINDOW - 1, sinks=sink_row)[0]
    slopes_dil = jnp.asarray(_alibi_slopes(n_dil))
    outs, lses = [], []
    for window, dil in DILATED_BRANCHES:
        src, col0 = (proj[None], qb0) if dil == 1 else (by_dilation[dil], 0)
        o_i, lse_i = _band_attention(src, slopes_dil, batch=batch, seq=seq, dilation=dil,
                                     q_col0=col0, k_col0=col0 + kb0 - qb0, v_col0=col0 + vb0 - qb0,
                                     n_heads=n_dil, kv_shared=False, max_diff=window // dil)
        outs.append(o_i)
        lses.append(lse_i)
    o_b = _merge(outs, lses, [dil for _, dil in DILATED_BRANCHES])
    return _outproj(o_a, o_b, w_out.astype(BF16), x2d, tm_proj, tn_out)


def kernel(x, norm1_g, w_in, q_norm_swa, k_norm_swa, q_norm_dil, k_norm_dil, sinks, w_out, norm2_g,
           w_router, b_router, w_gate_up, b_gate_up, w_down, b_down):
    b, s, d = x.shape
    depth = norm1_g.shape[0]
    n_heads = d // HEAD_DIM
    n_swa = n_heads // 2
    n_kv = n_swa // GQA_GROUP
    n_dil = n_heads - n_swa
    x2d = x.reshape(b * s, d)
    for l in range(depth):
        x_mid = _mixer(x2d, b, s, norm1_g[l], w_in[l], q_norm_swa[l], k_norm_swa[l], q_norm_dil[l],
                       k_norm_dil[l], sinks[l], w_out[l], n_swa, n_kv, n_dil, tm_proj=512, tn_out=512)
        x2d = _moe(x_mid, norm2_g[l][None, :], w_router[l], b_router[l], w_gate_up[l], b_gate_up[l],
                   w_down[l], b_down[l], tn_ff=256, tn_d=1024)
    return x2d.reshape(b, s, d)
```

```python
import functools

import jax
import jax.numpy as jnp
import numpy as np
from jax import lax
from jax.experimental import pallas as pl
from jax.experimental.pallas import tpu as pltpu

F32 = jnp.float32
BF16 = jnp.bfloat16

HEAD_DIM = 64
LANES = 128
ATTN_BLOCK = 128
GQA_GROUP = 8
SWA_WINDOW = 128
DILATED_BRANCHES = ((128, 1), (512, 4), (2048, 16))
N_EXPERTS = 32
TOP_K = 4
SWIGLU_LIMIT = 7.0
SWIGLU_ALPHA = 1.702
ALIBI_MAX_BIAS = 8.0
EPS = 1e-6
MASK_DIST = 1e30
NEG_BIG = -1e30

VMEM_LIMIT = 58 * 1024 * 1024

PROJ_TN = 512
ATTN_W = 1024
PAIRS_PER_KV = GQA_GROUP // 2
EXPERT_TM = 512
GATHER_ROWS = 256
COMBINE_TM = 128


def _cparams(n_axes):
    return pltpu.CompilerParams(dimension_semantics=("arbitrary",) * n_axes,
                                vmem_limit_bytes=VMEM_LIMIT)


RESIDUE_DILATIONS = tuple(d for _, d in DILATED_BRANCHES if d > 1)


def _inproj_kernel(flag_ref, x_ref, g_ref, w_ref, gain_ref, ones_ref, o_ref, *rest, dil_tile0):
    res_refs, (h_scr, y_scr) = rest[:len(RESIDUE_DILATIONS)], rest[len(RESIDUE_DILATIONS):]
    j = pl.program_id(1)
    tm, tn = o_ref.shape

    @pl.when(j == 0)
    def _():
        x = x_ref[...]
        ms = jnp.mean(x * x, axis=-1, keepdims=True)
        h_scr[...] = (x * lax.rsqrt(ms + EPS) * g_ref[...]).astype(BF16)

    acc = jnp.dot(h_scr[...], w_ref[...], preferred_element_type=F32)

    def emit(y):
        o_ref[...] = y.astype(BF16)
        for ch in range(tn // LANES):
            y_scr[ch] = y[:, ch * LANES:(ch + 1) * LANES]

    @pl.when(flag_ref[j] == 1)
    def _():
        ss = jnp.dot((acc * acc).astype(BF16), ones_ref[...], preferred_element_type=F32)
        emit(acc * lax.rsqrt(ss * (1.0 / HEAD_DIM) + EPS) * gain_ref[...])

    @pl.when(flag_ref[j] == 0)
    def _():
        emit(acc)

    @pl.when(j >= dil_tile0)
    def _():
        for d, ref in zip(RESIDUE_DILATIONS, res_refs):
            for r in range(d):
                for ch in range(tn // LANES):
                    rows = y_scr[ch, pl.ds(r, tm // d, stride=d), :]
                    ref[r, :, ch * LANES:(ch + 1) * LANES] = rows.astype(BF16)


def _inproj(x2d, g1, w, gains, flags, tm, dil_col0):
    t, d = x2d.shape
    c = w.shape[1]
    tn = PROJ_TN
    dil_tile0 = dil_col0 // tn
    cb = c - dil_col0
    head_id = np.arange(tn) // HEAD_DIM
    ones_bd = jnp.asarray(head_id[:, None] == head_id[None, :], dtype=BF16)
    res_specs = [pl.BlockSpec((dd, tm // dd, tn), lambda i, j, f: (0, i, jnp.maximum(j - dil_tile0, 0)))
                 for dd in RESIDUE_DILATIONS]
    res_shapes = [jax.ShapeDtypeStruct((dd, t // dd, cb), BF16) for dd in RESIDUE_DILATIONS]
    grid_spec = pltpu.PrefetchScalarGridSpec(
        num_scalar_prefetch=1,
        grid=(t // tm, c // tn),
        in_specs=[
            pl.BlockSpec((tm, d), lambda i, j, f: (i, 0)),
            pl.BlockSpec((1, d), lambda i, j, f: (0, 0)),
            pl.BlockSpec((d, tn), lambda i, j, f: (0, j)),
            pl.BlockSpec((1, tn), lambda i, j, f: (0, j)),
            pl.BlockSpec((tn, tn), lambda i, j, f: (0, 0)),
        ],
        out_specs=[pl.BlockSpec((tm, tn), lambda i, j, f: (i, j))] + res_specs,
        scratch_shapes=[pltpu.VMEM((tm, d), BF16), pltpu.VMEM((tn // LANES, tm, LANES), F32)],
    )
    return pl.pallas_call(
        functools.partial(_inproj_kernel, dil_tile0=dil_tile0), grid_spec=grid_spec,
        out_shape=[jax.ShapeDtypeStruct((t, c), BF16)] + res_shapes,
        compiler_params=_cparams(2), name="inproj",
    )(flags, x2d, g1, w, gains, ones_bd)


def _attn_kernel(slope_ref, q_ref, kp_ref, kc_ref, vp_ref, vc_ref, *rest,
                 n_pairs, kv_shared, max_diff, dist_scale, with_sink):
    if with_sink:
        sink_ref, o_ref = rest
        lse_ref = None
    else:
        o_ref, lse_ref = rest
    blk = pl.program_id(2)
    cg = pl.program_id(3)

    qi = lax.broadcasted_iota(jnp.int32, (ATTN_BLOCK, 2 * ATTN_BLOCK), 0)
    kj = lax.broadcasted_iota(jnp.int32, (ATTN_BLOCK, 2 * ATTN_BLOCK), 1)
    dist = qi + ATTN_BLOCK - kj
    valid = (dist >= 0) & (dist <= max_diff) & ((blk > 0) | (kj >= ATTN_BLOCK))
    dist_m = jnp.where(valid, (dist * dist_scale).astype(F32), MASK_DIST)

    lane = lax.broadcasted_iota(jnp.int32, (ATTN_BLOCK, LANES), 1)
    low = lane < HEAD_DIM

    for p in range(n_pairs):
        cols = slice(p * LANES, (p + 1) * LANES)
        kv = p // PAIRS_PER_KV
        kcols = slice(kv * LANES, (kv + 1) * LANES) if kv_shared else cols
        q2 = q_ref[:, cols]
        kk = jnp.concatenate([kp_ref[:, kcols], kc_ref[:, kcols]], axis=0)
        vv = jnp.concatenate([vp_ref[:, kcols], vc_ref[:, kcols]], axis=0)
        outs, lses = [], []
        for hh in range(2):
            slope = slope_ref[cg * (2 * n_pairs) + 2 * p + hh]
            qm = jnp.where(low if hh == 0 else ~low, q2, jnp.zeros_like(q2))
            s = lax.dot_general(qm, kk, (((1,), (1,)), ((), ())),
                                preferred_element_type=F32)
            s = s - slope * dist_m
            m = jnp.max(s, axis=-1, keepdims=True)
            e = jnp.exp(s - m)
            l = jnp.sum(e, axis=-1, keepdims=True)
            o = jnp.dot(e.astype(BF16), vv, preferred_element_type=F32)
            outs.append(o / l)
            lses.append(jnp.broadcast_to(m + jnp.log(l), (ATTN_BLOCK, LANES)))
        o2 = jnp.where(low, outs[0], outs[1])
        lse2 = jnp.where(low, lses[0], lses[1])
        if with_sink:
            o2 = o2 * jax.nn.sigmoid(lse2 - sink_ref[:, cols])
        else:
            lse_ref[:, cols] = lse2
        o_ref[:, cols] = o2.astype(o_ref.dtype)


def _band_attention(proj, slopes, *, batch, seq, dilation, q_col0, k_col0, v_col0, n_heads,
                    kv_shared, max_diff, sinks=None):
    d = dilation
    assert proj.shape[0] == d
    lsub = seq // d
    nblk = lsub // ATTN_BLOCK
    w = ATTN_W
    n_pairs = w // LANES
    width = n_heads * HEAD_DIM
    ncg = width // w
    kw = LANES * (n_pairs // PAIRS_PER_KV) if kv_shared else w
    pv = proj

    def qmap(b, r, i, g, s):
        return (r, b * nblk + i, q_col0 // w + g)

    def kvmap(col0, prev):
        def f(b, r, i, g, s):
            blk = jnp.maximum(i - 1, 0) if prev else i
            return (r, b * nblk + blk, col0 // kw + g)
        return f

    def omap(b, r, i, g, s):
        return (r, b * nblk + i, g)

    in_specs = [
        pl.BlockSpec((None, ATTN_BLOCK, w), qmap),
        pl.BlockSpec((None, ATTN_BLOCK, kw), kvmap(k_col0, True)),
        pl.BlockSpec((None, ATTN_BLOCK, kw), kvmap(k_col0, False)),
        pl.BlockSpec((None, ATTN_BLOCK, kw), kvmap(v_col0, True)),
        pl.BlockSpec((None, ATTN_BLOCK, kw), kvmap(v_col0, False)),
    ]
    args = [pv, pv, pv, pv, pv]
    with_sink = sinks is not None
    o_shape = jax.ShapeDtypeStruct((d, batch * lsub, width), BF16)
    o_spec = pl.BlockSpec((None, ATTN_BLOCK, w), omap)
    if with_sink:
        in_specs.append(pl.BlockSpec((1, w), lambda b, r, i, g, s: (0, g)))
        args.append(sinks)
        out_shape, out_specs = o_shape, o_spec
    else:
        out_shape = (o_shape, jax.ShapeDtypeStruct((d, batch * lsub, width), F32))
        out_specs = (o_spec, pl.BlockSpec((None, ATTN_BLOCK, w), omap))
    kern = functools.partial(_attn_kernel, n_pairs=n_pairs, kv_shared=kv_shared,
                             max_diff=max_diff, dist_scale=d, with_sink=with_sink)
    grid_spec = pltpu.PrefetchScalarGridSpec(
        num_scalar_prefetch=1, grid=(batch, d, nblk, ncg),
        in_specs=in_specs, out_specs=out_specs)
    res = pl.pallas_call(kern, grid_spec=grid_spec, out_shape=out_shape,
                         compiler_params=_cparams(4), name=f"band_attn_d{d}" + ("_sink" if with_sink else ""),
                         )(slopes, *args)
    return res


def _merge_kernel(*refs, dilations):
    n = len(dilations)
    o_refs, l_refs, out, scr = refs[:n], refs[n:2 * n], refs[2 * n], refs[2 * n + 1]
    tm, w = out.shape
    nch = w // LANES
    for a, d in enumerate(dilations):
        if d == 1:
            continue
        for b, ref in enumerate((o_refs[a], l_refs[a])):
            for r in range(d):
                for ch in range(nch):
                    scr[2 * a + b, ch, pl.ds(r, tm // d, stride=d), :] = (
                        ref[r, :, ch * LANES:(ch + 1) * LANES].astype(F32))
    for ch in range(nch):
        cols = slice(ch * LANES, (ch + 1) * LANES)
        os_, ls_ = [], []
        for a, d in enumerate(dilations):
            if d == 1:
                os_.append(o_refs[a][0, :, cols].astype(F32))
                ls_.append(l_refs[a][0, :, cols])
            else:
                os_.append(scr[2 * a, ch])
                ls_.append(scr[2 * a + 1, ch])
        m = functools.reduce(jnp.maximum, ls_)
        es = [jnp.exp(l - m) for l in ls_]
        num = functools.reduce(lambda p, q: p + q, [e * o for e, o in zip(es, os_)])
        den = functools.reduce(lambda p, q: p + q, es)
        out[:, cols] = (num / den).astype(out.dtype)


def _merge(os_, ls_, dilations, tm=256):
    w = os_[0].shape[2]
    t = os_[0].shape[0] * os_[0].shape[1]
    specs = [pl.BlockSpec((d, tm // d, w), lambda i: (0, i, 0)) for d in dilations]
    return pl.pallas_call(
        functools.partial(_merge_kernel, dilations=tuple(dilations)), grid=(t // tm,),
        in_specs=specs + specs, out_specs=pl.BlockSpec((tm, w), lambda i: (i, 0)),
        out_shape=jax.ShapeDtypeStruct((t, w), BF16),
        scratch_shapes=[pltpu.VMEM((2 * len(dilations), w // LANES, tm, LANES), F32)],
        compiler_params=_cparams(1), name="branch_merge",
    )(*os_, *ls_)


def _outproj_kernel(oa_ref, ob_ref, wt_ref, wb_ref, x_ref, o_ref):
    acc = jnp.dot(oa_ref[...], wt_ref[...], preferred_element_type=F32)
    acc = acc + jnp.dot(ob_ref[...], wb_ref[...], preferred_element_type=F32)
    o_ref[...] = x_ref[...] + acc


def _outproj(o_a, o_b, w_out, x2d, tm, tn):
    t, d = x2d.shape
    ha, hb = o_a.shape[1], o_b.shape[1]
    assert ha == hb
    return pl.pallas_call(
        _outproj_kernel, grid=(t // tm, d // tn),
        in_specs=[
            pl.BlockSpec((tm, ha), lambda i, j: (i, 0)),
            pl.BlockSpec((tm, hb), lambda i, j: (i, 0)),
            pl.BlockSpec((ha, tn), lambda i, j: (0, j)),
            pl.BlockSpec((hb, tn), lambda i, j: (1, j)),
            pl.BlockSpec((tm, tn), lambda i, j: (i, j)),
        ],
        out_specs=pl.BlockSpec((tm, tn), lambda i, j: (i, j)),
        out_shape=jax.ShapeDtypeStruct((t, d), F32), compiler_params=_cparams(2), name="outproj",
    )(o_a, o_b, w_out, w_out, x2d)


def _router_kernel(x_ref, g_ref, whi_ref, wlo_ref, b_ref, tri_ref,
                   h_ref, idx_ref, gate_ref, rank_ref, cnt_ref, carry, h_scr):
    i = pl.program_id(0)

    @pl.when(i == 0)
    def _():
        carry[...] = jnp.zeros_like(carry)

    x = x_ref[...]
    ms = jnp.mean(x * x, axis=-1, keepdims=True)
    h = x * lax.rsqrt(ms + EPS) * g_ref[...]
    h_hi = h.astype(BF16)
    h_lo = (h - h_hi.astype(F32)).astype(BF16)
    u = pltpu.bitcast(h, jnp.uint32)
    r = (u + jnp.uint32(0x7FFF) + ((u >> 16) & jnp.uint32(1))) >> 16
    tok, words, _ = h_ref.shape
    for s in range(words):
        lo = r[:, 2 * s * LANES:(2 * s + 1) * LANES]
        hi = r[:, (2 * s + 1) * LANES:(2 * s + 2) * LANES]
        h_scr[pl.ds(s, tok, stride=words), :] = lo | (hi << 16)
    h_ref[...] = h_scr[...].reshape(tok, words, LANES)
    logits = (jnp.dot(h_hi, whi_ref[...], preferred_element_type=F32)
              + jnp.dot(h_lo, whi_ref[...], preferred_element_type=F32)
              + jnp.dot(h_hi, wlo_ref[...], preferred_element_type=F32)) + b_ref[...]

    tm = x.shape[0]
    lane = lax.broadcasted_iota(jnp.int32, (tm, LANES), 1).astype(F32)
    work = logits
    multihot = jnp.zeros((tm, LANES), F32)
    vals, idxs = [], []
    for _ in range(TOP_K):
        m = jnp.max(work, axis=-1, keepdims=True)
        ik = jnp.min(jnp.where(work == m, lane, float(LANES)), axis=-1, keepdims=True)
        sel = lane == ik
        work = jnp.where(sel, -jnp.inf, work)
        multihot = jnp.where(sel, 1.0, multihot)
        vals.append(m)
        idxs.append(ik)
    es = [jnp.exp(v - vals[0]) for v in vals]
    denom = es[0] + es[1] + es[2] + es[3]

    cum = jnp.dot(tri_ref[...], multihot.astype(BF16), preferred_element_type=F32) + carry[0:1, :]
    idx_t = jnp.zeros((tm, LANES), F32)
    gate_t = jnp.zeros((tm, LANES), F32)
    rank_t = jnp.zeros((tm, LANES), F32)
    for k in range(TOP_K):
        rk = jnp.sum(jnp.where(lane == idxs[k], cum, 0.0), axis=-1, keepdims=True)
        here = lane == float(k)
        idx_t = jnp.where(here, idxs[k], idx_t)
        gate_t = jnp.where(here, es[k] / denom, gate_t)
        rank_t = jnp.where(here, rk, rank_t)
    idx_ref[...] = idx_t.astype(jnp.int32)
    gate_ref[...] = gate_t
    rank_ref[...] = rank_t.astype(jnp.int32)
    new_carry = carry[0:1, :] + jnp.sum(multihot, axis=0, keepdims=True)
    carry[...] = jnp.broadcast_to(new_carry, carry.shape)
    cnt_ref[...] = jnp.broadcast_to(new_carry, cnt_ref.shape)


def _router(x2d, g2, w_router, b_router, tm=256):
    t, d = x2d.shape
    ne = w_router.shape[1]
    w_pad = jnp.zeros((d, LANES), F32).at[:, :ne].set(w_router)
    w_hi = w_pad.astype(BF16)
    w_lo = (w_pad - w_hi.astype(F32)).astype(BF16)
    b_pad = jnp.full((1, LANES), NEG_BIG, F32).at[0, :ne].set(b_router)
    tri = jnp.asarray(np.tril(np.ones((tm, tm), np.float32), -1), dtype=BF16)
    tile = lambda dt: jax.ShapeDtypeStruct((t, LANES), dt)
    row = pl.BlockSpec((tm, LANES), lambda i: (i, 0))
    const = lambda shape: pl.BlockSpec(shape, lambda i: (0, 0))
    return pl.pallas_call(
        _router_kernel, grid=(t // tm,),
        in_specs=[pl.BlockSpec((tm, d), lambda i: (i, 0)), const((1, d)), const((d, LANES)),
                  const((d, LANES)), const((1, LANES)), const((tm, tm))],
        out_specs=(pl.BlockSpec((tm, d // (2 * LANES), LANES), lambda i: (i, 0, 0)), row, row, row,
                   const((8, LANES))),
        out_shape=(jax.ShapeDtypeStruct((t, d // (2 * LANES), LANES), jnp.uint32),
                   tile(jnp.int32), tile(F32), tile(jnp.int32),
                   jax.ShapeDtypeStruct((8, LANES), F32)),
        scratch_shapes=[pltpu.VMEM((8, LANES), F32), pltpu.VMEM((tm * (d // (2 * LANES)), LANES), jnp.uint32)],
        compiler_params=_cparams(1), name="router",
    )(x2d, g2, w_hi, w_lo, b_pad, tri)


INDEX_SLOTS = 3
ROW_SLOTS = 2


def _row_prefetch_ring(idx_hbm, idx_smem, isem, start_rows, wait_rows):
    i = pl.program_id(0)
    n = pl.num_programs(0)

    def idx_copy(step):
        slot = lax.rem(step, INDEX_SLOTS)
        return pltpu.make_async_copy(idx_hbm.at[step], idx_smem.at[slot], isem.at[slot])

    @pl.when(i == 0)
    def _():
        idx_copy(0).start()
        idx_copy(0).wait()
        start_rows(0, 0, 0)

        @pl.when(n > 1)
        def _():
            idx_copy(1).start()

    @pl.when(i + 1 < n)
    def _():
        idx_copy(i + 1).wait()
        start_rows(i + 1, lax.rem(i + 1, INDEX_SLOTS), lax.rem(i + 1, ROW_SLOTS))

    @pl.when(i + 2 < n)
    def _():
        idx_copy(i + 2).start()

    row_slot = lax.rem(i, ROW_SLOTS)
    wait_rows(row_slot)
    return row_slot


def _gather_kernel(tok_hbm, h_hbm, xs_ref, tok_smem, buf, isem, rsem):
    g = GATHER_ROWS
    words = h_hbm.shape[1]

    def start_rows(step, idx_slot, row_slot):
        def body(r2, c):
            for prio in range(2):
                r = 2 * r2 + prio
                t = tok_smem[idx_slot, r]
                dst = buf.at[row_slot, pl.ds(pl.multiple_of(r * words, words), words)]
                pltpu.make_async_copy(h_hbm.at[t], dst, rsem.at[row_slot]).start(priority=prio)
            return c
        lax.fori_loop(0, g // 2, body, 0)

    def wait_rows(row_slot):
        pltpu.make_async_copy(buf.at[row_slot], buf.at[row_slot], rsem.at[row_slot]).wait()

    row_slot = _row_prefetch_ring(tok_hbm, tok_smem, isem, start_rows, wait_rows)

    for s in range(words):
        x = buf[row_slot, pl.ds(s, g, stride=words), :]
        lo = pltpu.bitcast(x << 16, F32)
        hi = pltpu.bitcast(x & jnp.uint32(0xFFFF0000), F32)
        xs_ref[:, 2 * s * LANES:(2 * s + 1) * LANES] = lo.astype(BF16)
        xs_ref[:, (2 * s + 1) * LANES:(2 * s + 2) * LANES] = hi.astype(BF16)


def _gather_rows(h_packed, row_tok):
    t, words, _ = h_packed.shape
    r_max = row_tok.shape[0]
    g = GATHER_ROWS
    n_steps = r_max // g
    tok2d = row_tok.reshape(n_steps, g)
    return pl.pallas_call(
        _gather_kernel, grid=(n_steps,),
        in_specs=[pl.BlockSpec(memory_space=pl.ANY), pl.BlockSpec(memory_space=pl.ANY)],
        out_specs=pl.BlockSpec((g, 2 * words * LANES), lambda i: (i, 0)),
        scratch_shapes=[pltpu.SMEM((INDEX_SLOTS, g), jnp.int32),
                        pltpu.VMEM((ROW_SLOTS, g * words, LANES), jnp.uint32),
                        pltpu.SemaphoreType.DMA((INDEX_SLOTS,)), pltpu.SemaphoreType.DMA((ROW_SLOTS,))],
        out_shape=jax.ShapeDtypeStruct((r_max, 2 * words * LANES), BF16),
        compiler_params=_cparams(1), name="row_gather",
    )(tok2d, h_packed)


PREP_ROWS = 1024


def _tile_clamp(i, na):
    return jnp.minimum(i, na[0] - 1)


def _weights_changed(te_ref, na_ref, i):
    cur = _tile_clamp(i, na_ref)
    prev = jnp.maximum(cur - 1, 0)
    return (i < na_ref[0]) & ((i == 0) | (te_ref[cur] != te_ref[prev]))


def _gateup_kernel(te_ref, na_ref, xs_ref, w_ref, b_ref, pe_ref, h_ref, w_s):
    i = pl.program_id(1)
    d, tc = w_s.shape

    @pl.when(_weights_changed(te_ref, na_ref, i))
    def _():
        for r0 in range(0, d, PREP_ROWS):
            rows = slice(r0, min(r0 + PREP_ROWS, d))
            w_s[rows, :] = w_ref[rows, :].astype(BF16)

    @pl.when(i < na_ref[0])
    def _():
        gu = jnp.dot(xs_ref[...], w_s[...], preferred_element_type=F32) + b_ref[...]
        acts = []
        for c in range(tc // LANES):
            g = gu[:, c * LANES:(c + 1) * LANES]
            u = pltpu.roll(g, LANES - 1, axis=1)
            gate = jnp.minimum(g, SWIGLU_LIMIT)
            up = jnp.clip(u, -SWIGLU_LIMIT, SWIGLU_LIMIT)
            acts.append(((up + 1.0) * gate * jax.nn.sigmoid(SWIGLU_ALPHA * gate)).astype(BF16))
        act = jnp.concatenate(acts, axis=1)
        h_ref[...] = jnp.dot(act, pe_ref[...], preferred_element_type=F32).astype(h_ref.dtype)

    @pl.when(i >= na_ref[0])
    def _():
        h_ref[...] = jnp.zeros_like(h_ref)


def _down_kernel(te_ref, na_ref, h_ref, w_ref, bd_ref, rw_ref, o_ref, wd_s, o_scr):
    i = pl.program_id(1)

    @pl.when(_weights_changed(te_ref, na_ref, i))
    def _():
        ff = wd_s.shape[0]
        for r0 in range(0, ff, PREP_ROWS):
            rows = slice(r0, min(r0 + PREP_ROWS, ff))
            wd_s[rows, :] = w_ref[rows, :].astype(BF16)

    @pl.when(i < na_ref[0])
    def _():
        out = (jnp.dot(h_ref[...], wd_s[...], preferred_element_type=F32) + bd_ref[...]) * rw_ref[...]
        tm, chunks, _ = o_ref.shape
        for c in range(chunks):
            o_scr[pl.ds(c, tm, stride=chunks), :] = out[:, c * LANES:(c + 1) * LANES]
        o_ref[...] = o_scr[...].reshape(tm, chunks, LANES)

    @pl.when(i >= na_ref[0])
    def _():
        o_ref[...] = jnp.zeros_like(o_ref)


def _experts(xs, row_w, tile_e, n_active, w_gate_up, b_gate_up, w_down, bd, tn_ff, tn_d):
    r_max, d = xs.shape
    ff = w_down.shape[1]
    tm = EXPERT_TM
    n_tiles = r_max // tm
    tc = 2 * tn_ff
    p_even = jnp.asarray(np.arange(tc)[:, None] == 2 * np.arange(tn_ff)[None, :], dtype=BF16)
    gu_spec = pltpu.PrefetchScalarGridSpec(
        num_scalar_prefetch=2, grid=(ff // tn_ff, n_tiles),
        in_specs=[
            pl.BlockSpec((tm, d), lambda j, i, te, na: (_tile_clamp(i, na), 0)),
            pl.BlockSpec((None, d, tc), lambda j, i, te, na: (te[_tile_clamp(i, na)], 0, j)),
            pl.BlockSpec((None, 1, tc), lambda j, i, te, na: (te[_tile_clamp(i, na)], 0, j)),
            pl.BlockSpec((tc, tn_ff), lambda j, i, te, na: (0, 0)),
        ],
        out_specs=pl.BlockSpec((tm, tn_ff), lambda j, i, te, na: (i, j)),
        scratch_shapes=[pltpu.VMEM((d, tc), BF16)])
    hidden = pl.pallas_call(
        _gateup_kernel, grid_spec=gu_spec, out_shape=jax.ShapeDtypeStruct((r_max, ff), BF16),
        compiler_params=_cparams(2), name="expert_gate_up",
    )(tile_e, n_active, xs, w_gate_up, b_gate_up, p_even)
    dn_spec = pltpu.PrefetchScalarGridSpec(
        num_scalar_prefetch=2, grid=(d // tn_d, n_tiles),
        in_specs=[
            pl.BlockSpec((tm, ff), lambda j, i, te, na: (_tile_clamp(i, na), 0)),
            pl.BlockSpec((None, ff, tn_d), lambda j, i, te, na: (te[_tile_clamp(i, na)], 0, j)),
            pl.BlockSpec((None, 1, tn_d), lambda j, i, te, na: (te[_tile_clamp(i, na)], 0, j)),
            pl.BlockSpec((tm, 1), lambda j, i, te, na: (_tile_clamp(i, na), 0)),
        ],
        out_specs=pl.BlockSpec((tm, tn_d // LANES, LANES), lambda j, i, te, na: (i, j, 0)),
        scratch_shapes=[pltpu.VMEM((ff, tn_d), BF16), pltpu.VMEM((tm * (tn_d // LANES), LANES), F32)])
    return pl.pallas_call(
        _down_kernel, grid_spec=dn_spec, out_shape=jax.ShapeDtypeStruct((r_max, d // LANES, LANES), F32),
        compiler_params=_cparams(2), name="expert_down",
    )(tile_e, n_active, hidden, w_down, bd, row_w)


def _combine_kernel(dest_hbm, rows_hbm, x_ref, o_ref, dest_smem, buf, sum_scr, isem, rsem):
    tm = COMBINE_TM
    chunks = rows_hbm.shape[1]
    seg = tm * chunks

    def start_rows(step, idx_slot, row_slot):
        def body(r, c):
            for k in range(TOP_K):
                src = dest_smem[idx_slot, r * TOP_K + k]
                off = pl.multiple_of((k * tm + r) * chunks, chunks)
                pltpu.make_async_copy(rows_hbm.at[src], buf.at[row_slot, pl.ds(off, chunks)],
                                      rsem.at[row_slot]).start()
            return c
        lax.fori_loop(0, tm, body, 0)

    def wait_rows(row_slot):
        pltpu.make_async_copy(buf.at[row_slot], buf.at[row_slot], rsem.at[row_slot]).wait()

    row_slot = _row_prefetch_ring(dest_hbm, dest_smem, isem, start_rows, wait_rows)

    acc = buf[row_slot, pl.ds(0, seg), :]
    for k in range(1, TOP_K):
        acc = acc + buf[row_slot, pl.ds(k * seg, seg), :]
    sum_scr[...] = acc
    for c in range(chunks):
        cols = slice(c * LANES, (c + 1) * LANES)
        o_ref[:, cols] = x_ref[:, cols] + sum_scr[pl.ds(c, tm, stride=chunks), :]


def _combine(rows, dest, x2d):
    t, d = x2d.shape
    chunks = rows.shape[1]
    tm = COMBINE_TM
    n_steps = t // tm
    dest2d = dest.reshape(n_steps, tm * TOP_K)
    blk = pl.BlockSpec((tm, d), lambda i: (i, 0))
    return pl.pallas_call(
        _combine_kernel, grid=(n_steps,),
        in_specs=[pl.BlockSpec(memory_space=pl.ANY), pl.BlockSpec(memory_space=pl.ANY), blk],
        out_specs=blk,
        out_shape=jax.ShapeDtypeStruct((t, d), F32),
        scratch_shapes=[pltpu.SMEM((INDEX_SLOTS, tm * TOP_K), jnp.int32),
                        pltpu.VMEM((ROW_SLOTS, TOP_K * tm * chunks, LANES), F32),
                        pltpu.VMEM((tm * chunks, LANES), F32),
                        pltpu.SemaphoreType.DMA((INDEX_SLOTS,)), pltpu.SemaphoreType.DMA((ROW_SLOTS,))],
        compiler_params=_cparams(1), name="combine",
    )(dest2d, rows, x2d)


def _moe(x_mid, g2, w_router, b_router, w_gate_up, b_gate_up, w_down, b_down, tn_ff, tn_d):
    t, d = x_mid.shape
    ne = w_router.shape[1]
    h2, idx_t, gate_t, rank_t, cnt = _router(x_mid, g2, w_router, b_router)

    tm = EXPERT_TM
    r_max = t * TOP_K + ne * tm
    n_tiles = r_max // tm
    counts = cnt[0, :ne].astype(jnp.int32)
    padded = (counts + tm - 1) // tm * tm
    pend = jnp.cumsum(padded)
    pstart = pend - padded
    idx = idx_t[:, :TOP_K]
    dest = (pstart[idx] + rank_t[:, :TOP_K]).astype(jnp.int32)
    n_rows_used = pend[-1:].astype(jnp.int32)
    n_active = n_rows_used // tm
    tok = jnp.repeat(jnp.arange(t, dtype=jnp.int32), TOP_K)
    gate_bits = lax.bitcast_convert_type(gate_t[:, :TOP_K].reshape(-1), jnp.int32)
    table = jnp.zeros((r_max, 2), jnp.int32).at[dest.reshape(-1)].set(
        jnp.stack([tok, gate_bits], axis=1), unique_indices=True)
    row_tok = table[:, 0]
    row_w = lax.bitcast_convert_type(table[:, 1], F32)[:, None]
    tile_start = jnp.arange(n_tiles, dtype=jnp.int32) * tm
    tile_e = jnp.minimum(jnp.sum((pend[None, :] <= tile_start[:, None]).astype(jnp.int32), axis=1), ne - 1)

    xs = _gather_rows(h2, row_tok)

    rows = _experts(xs, row_w, tile_e, n_active, w_gate_up, b_gate_up[:, None, :], w_down, b_down[:, None, :],
                    tn_ff, tn_d)
    return _combine(rows, dest, x_mid)


def _alibi_slopes(n):
    return (2.0 ** (-ALIBI_MAX_BIAS * (np.arange(n, dtype=np.float32) + 1.0) / n)).astype(np.float32)


def _mixer(x2d, batch, seq, norm1_g, w_in, q_norm_swa, k_norm_swa, q_norm_dil, k_norm_dil, sinks, w_out,
           n_swa, n_kv, n_dil, tm_proj, tn_out):
    t, d = x2d.shape
    swa_q, swa_kv, dil_w = n_swa * HEAD_DIM, n_kv * HEAD_DIM, n_dil * HEAD_DIM
    assert n_swa // n_kv == GQA_GROUP and seq % (16 * ATTN_BLOCK) == 0
    s1, s2, s3 = swa_q, swa_q + swa_kv, swa_q + 2 * swa_kv

    def dup(wc):
        wc = wc.reshape(d, n_kv, HEAD_DIM)
        return jnp.concatenate([wc, wc], axis=-1).reshape(d, 2 * swa_kv)

    w = jnp.concatenate([w_in[:, :s1], dup(w_in[:, s1:s2]), dup(w_in[:, s2:s3]), w_in[:, s3:]],
                        axis=1).astype(BF16)
    scale = HEAD_DIM ** -0.5
    ones = lambda n: jnp.ones((n,), F32)
    gains = jnp.concatenate([
        jnp.tile(q_norm_swa * scale, n_swa), jnp.tile(k_norm_swa, 2 * n_kv), ones(2 * swa_kv),
        jnp.tile(q_norm_dil * scale, n_dil), jnp.tile(k_norm_dil, n_dil), ones(dil_w)])[None, :]
    c = w.shape[1]
    bounds = np.cumsum([0, swa_q, 2 * swa_kv, 2 * swa_kv, dil_w, dil_w, dil_w])
    assert all(b % PROJ_TN == 0 for b in bounds)
    seg_norm = [1, 1, 0, 1, 1, 0]
    flags = np.zeros((c // PROJ_TN,), np.int32)
    for sidx in range(6):
        flags[bounds[sidx] // PROJ_TN:bounds[sidx + 1] // PROJ_TN] = seg_norm[sidx]
    qa0, ka0, va0, qb0, kb0, vb0 = (int(b) for b in bounds[:6])
    proj, *residue_major = _inproj(x2d, norm1_g[None, :], w, gains, jnp.asarray(flags), tm_proj, dil_col0=qb0)
    by_dilation = dict(zip(RESIDUE_DILATIONS, residue_major))

    sink_row = jnp.repeat(sinks.astype(F32), HEAD_DIM)[None, :]
    o_a = _band_attention(proj[None], jnp.asarray(_alibi_slopes(n_swa)), batch=batch, seq=seq, dilation=1,
                          q_col0=qa0, k_col0=ka0, v_col0=va0, n_heads=n_swa, kv_shared=True,
                          max_diff=SWA_WINDOW - 1, sinks=sink_row)[0]
    slopes_dil = jnp.asarray(_alibi_slopes(n_dil))
    outs, lses = [], []
    for window, dil in DILATED_BRANCHES:
        src, col0 = (proj[None], qb0) if dil == 1 else (by_dilation[dil], 0)
        o_i, lse_i = _band_attention(src, slopes_dil, batch=batch, seq=seq, dilation=dil,
                                     q_col0=col0, k_col0=col0 + kb0 - qb0, v_col0=col0 + vb0 - qb0,
                                     n_heads=n_dil, kv_shared=False, max_diff=window // dil)
        outs.append(o_i)
        lses.append(lse_i)
    o_b = _merge(outs, lses, [dil for _, dil in DILATED_BRANCHES])
    return _outproj(o_a, o_b, w_out.astype(BF16), x2d, tm_proj, tn_out)


def kernel(x, norm1_g, w_in, q_norm_swa, k_norm_swa, q_norm_dil, k_norm_dil, sinks, w_out, norm2_g,
           w_router, b_router, w_gate_up, b_gate_up, w_down, b_down):
    b, s, d = x.shape
    depth = norm1_g.shape[0]
    n_heads = d // HEAD_DIM
    n_swa = n_heads // 2
    n_kv = n_swa // GQA_GROUP
    n_dil = n_heads - n_swa
    x2d = x.reshape(b * s, d)
    for l in range(depth):
        x_mid = _mixer(x2d, b, s, norm1_g[l], w_in[l], q_norm_swa[l], k_norm_swa[l], q_norm_dil[l],
                       k_norm_dil[l], sinks[l], w_out[l], n_swa, n_kv, n_dil, tm_proj=512, tn_out=512)
        x2d = _moe(x_mid, norm2_g[l][None, :], w_router[l], b_router[l], w_gate_up[l], b_gate_up[l],
                   w_down[l], b_down[l], tn_ff=512, tn_d=1024)
    return x2d.reshape(b, s, d)
```

```python
import functools

import jax
import jax.numpy as jnp
import numpy as np
from jax import lax
from jax.experimental import pallas as pl
from jax.experimental.pallas import tpu as pltpu

F32 = jnp.float32
BF16 = jnp.bfloat16

HEAD_DIM = 64
LANES = 128
ATTN_BLOCK = 128
GQA_GROUP = 8
SWA_WINDOW = 128
DILATED_BRANCHES = ((128, 1), (512, 4), (2048, 16))
N_EXPERTS = 32
TOP_K = 4
SWIGLU_LIMIT = 7.0
SWIGLU_ALPHA = 1.702
ALIBI_MAX_BIAS = 8.0
EPS = 1e-6
MASK_DIST = 1e30
NEG_BIG = -1e30

VMEM_LIMIT = 58 * 1024 * 1024

PROJ_TN = 512
ATTN_W_CHOICES = (2048, 1024, 512)
PAIRS_PER_KV = GQA_GROUP // 2
EXPERT_TM = 512
GATHER_ROWS = 256
COMBINE_TM = 128


def _cparams(n_axes):
    return pltpu.CompilerParams(dimension_semantics=("arbitrary",) * n_axes,
                                vmem_limit_bytes=VMEM_LIMIT)


RESIDUE_DILATIONS = tuple(d for _, d in DILATED_BRANCHES if d > 1)


def _inproj_kernel(flag_ref, x_ref, g_ref, w_ref, gain_ref, ones_ref, o_ref, *rest, dil_tile0):
    res_refs, (h_scr, y_scr) = rest[:len(RESIDUE_DILATIONS)], rest[len(RESIDUE_DILATIONS):]
    j = pl.program_id(1)
    tm, tn = o_ref.shape

    @pl.when(j == 0)
    def _():
        x = x_ref[...]
        ms = jnp.mean(x * x, axis=-1, keepdims=True)
        h_scr[...] = (x * lax.rsqrt(ms + EPS) * g_ref[...]).astype(BF16)

    acc = jnp.dot(h_scr[...], w_ref[...], preferred_element_type=F32)

    def emit(y):
        o_ref[...] = y.astype(BF16)
        for ch in range(tn // LANES):
            y_scr[ch] = y[:, ch * LANES:(ch + 1) * LANES]

    @pl.when(flag_ref[j] == 1)
    def _():
        ss = jnp.dot((acc * acc).astype(BF16), ones_ref[...], preferred_element_type=F32)
        emit(acc * lax.rsqrt(ss * (1.0 / HEAD_DIM) + EPS) * gain_ref[...])

    @pl.when(flag_ref[j] == 0)
    def _():
        emit(acc)

    @pl.when(j >= dil_tile0)
    def _():
        for d, ref in zip(RESIDUE_DILATIONS, res_refs):
            for r in range(d):
                for ch in range(tn // LANES):
                    rows = y_scr[ch, pl.ds(r, tm // d, stride=d), :]
                    ref[r, :, ch * LANES:(ch + 1) * LANES] = rows.astype(BF16)


def _inproj(x2d, g1, w, gains, flags, tm, dil_col0):
    t, d = x2d.shape
    c = w.shape[1]
    tn = PROJ_TN
    dil_tile0 = dil_col0 // tn
    cb = c - dil_col0
    head_id = np.arange(tn) // HEAD_DIM
    ones_bd = jnp.asarray(head_id[:, None] == head_id[None, :], dtype=BF16)
    res_specs = [pl.BlockSpec((dd, tm // dd, tn), lambda i, j, f: (0, i, jnp.maximum(j - dil_tile0, 0)))
                 for dd in RESIDUE_DILATIONS]
    res_shapes = [jax.ShapeDtypeStruct((dd, t // dd, cb), BF16) for dd in RESIDUE_DILATIONS]
    grid_spec = pltpu.PrefetchScalarGridSpec(
        num_scalar_prefetch=1,
        grid=(t // tm, c // tn),
        in_specs=[
            pl.BlockSpec((tm, d), lambda i, j, f: (i, 0)),
            pl.BlockSpec((1, d), lambda i, j, f: (0, 0)),
            pl.BlockSpec((d, tn), lambda i, j, f: (0, j)),
            pl.BlockSpec((1, tn), lambda i, j, f: (0, j)),
            pl.BlockSpec((tn, tn), lambda i, j, f: (0, 0)),
        ],
        out_specs=[pl.BlockSpec((tm, tn), lambda i, j, f: (i, j))] + res_specs,
        scratch_shapes=[pltpu.VMEM((tm, d), BF16), pltpu.VMEM((tn // LANES, tm, LANES), F32)],
    )
    return pl.pallas_call(
        functools.partial(_inproj_kernel, dil_tile0=dil_tile0), grid_spec=grid_spec,
        out_shape=[jax.ShapeDtypeStruct((t, c), BF16)] + res_shapes,
        compiler_params=_cparams(2), name="inproj",
    )(flags, x2d, g1, w, gains, ones_bd)


def _attn_kernel(slope_ref, q_ref, kp_ref, kc_ref, vp_ref, vc_ref, *rest,
                 n_pairs, kv_shared, max_diff, dist_scale, with_sink):
    if with_sink:
        sink_ref, o_ref = rest
        lse_ref = None
    else:
        o_ref, lse_ref = rest
    blk = pl.program_id(2)
    cg = pl.program_id(3)

    qi = lax.broadcasted_iota(jnp.int32, (ATTN_BLOCK, 2 * ATTN_BLOCK), 0)
    kj = lax.broadcasted_iota(jnp.int32, (ATTN_BLOCK, 2 * ATTN_BLOCK), 1)
    dist = qi + ATTN_BLOCK - kj
    valid = (dist >= 0) & (dist <= max_diff) & ((blk > 0) | (kj >= ATTN_BLOCK))
    dist_m = jnp.where(valid, (dist * dist_scale).astype(F32), MASK_DIST)

    lane = lax.broadcasted_iota(jnp.int32, (ATTN_BLOCK, LANES), 1)
    low = lane < HEAD_DIM

    for p in range(n_pairs):
        cols = slice(p * LANES, (p + 1) * LANES)
        kv = p // PAIRS_PER_KV
        kcols = slice(kv * LANES, (kv + 1) * LANES) if kv_shared else cols
        q2 = q_ref[:, cols]
        kk = jnp.concatenate([kp_ref[:, kcols], kc_ref[:, kcols]], axis=0)
        vv = jnp.concatenate([vp_ref[:, kcols], vc_ref[:, kcols]], axis=0)
        outs, lses = [], []
        for hh in range(2):
            slope = slope_ref[cg * (2 * n_pairs) + 2 * p + hh]
            qm = jnp.where(low if hh == 0 else ~low, q2, jnp.zeros_like(q2))
            s = lax.dot_general(qm, kk, (((1,), (1,)), ((), ())),
                                preferred_element_type=F32)
            s = s - slope * dist_m
            m = jnp.max(s, axis=-1, keepdims=True)
            e = jnp.exp(s - m)
            l = jnp.sum(e, axis=-1, keepdims=True)
            o = jnp.dot(e.astype(BF16), vv, preferred_element_type=F32)
            outs.append(o / l)
            lses.append(jnp.broadcast_to(m + jnp.log(l), (ATTN_BLOCK, LANES)))
        o2 = jnp.where(low, outs[0], outs[1])
        lse2 = jnp.where(low, lses[0], lses[1])
        if with_sink:
            o2 = o2 * jax.nn.sigmoid(lse2 - sink_ref[:, cols])
        else:
            lse_ref[:, cols] = lse2
        o_ref[:, cols] = o2.astype(o_ref.dtype)


def _band_attention(proj, slopes, *, batch, seq, dilation, q_col0, k_col0, v_col0, n_heads,
                    kv_shared, max_diff, sinks=None):
    d = dilation
    assert proj.shape[0] == d
    lsub = seq // d
    nblk = lsub // ATTN_BLOCK
    width = n_heads * HEAD_DIM

    def kv_width(w):
        return LANES * (w // LANES // PAIRS_PER_KV) if kv_shared else w

    w = next(c for c in ATTN_W_CHOICES
             if width % c == 0 and q_col0 % c == 0 and k_col0 % kv_width(c) == 0 and v_col0 % kv_width(c) == 0)
    n_pairs = w // LANES
    ncg = width // w
    kw = kv_width(w)
    pv = proj

    def qmap(b, r, i, g, s):
        return (r, b * nblk + i, q_col0 // w + g)

    def kvmap(col0, prev):
        def f(b, r, i, g, s):
            blk = jnp.maximum(i - 1, 0) if prev else i
            return (r, b * nblk + blk, col0 // kw + g)
        return f

    def omap(b, r, i, g, s):
        return (r, b * nblk + i, g)

    in_specs = [
        pl.BlockSpec((None, ATTN_BLOCK, w), qmap),
        pl.BlockSpec((None, ATTN_BLOCK, kw), kvmap(k_col0, True)),
        pl.BlockSpec((None, ATTN_BLOCK, kw), kvmap(k_col0, False)),
        pl.BlockSpec((None, ATTN_BLOCK, kw), kvmap(v_col0, True)),
        pl.BlockSpec((None, ATTN_BLOCK, kw), kvmap(v_col0, False)),
    ]
    args = [pv, pv, pv, pv, pv]
    with_sink = sinks is not None
    o_shape = jax.ShapeDtypeStruct((d, batch * lsub, width), BF16)
    o_spec = pl.BlockSpec((None, ATTN_BLOCK, w), omap)
    if with_sink:
        in_specs.append(pl.BlockSpec((1, w), lambda b, r, i, g, s: (0, g)))
        args.append(sinks)
        out_shape, out_specs = o_shape, o_spec
    else:
        out_shape = (o_shape, jax.ShapeDtypeStruct((d, batch * lsub, width), F32))
        out_specs = (o_spec, pl.BlockSpec((None, ATTN_BLOCK, w), omap))
    kern = functools.partial(_attn_kernel, n_pairs=n_pairs, kv_shared=kv_shared,
                             max_diff=max_diff, dist_scale=d, with_sink=with_sink)
    grid_spec = pltpu.PrefetchScalarGridSpec(
        num_scalar_prefetch=1, grid=(batch, d, nblk, ncg),
        in_specs=in_specs, out_specs=out_specs)
    res = pl.pallas_call(kern, grid_spec=grid_spec, out_shape=out_shape,
                         compiler_params=_cparams(4), name=f"band_attn_d{d}" + ("_sink" if with_sink else ""),
                         )(slopes, *args)
    return res


def _merge_kernel(*refs, dilations):
    n = len(dilations)
    o_refs, l_refs, out, scr = refs[:n], refs[n:2 * n], refs[2 * n], refs[2 * n + 1]
    tm, w = out.shape
    nch = w // LANES
    for a, d in enumerate(dilations):
        if d == 1:
            continue
        for b, ref in enumerate((o_refs[a], l_refs[a])):
            for r in range(d):
                for ch in range(nch):
                    scr[2 * a + b, ch, pl.ds(r, tm // d, stride=d), :] = (
                        ref[r, :, ch * LANES:(ch + 1) * LANES].astype(F32))
    for ch in range(nch):
        cols = slice(ch * LANES, (ch + 1) * LANES)
        os_, ls_ = [], []
        for a, d in enumerate(dilations):
            if d == 1:
                os_.append(o_refs[a][0, :, cols].astype(F32))
                ls_.append(l_refs[a][0, :, cols])
            else:
                os_.append(scr[2 * a, ch])
                ls_.append(scr[2 * a + 1, ch])
        m = functools.reduce(jnp.maximum, ls_)
        es = [jnp.exp(l - m) for l in ls_]
        num = functools.reduce(lambda p, q: p + q, [e * o for e, o in zip(es, os_)])
        den = functools.reduce(lambda p, q: p + q, es)
        out[:, cols] = (num / den).astype(out.dtype)


def _merge(os_, ls_, dilations, tm=256):
    w = os_[0].shape[2]
    t = os_[0].shape[0] * os_[0].shape[1]
    specs = [pl.BlockSpec((d, tm // d, w), lambda i: (0, i, 0)) for d in dilations]
    return pl.pallas_call(
        functools.partial(_merge_kernel, dilations=tuple(dilations)), grid=(t // tm,),
        in_specs=specs + specs, out_specs=pl.BlockSpec((tm, w), lambda i: (i, 0)),
        out_shape=jax.ShapeDtypeStruct((t, w), BF16),
        scratch_shapes=[pltpu.VMEM((2 * len(dilations), w // LANES, tm, LANES), F32)],
        compiler_params=_cparams(1), name="branch_merge",
    )(*os_, *ls_)


def _outproj_kernel(oa_ref, ob_ref, wt_ref, wb_ref, x_ref, o_ref):
    acc = jnp.dot(oa_ref[...], wt_ref[...], preferred_element_type=F32)
    acc = acc + jnp.dot(ob_ref[...], wb_ref[...], preferred_element_type=F32)
    o_ref[...] = x_ref[...] + acc


def _outproj(o_a, o_b, w_out, x2d, tm, tn):
    t, d = x2d.shape
    ha, hb = o_a.shape[1], o_b.shape[1]
    assert ha == hb
    return pl.pallas_call(
        _outproj_kernel, grid=(t // tm, d // tn),
        in_specs=[
            pl.BlockSpec((tm, ha), lambda i, j: (i, 0)),
            pl.BlockSpec((tm, hb), lambda i, j: (i, 0)),
            pl.BlockSpec((ha, tn), lambda i, j: (0, j)),
            pl.BlockSpec((hb, tn), lambda i, j: (1, j)),
            pl.BlockSpec((tm, tn), lambda i, j: (i, j)),
        ],
        out_specs=pl.BlockSpec((tm, tn), lambda i, j: (i, j)),
        out_shape=jax.ShapeDtypeStruct((t, d), F32), compiler_params=_cparams(2), name="outproj",
    )(o_a, o_b, w_out, w_out, x2d)


def _router_kernel(x_ref, g_ref, whi_ref, wlo_ref, b_ref, tri_ref,
                   h_ref, idx_ref, gate_ref, rank_ref, cnt_ref, carry, h_scr):
    i = pl.program_id(0)

    @pl.when(i == 0)
    def _():
        carry[...] = jnp.zeros_like(carry)

    x = x_ref[...]
    ms = jnp.mean(x * x, axis=-1, keepdims=True)
    h = x * lax.rsqrt(ms + EPS) * g_ref[...]
    h_hi = h.astype(BF16)
    h_lo = (h - h_hi.astype(F32)).astype(BF16)
    u = pltpu.bitcast(h, jnp.uint32)
    r = (u + jnp.uint32(0x7FFF) + ((u >> 16) & jnp.uint32(1))) >> 16
    tok, words, _ = h_ref.shape
    for s in range(words):
        lo = r[:, 2 * s * LANES:(2 * s + 1) * LANES]
        hi = r[:, (2 * s + 1) * LANES:(2 * s + 2) * LANES]
        h_scr[pl.ds(s, tok, stride=words), :] = lo | (hi << 16)
    h_ref[...] = h_scr[...].reshape(tok, words, LANES)
    logits = (jnp.dot(h_hi, whi_ref[...], preferred_element_type=F32)
              + jnp.dot(h_lo, whi_ref[...], preferred_element_type=F32)
              + jnp.dot(h_hi, wlo_ref[...], preferred_element_type=F32)) + b_ref[...]

    tm = x.shape[0]
    lane = lax.broadcasted_iota(jnp.int32, (tm, LANES), 1).astype(F32)
    work = logits
    multihot = jnp.zeros((tm, LANES), F32)
    vals, idxs = [], []
    for _ in range(TOP_K):
        m = jnp.max(work, axis=-1, keepdims=True)
        ik = jnp.min(jnp.where(work == m, lane, float(LANES)), axis=-1, keepdims=True)
        sel = lane == ik
        work = jnp.where(sel, -jnp.inf, work)
        multihot = jnp.where(sel, 1.0, multihot)
        vals.append(m)
        idxs.append(ik)
    es = [jnp.exp(v - vals[0]) for v in vals]
    denom = es[0] + es[1] + es[2] + es[3]

    cum = jnp.dot(tri_ref[...], multihot.astype(BF16), preferred_element_type=F32) + carry[0:1, :]
    idx_t = jnp.zeros((tm, LANES), F32)
    gate_t = jnp.zeros((tm, LANES), F32)
    rank_t = jnp.zeros((tm, LANES), F32)
    for k in range(TOP_K):
        rk = jnp.sum(jnp.where(lane == idxs[k], cum, 0.0), axis=-1, keepdims=True)
        here = lane == float(k)
        idx_t = jnp.where(here, idxs[k], idx_t)
        gate_t = jnp.where(here, es[k] / denom, gate_t)
        rank_t = jnp.where(here, rk, rank_t)
    idx_ref[...] = idx_t.astype(jnp.int32)
    gate_ref[...] = gate_t
    rank_ref[...] = rank_t.astype(jnp.int32)
    new_carry = carry[0:1, :] + jnp.sum(multihot, axis=0, keepdims=True)
    carry[...] = jnp.broadcast_to(new_carry, carry.shape)
    cnt_ref[...] = jnp.broadcast_to(new_carry, cnt_ref.shape)


def _router(x2d, g2, w_router, b_router, tm=256):
    t, d = x2d.shape
    ne = w_router.shape[1]
    w_pad = jnp.zeros((d, LANES), F32).at[:, :ne].set(w_router)
    w_hi = w_pad.astype(BF16)
    w_lo = (w_pad - w_hi.astype(F32)).astype(BF16)
    b_pad = jnp.full((1, LANES), NEG_BIG, F32).at[0, :ne].set(b_router)
    tri = jnp.asarray(np.tril(np.ones((tm, tm), np.float32), -1), dtype=BF16)
    tile = lambda dt: jax.ShapeDtypeStruct((t, LANES), dt)
    row = pl.BlockSpec((tm, LANES), lambda i: (i, 0))
    const = lambda shape: pl.BlockSpec(shape, lambda i: (0, 0))
    return pl.pallas_call(
        _router_kernel, grid=(t // tm,),
        in_specs=[pl.BlockSpec((tm, d), lambda i: (i, 0)), const((1, d)), const((d, LANES)),
                  const((d, LANES)), const((1, LANES)), const((tm, tm))],
        out_specs=(pl.BlockSpec((tm, d // (2 * LANES), LANES), lambda i: (i, 0, 0)), row, row, row,
                   const((8, LANES))),
        out_shape=(jax.ShapeDtypeStruct((t, d // (2 * LANES), LANES), jnp.uint32),
                   tile(jnp.int32), tile(F32), tile(jnp.int32),
                   jax.ShapeDtypeStruct((8, LANES), F32)),
        scratch_shapes=[pltpu.VMEM((8, LANES), F32), pltpu.VMEM((tm * (d // (2 * LANES)), LANES), jnp.uint32)],
        compiler_params=_cparams(1), name="router",
    )(x2d, g2, w_hi, w_lo, b_pad, tri)


INDEX_SLOTS = 3
ROW_SLOTS = 2


def _row_prefetch_ring(idx_hbm, idx_smem, isem, start_rows, wait_rows):
    i = pl.program_id(0)
    n = pl.num_programs(0)

    def idx_copy(step):
        slot = lax.rem(step, INDEX_SLOTS)
        return pltpu.make_async_copy(idx_hbm.at[step], idx_smem.at[slot], isem.at[slot])

    @pl.when(i == 0)
    def _():
        idx_copy(0).start()
        idx_copy(0).wait()
        start_rows(0, 0, 0)

        @pl.when(n > 1)
        def _():
            idx_copy(1).start()

    @pl.when(i + 1 < n)
    def _():
        idx_copy(i + 1).wait()
        start_rows(i + 1, lax.rem(i + 1, INDEX_SLOTS), lax.rem(i + 1, ROW_SLOTS))

    @pl.when(i + 2 < n)
    def _():
        idx_copy(i + 2).start()

    row_slot = lax.rem(i, ROW_SLOTS)
    wait_rows(row_slot)
    return row_slot


def _gather_kernel(tok_hbm, h_hbm, xs_ref, tok_smem, buf, isem, rsem):
    g = GATHER_ROWS
    words = h_hbm.shape[1]

    def start_rows(step, idx_slot, row_slot):
        def body(r2, c):
            for prio in range(2):
                r = 2 * r2 + prio
                t = tok_smem[idx_slot, r]
                dst = buf.at[row_slot, pl.ds(pl.multiple_of(r * words, words), words)]
                pltpu.make_async_copy(h_hbm.at[t], dst, rsem.at[row_slot]).start(priority=prio)
            return c
        lax.fori_loop(0, g // 2, body, 0)

    def wait_rows(row_slot):
        pltpu.make_async_copy(buf.at[row_slot], buf.at[row_slot], rsem.at[row_slot]).wait()

    row_slot = _row_prefetch_ring(tok_hbm, tok_smem, isem, start_rows, wait_rows)

    for s in range(words):
        x = buf[row_slot, pl.ds(s, g, stride=words), :]
        lo = pltpu.bitcast(x << 16, F32)
        hi = pltpu.bitcast(x & jnp.uint32(0xFFFF0000), F32)
        xs_ref[:, 2 * s * LANES:(2 * s + 1) * LANES] = lo.astype(BF16)
        xs_ref[:, (2 * s + 1) * LANES:(2 * s + 2) * LANES] = hi.astype(BF16)


def _gather_rows(h_packed, row_tok):
    t, words, _ = h_packed.shape
    r_max = row_tok.shape[0]
    g = GATHER_ROWS
    n_steps = r_max // g
    tok2d = row_tok.reshape(n_steps, g)
    return pl.pallas_call(
        _gather_kernel, grid=(n_steps,),
        in_specs=[pl.BlockSpec(memory_space=pl.ANY), pl.BlockSpec(memory_space=pl.ANY)],
        out_specs=pl.BlockSpec((g, 2 * words * LANES), lambda i: (i, 0)),
        scratch_shapes=[pltpu.SMEM((INDEX_SLOTS, g), jnp.int32),
                        pltpu.VMEM((ROW_SLOTS, g * words, LANES), jnp.uint32),
                        pltpu.SemaphoreType.DMA((INDEX_SLOTS,)), pltpu.SemaphoreType.DMA((ROW_SLOTS,))],
        out_shape=jax.ShapeDtypeStruct((r_max, 2 * words * LANES), BF16),
        compiler_params=_cparams(1), name="row_gather",
    )(tok2d, h_packed)


PREP_ROWS = 1024


def _tile_clamp(i, na):
    return jnp.minimum(i, na[0] - 1)


def _weights_changed(te_ref, na_ref, i):
    cur = _tile_clamp(i, na_ref)
    prev = jnp.maximum(cur - 1, 0)
    return (i < na_ref[0]) & ((i == 0) | (te_ref[cur] != te_ref[prev]))


def _row_cases(na_ref, tv_ref, i, tm, compute, zero):
    active = i < na_ref[0]
    valid = tv_ref[_tile_clamp(i, na_ref)]

    @pl.when(active & (valid > tm // 2))
    def _():
        compute(tm)

    @pl.when(active & (valid <= tm // 2))
    def _():
        compute(tm // 2)

    @pl.when(jnp.logical_not(active))
    def _():
        zero()


def _gateup_kernel(te_ref, na_ref, tv_ref, xs_ref, w_ref, b_ref, pe_ref, h_ref, w_s):
    i = pl.program_id(1)
    d, tc = w_s.shape
    tm = h_ref.shape[0]

    @pl.when(_weights_changed(te_ref, na_ref, i))
    def _():
        for r0 in range(0, d, PREP_ROWS):
            rows = slice(r0, min(r0 + PREP_ROWS, d))
            w_s[rows, :] = w_ref[rows, :].astype(BF16)

    def compute(nrows):
        gu = jnp.dot(xs_ref[0:nrows, :], w_s[...], preferred_element_type=F32) + b_ref[...]
        acts = []
        for c in range(tc // LANES):
            g = gu[:, c * LANES:(c + 1) * LANES]
            u = pltpu.roll(g, LANES - 1, axis=1)
            gate = jnp.minimum(g, SWIGLU_LIMIT)
            up = jnp.clip(u, -SWIGLU_LIMIT, SWIGLU_LIMIT)
            acts.append(((up + 1.0) * gate * jax.nn.sigmoid(SWIGLU_ALPHA * gate)).astype(BF16))
        act = jnp.concatenate(acts, axis=1)
        h_ref[0:nrows, :] = jnp.dot(act, pe_ref[...], preferred_element_type=F32).astype(h_ref.dtype)
        if nrows < tm:
            h_ref[nrows:tm, :] = jnp.zeros((tm - nrows, h_ref.shape[1]), h_ref.dtype)

    def zero():
        h_ref[...] = jnp.zeros_like(h_ref)

    _row_cases(na_ref, tv_ref, i, tm, compute, zero)


def _down_kernel(te_ref, na_ref, tv_ref, h_ref, w_ref, bd_ref, rw_ref, o_ref, wd_s, o_scr):
    i = pl.program_id(1)
    tm, chunks, _ = o_ref.shape

    @pl.when(_weights_changed(te_ref, na_ref, i))
    def _():
        ff = wd_s.shape[0]
        for r0 in range(0, ff, PREP_ROWS):
            rows = slice(r0, min(r0 + PREP_ROWS, ff))
            wd_s[rows, :] = w_ref[rows, :].astype(BF16)

    def compute(nrows):
        out = jnp.dot(h_ref[0:nrows, :], wd_s[...], preferred_element_type=F32)
        out = (out + bd_ref[...]) * rw_ref[0:nrows, :]
        for c in range(chunks):
            o_scr[pl.ds(c, nrows, stride=chunks), :] = out[:, c * LANES:(c + 1) * LANES]
        if nrows < tm:
            o_scr[nrows * chunks:tm * chunks, :] = jnp.zeros(((tm - nrows) * chunks, LANES), F32)
        o_ref[...] = o_scr[...].reshape(tm, chunks, LANES)

    def zero():
        o_ref[...] = jnp.zeros_like(o_ref)

    _row_cases(na_ref, tv_ref, i, tm, compute, zero)


def _experts(xs, row_w, tile_e, n_active, tile_valid, w_gate_up, b_gate_up, w_down, bd, tn_ff, tn_d):
    r_max, d = xs.shape
    ff = w_down.shape[1]
    tm = EXPERT_TM
    n_tiles = r_max // tm
    tc = 2 * tn_ff
    p_even = jnp.asarray(np.arange(tc)[:, None] == 2 * np.arange(tn_ff)[None, :], dtype=BF16)

    def row_tile(j, i, te, na, tv):
        return (_tile_clamp(i, na), 0)

    def expert_cols(j, i, te, na, tv):
        return (te[_tile_clamp(i, na)], 0, j)

    gu_spec = pltpu.PrefetchScalarGridSpec(
        num_scalar_prefetch=3, grid=(ff // tn_ff, n_tiles),
        in_specs=[
            pl.BlockSpec((tm, d), row_tile),
            pl.BlockSpec((None, d, tc), expert_cols),
            pl.BlockSpec((None, 1, tc), expert_cols),
            pl.BlockSpec((tc, tn_ff), lambda j, i, te, na, tv: (0, 0)),
        ],
        out_specs=pl.BlockSpec((tm, tn_ff), lambda j, i, te, na, tv: (i, j)),
        scratch_shapes=[pltpu.VMEM((d, tc), BF16)])
    hidden = pl.pallas_call(
        _gateup_kernel, grid_spec=gu_spec, out_shape=jax.ShapeDtypeStruct((r_max, ff), BF16),
        compiler_params=_cparams(2), name="expert_gate_up",
    )(tile_e, n_active, tile_valid, xs, w_gate_up, b_gate_up, p_even)
    dn_spec = pltpu.PrefetchScalarGridSpec(
        num_scalar_prefetch=3, grid=(d // tn_d, n_tiles),
        in_specs=[
            pl.BlockSpec((tm, ff), row_tile),
            pl.BlockSpec((None, ff, tn_d), expert_cols),
            pl.BlockSpec((None, 1, tn_d), expert_cols),
            pl.BlockSpec((tm, 1), row_tile),
        ],
        out_specs=pl.BlockSpec((tm, tn_d // LANES, LANES), lambda j, i, te, na, tv: (i, j, 0)),
        scratch_shapes=[pltpu.VMEM((ff, tn_d), BF16), pltpu.VMEM((tm * (tn_d // LANES), LANES), F32)])
    return pl.pallas_call(
        _down_kernel, grid_spec=dn_spec, out_shape=jax.ShapeDtypeStruct((r_max, d // LANES, LANES), F32),
        compiler_params=_cparams(2), name="expert_down",
    )(tile_e, n_active, tile_valid, hidden, w_down, bd, row_w)


def _combine_kernel(dest_hbm, rows_hbm, x_ref, o_ref, dest_smem, buf, sum_scr, isem, rsem):
    tm = COMBINE_TM
    chunks = rows_hbm.shape[1]
    seg = tm * chunks

    def start_rows(step, idx_slot, row_slot):
        def body(r, c):
            for k in range(TOP_K):
                src = dest_smem[idx_slot, r * TOP_K + k]
                off = pl.multiple_of((k * tm + r) * chunks, chunks)
                pltpu.make_async_copy(rows_hbm.at[src], buf.at[row_slot, pl.ds(off, chunks)],
                                      rsem.at[row_slot]).start()
            return c
        lax.fori_loop(0, tm, body, 0, unroll=2)

    def wait_rows(row_slot):
        pltpu.make_async_copy(buf.at[row_slot], buf.at[row_slot], rsem.at[row_slot]).wait()

    row_slot = _row_prefetch_ring(dest_hbm, dest_smem, isem, start_rows, wait_rows)

    acc = buf[row_slot, pl.ds(0, seg), :]
    for k in range(1, TOP_K):
        acc = acc + buf[row_slot, pl.ds(k * seg, seg), :]
    sum_scr[...] = acc
    for c in range(chunks):
        cols = slice(c * LANES, (c + 1) * LANES)
        o_ref[:, cols] = x_ref[:, cols] + sum_scr[pl.ds(c, tm, stride=chunks), :]


def _combine(rows, dest, x2d):
    t, d = x2d.shape
    chunks = rows.shape[1]
    tm = COMBINE_TM
    n_steps = t // tm
    dest2d = dest.reshape(n_steps, tm * TOP_K)
    blk = pl.BlockSpec((tm, d), lambda i: (i, 0))
    return pl.pallas_call(
        _combine_kernel, grid=(n_steps,),
        in_specs=[pl.BlockSpec(memory_space=pl.ANY), pl.BlockSpec(memory_space=pl.ANY), blk],
        out_specs=blk,
        out_shape=jax.ShapeDtypeStruct((t, d), F32),
        scratch_shapes=[pltpu.SMEM((INDEX_SLOTS, tm * TOP_K), jnp.int32),
                        pltpu.VMEM((ROW_SLOTS, TOP_K * tm * chunks, LANES), F32),
                        pltpu.VMEM((tm * chunks, LANES), F32),
                        pltpu.SemaphoreType.DMA((INDEX_SLOTS,)), pltpu.SemaphoreType.DMA((ROW_SLOTS,))],
        compiler_params=_cparams(1), name="combine",
    )(dest2d, rows, x2d)


def _moe(x_mid, g2, w_router, b_router, w_gate_up, b_gate_up, w_down, b_down, tn_ff, tn_d):
    t, d = x_mid.shape
    ne = w_router.shape[1]
    h2, idx_t, gate_t, rank_t, cnt = _router(x_mid, g2, w_router, b_router)

    tm = EXPERT_TM
    r_max = t * TOP_K + ne * tm
    n_tiles = r_max // tm
    counts = cnt[0, :ne].astype(jnp.int32)
    padded = (counts + tm - 1) // tm * tm
    pend = jnp.cumsum(padded)
    pstart = pend - padded
    idx = idx_t[:, :TOP_K]
    dest = (pstart[idx] + rank_t[:, :TOP_K]).astype(jnp.int32)
    n_rows_used = pend[-1:].astype(jnp.int32)
    n_active = n_rows_used // tm
    tok = jnp.repeat(jnp.arange(t, dtype=jnp.int32), TOP_K)
    gate_bits = lax.bitcast_convert_type(gate_t[:, :TOP_K].reshape(-1), jnp.int32)
    table = jnp.zeros((r_max, 2), jnp.int32).at[dest.reshape(-1)].set(
        jnp.stack([tok, gate_bits], axis=1), unique_indices=True)
    row_tok = table[:, 0]
    row_w = lax.bitcast_convert_type(table[:, 1], F32)[:, None]
    tile_start = jnp.arange(n_tiles, dtype=jnp.int32) * tm
    tile_e = jnp.minimum(jnp.sum((pend[None, :] <= tile_start[:, None]).astype(jnp.int32), axis=1), ne - 1)
    tile_valid = jnp.clip((pstart + counts)[tile_e] - tile_start, 0, tm).astype(jnp.int32)

    xs = _gather_rows(h2, row_tok)

    rows = _experts(xs, row_w, tile_e, n_active, tile_valid, w_gate_up, b_gate_up[:, None, :], w_down,
                    b_down[:, None, :], tn_ff, tn_d)
    return _combine(rows, dest, x_mid)


def _alibi_slopes(n):
    return (2.0 ** (-ALIBI_MAX_BIAS * (np.arange(n, dtype=np.float32) + 1.0) / n)).astype(np.float32)


def _mixer(x2d, batch, seq, norm1_g, w_in, q_norm_swa, k_norm_swa, q_norm_dil, k_norm_dil, sinks, w_out,
           n_swa, n_kv, n_dil, tm_proj, tn_out):
    t, d = x2d.shape
    swa_q, swa_kv, dil_w = n_swa * HEAD_DIM, n_kv * HEAD_DIM, n_dil * HEAD_DIM
    assert n_swa // n_kv == GQA_GROUP and seq % (16 * ATTN_BLOCK) == 0
    s1, s2, s3 = swa_q, swa_q + swa_kv, swa_q + 2 * swa_kv

    def dup(wc):
        wc = wc.reshape(d, n_kv, HEAD_DIM)
        return jnp.concatenate([wc, wc], axis=-1).reshape(d, 2 * swa_kv)

    w = jnp.concatenate([w_in[:, :s1], dup(w_in[:, s1:s2]), dup(w_in[:, s2:s3]), w_in[:, s3:]],
                        axis=1).astype(BF16)
    scale = HEAD_DIM ** -0.5
    ones = lambda n: jnp.ones((n,), F32)
    gains = jnp.concatenate([
        jnp.tile(q_norm_swa * scale, n_swa), jnp.tile(k_norm_swa, 2 * n_kv), ones(2 * swa_kv),
        jnp.tile(q_norm_dil * scale, n_dil), jnp.tile(k_norm_dil, n_dil), ones(dil_w)])[None, :]
    c = w.shape[1]
    bounds = np.cumsum([0, swa_q, 2 * swa_kv, 2 * swa_kv, dil_w, dil_w, dil_w])
    assert all(b % PROJ_TN == 0 for b in bounds)
    seg_norm = [1, 1, 0, 1, 1, 0]
    flags = np.zeros((c // PROJ_TN,), np.int32)
    for sidx in range(6):
        flags[bounds[sidx] // PROJ_TN:bounds[sidx + 1] // PROJ_TN] = seg_norm[sidx]
    qa0, ka0, va0, qb0, kb0, vb0 = (int(b) for b in bounds[:6])
    proj, *residue_major = _inproj(x2d, norm1_g[None, :], w, gains, jnp.asarray(flags), tm_proj, dil_col0=qb0)
    by_dilation = dict(zip(RESIDUE_DILATIONS, residue_major))

    sink_row = jnp.repeat(sinks.astype(F32), HEAD_DIM)[None, :]
    o_a = _band_attention(proj[None], jnp.asarray(_alibi_slopes(n_swa)), batch=batch, seq=seq, dilation=1,
                          q_col0=qa0, k_col0=ka0, v_col0=va0, n_heads=n_swa, kv_shared=True,
                          max_diff=SWA_WINDOW - 1, sinks=sink_row)[0]
    slopes_dil = jnp.asarray(_alibi_slopes(n_dil))
    outs, lses = [], []
    for window, dil in DILATED_BRANCHES:
        src, col0 = (proj[None], qb0) if dil == 1 else (by_dilation[dil], 0)
        o_i, lse_i = _band_attention(src, slopes_dil, batch=batch, seq=seq, dilation=dil,
                                     q_col0=col0, k_col0=col0 + kb0 - qb0, v_col0=col0 + vb0 - qb0,
                                     n_heads=n_dil, kv_shared=False, max_diff=window // dil)
        outs.append(o_i)
        lses.append(lse_i)
    o_b = _merge(outs, lses, [dil for _, dil in DILATED_BRANCHES])
    return _outproj(o_a, o_b, w_out.astype(BF16), x2d, tm_proj, tn_out)


def kernel(x, norm1_g, w_in, q_norm_swa, k_norm_swa, q_norm_dil, k_norm_dil, sinks, w_out, norm2_g,
           w_router, b_router, w_gate_up, b_gate_up, w_down, b_down):
    b, s, d = x.shape
    depth = norm1_g.shape[0]
    n_heads = d // HEAD_DIM
    n_swa = n_heads // 2
    n_kv = n_swa // GQA_GROUP
    n_dil = n_heads - n_swa
    x2d = x.reshape(b * s, d)
    for l in range(depth):
        x_mid = _mixer(x2d, b, s, norm1_g[l], w_in[l], q_norm_swa[l], k_norm_swa[l], q_norm_dil[l],
                       k_norm_dil[l], sinks[l], w_out[l], n_swa, n_kv, n_dil, tm_proj=512, tn_out=1024)
        x2d = _moe(x_mid, norm2_g[l][None, :], w_router[l], b_router[l], w_gate_up[l], b_gate_up[l],
                   w_down[l], b_down[l], tn_ff=512, tn_d=1024)
    return x2d.reshape(b, s, d)
```

```python
import functools

import jax
import jax.numpy as jnp
import numpy as np
from jax import lax
from jax.experimental import pallas as pl
from jax.experimental.pallas import tpu as pltpu

F32 = jnp.float32
BF16 = jnp.bfloat16

HEAD_DIM = 64
LANES = 128
ATTN_BLOCK = 128
GQA_GROUP = 8
SWA_WINDOW = 128
DILATED_BRANCHES = ((128, 1), (512, 4), (2048, 16))
N_EXPERTS = 32
TOP_K = 4
SWIGLU_LIMIT = 7.0
SWIGLU_ALPHA = 1.702
ALIBI_MAX_BIAS = 8.0
EPS = 1e-6
MASK_DIST = 1e30
NEG_BIG = -1e30

VMEM_LIMIT = 58 * 1024 * 1024

PROJ_TN = 512
ATTN_W_CHOICES = (2048, 1024, 512)
PAIRS_PER_KV = GQA_GROUP // 2
EXPERT_TM = 512
GATHER_ROWS = 256
COMBINE_TM = 128


def _cparams(n_axes):
    return pltpu.CompilerParams(dimension_semantics=("arbitrary",) * n_axes,
                                vmem_limit_bytes=VMEM_LIMIT)


RESIDUE_DILATIONS = tuple(d for _, d in DILATED_BRANCHES if d > 1)


def _inproj_kernel(flag_ref, x_ref, g_ref, w_ref, gain_ref, ones_ref, o_ref, *rest, dil_tile0):
    res_refs, (h_scr, y_scr) = rest[:len(RESIDUE_DILATIONS)], rest[len(RESIDUE_DILATIONS):]
    j = pl.program_id(1)
    tm, tn = o_ref.shape

    @pl.when(j == 0)
    def _():
        x = x_ref[...]
        ms = jnp.mean(x * x, axis=-1, keepdims=True)
        h_scr[...] = (x * lax.rsqrt(ms + EPS) * g_ref[...]).astype(BF16)

    acc = jnp.dot(h_scr[...], w_ref[...], preferred_element_type=F32)

    def emit(y):
        o_ref[...] = y.astype(BF16)
        for ch in range(tn // LANES):
            y_scr[ch] = y[:, ch * LANES:(ch + 1) * LANES]

    @pl.when(flag_ref[j] == 1)
    def _():
        ss = jnp.dot((acc * acc).astype(BF16), ones_ref[...], preferred_element_type=F32)
        emit(acc * lax.rsqrt(ss * (1.0 / HEAD_DIM) + EPS) * gain_ref[...])

    @pl.when(flag_ref[j] == 0)
    def _():
        emit(acc)

    @pl.when(j >= dil_tile0)
    def _():
        for d, ref in zip(RESIDUE_DILATIONS, res_refs):
            for r in range(d):
                for ch in range(tn // LANES):
                    rows = y_scr[ch, pl.ds(r, tm // d, stride=d), :]
                    ref[r, :, ch * LANES:(ch + 1) * LANES] = rows.astype(BF16)


def _inproj(x2d, g1, w, gains, flags, tm, dil_col0):
    t, d = x2d.shape
    c = w.shape[1]
    tn = PROJ_TN
    dil_tile0 = dil_col0 // tn
    cb = c - dil_col0
    head_id = np.arange(tn) // HEAD_DIM
    ones_bd = jnp.asarray(head_id[:, None] == head_id[None, :], dtype=BF16)
    res_specs = [pl.BlockSpec((dd, tm // dd, tn), lambda i, j, f: (0, i, jnp.maximum(j - dil_tile0, 0)))
                 for dd in RESIDUE_DILATIONS]
    res_shapes = [jax.ShapeDtypeStruct((dd, t // dd, cb), BF16) for dd in RESIDUE_DILATIONS]
    grid_spec = pltpu.PrefetchScalarGridSpec(
        num_scalar_prefetch=1,
        grid=(t // tm, c // tn),
        in_specs=[
            pl.BlockSpec((tm, d), lambda i, j, f: (i, 0)),
            pl.BlockSpec((1, d), lambda i, j, f: (0, 0)),
            pl.BlockSpec((d, tn), lambda i, j, f: (0, j)),
            pl.BlockSpec((1, tn), lambda i, j, f: (0, j)),
            pl.BlockSpec((tn, tn), lambda i, j, f: (0, 0)),
        ],
        out_specs=[pl.BlockSpec((tm, tn), lambda i, j, f: (i, j))] + res_specs,
        scratch_shapes=[pltpu.VMEM((tm, d), BF16), pltpu.VMEM((tn // LANES, tm, LANES), F32)],
    )
    return pl.pallas_call(
        functools.partial(_inproj_kernel, dil_tile0=dil_tile0), grid_spec=grid_spec,
        out_shape=[jax.ShapeDtypeStruct((t, c), BF16)] + res_shapes,
        compiler_params=_cparams(2), name="inproj",
    )(flags, x2d, g1, w, gains, ones_bd)


def _attn_kernel(slope_ref, q_ref, kp_ref, kc_ref, vp_ref, vc_ref, *rest,
                 n_pairs, kv_shared, max_diff, dist_scale, with_sink):
    if with_sink:
        sink_ref, o_ref = rest
        lse_ref = None
    else:
        o_ref, lse_ref = rest
    blk = pl.program_id(2)
    cg = pl.program_id(3)

    qi = lax.broadcasted_iota(jnp.int32, (ATTN_BLOCK, 2 * ATTN_BLOCK), 0)
    kj = lax.broadcasted_iota(jnp.int32, (ATTN_BLOCK, 2 * ATTN_BLOCK), 1)
    dist = qi + ATTN_BLOCK - kj
    valid = (dist >= 0) & (dist <= max_diff) & ((blk > 0) | (kj >= ATTN_BLOCK))
    dist_m = jnp.where(valid, (dist * dist_scale).astype(F32), MASK_DIST)

    lane = lax.broadcasted_iota(jnp.int32, (ATTN_BLOCK, LANES), 1)
    low = lane < HEAD_DIM

    for p in range(n_pairs):
        cols = slice(p * LANES, (p + 1) * LANES)
        kv = p // PAIRS_PER_KV
        kcols = slice(kv * LANES, (kv + 1) * LANES) if kv_shared else cols
        q2 = q_ref[:, cols]
        kk = jnp.concatenate([kp_ref[:, kcols], kc_ref[:, kcols]], axis=0)
        vv = jnp.concatenate([vp_ref[:, kcols], vc_ref[:, kcols]], axis=0)
        outs, lses = [], []
        for hh in range(2):
            slope = slope_ref[cg * (2 * n_pairs) + 2 * p + hh]
            qm = jnp.where(low if hh == 0 else ~low, q2, jnp.zeros_like(q2))
            s = lax.dot_general(qm, kk, (((1,), (1,)), ((), ())),
                                preferred_element_type=F32)
            s = s - slope * dist_m
            m = jnp.max(s, axis=-1, keepdims=True)
            e = jnp.exp(s - m)
            l = jnp.sum(e, axis=-1, keepdims=True)
            o = jnp.dot(e.astype(BF16), vv, preferred_element_type=F32)
            outs.append(o / l)
            lses.append(jnp.broadcast_to(m + jnp.log(l), (ATTN_BLOCK, LANES)))
        o2 = jnp.where(low, outs[0], outs[1])
        lse2 = jnp.where(low, lses[0], lses[1])
        if with_sink:
            o2 = o2 * jax.nn.sigmoid(lse2 - sink_ref[:, cols])
        else:
            lse_ref[:, cols] = lse2
        o_ref[:, cols] = o2.astype(o_ref.dtype)


def _band_attention(proj, slopes, *, batch, seq, dilation, q_col0, k_col0, v_col0, n_heads,
                    kv_shared, max_diff, sinks=None):
    d = dilation
    assert proj.shape[0] == d
    lsub = seq // d
    nblk = lsub // ATTN_BLOCK
    width = n_heads * HEAD_DIM

    def kv_width(w):
        return LANES * (w // LANES // PAIRS_PER_KV) if kv_shared else w

    w = next(c for c in ATTN_W_CHOICES
             if width % c == 0 and q_col0 % c == 0 and k_col0 % kv_width(c) == 0 and v_col0 % kv_width(c) == 0)
    n_pairs = w // LANES
    ncg = width // w
    kw = kv_width(w)
    pv = proj

    def qmap(b, r, i, g, s):
        return (r, b * nblk + i, q_col0 // w + g)

    def kvmap(col0, prev):
        def f(b, r, i, g, s):
            blk = jnp.maximum(i - 1, 0) if prev else i
            return (r, b * nblk + blk, col0 // kw + g)
        return f

    def omap(b, r, i, g, s):
        return (r, b * nblk + i, g)

    in_specs = [
        pl.BlockSpec((None, ATTN_BLOCK, w), qmap),
        pl.BlockSpec((None, ATTN_BLOCK, kw), kvmap(k_col0, True)),
        pl.BlockSpec((None, ATTN_BLOCK, kw), kvmap(k_col0, False)),
        pl.BlockSpec((None, ATTN_BLOCK, kw), kvmap(v_col0, True)),
        pl.BlockSpec((None, ATTN_BLOCK, kw), kvmap(v_col0, False)),
    ]
    args = [pv, pv, pv, pv, pv]
    with_sink = sinks is not None
    o_shape = jax.ShapeDtypeStruct((d, batch * lsub, width), BF16)
    o_spec = pl.BlockSpec((None, ATTN_BLOCK, w), omap)
    if with_sink:
        in_specs.append(pl.BlockSpec((1, w), lambda b, r, i, g, s: (0, g)))
        args.append(sinks)
        out_shape, out_specs = o_shape, o_spec
    else:
        out_shape = (o_shape, jax.ShapeDtypeStruct((d, batch * lsub, width), F32))
        out_specs = (o_spec, pl.BlockSpec((None, ATTN_BLOCK, w), omap))
    kern = functools.partial(_attn_kernel, n_pairs=n_pairs, kv_shared=kv_shared,
                             max_diff=max_diff, dist_scale=d, with_sink=with_sink)
    grid_spec = pltpu.PrefetchScalarGridSpec(
        num_scalar_prefetch=1, grid=(batch, d, nblk, ncg),
        in_specs=in_specs, out_specs=out_specs)
    res = pl.pallas_call(kern, grid_spec=grid_spec, out_shape=out_shape,
                         compiler_params=_cparams(4), name=f"band_attn_d{d}" + ("_sink" if with_sink else ""),
                         )(slopes, *args)
    return res


def _merge_kernel(*refs, dilations):
    n = len(dilations)
    o_refs, l_refs, out, scr = refs[:n], refs[n:2 * n], refs[2 * n], refs[2 * n + 1]
    tm, w = out.shape
    nch = w // LANES
    for a, d in enumerate(dilations):
        if d == 1:
            continue
        for b, ref in enumerate((o_refs[a], l_refs[a])):
            for r in range(d):
                for ch in range(nch):
                    scr[2 * a + b, ch, pl.ds(r, tm // d, stride=d), :] = (
                        ref[r, :, ch * LANES:(ch + 1) * LANES].astype(F32))
    for ch in range(nch):
        cols = slice(ch * LANES, (ch + 1) * LANES)
        os_, ls_ = [], []
        for a, d in enumerate(dilations):
            if d == 1:
                os_.append(o_refs[a][0, :, cols].astype(F32))
                ls_.append(l_refs[a][0, :, cols])
            else:
                os_.append(scr[2 * a, ch])
                ls_.append(scr[2 * a + 1, ch])
        m = functools.reduce(jnp.maximum, ls_)
        es = [jnp.exp(l - m) for l in ls_]
        num = functools.reduce(lambda p, q: p + q, [e * o for e, o in zip(es, os_)])
        den = functools.reduce(lambda p, q: p + q, es)
        out[:, cols] = (num / den).astype(out.dtype)


def _merge(os_, ls_, dilations, tm=256):
    w = os_[0].shape[2]
    t = os_[0].shape[0] * os_[0].shape[1]
    specs = [pl.BlockSpec((d, tm // d, w), lambda i: (0, i, 0)) for d in dilations]
    return pl.pallas_call(
        functools.partial(_merge_kernel, dilations=tuple(dilations)), grid=(t // tm,),
        in_specs=specs + specs, out_specs=pl.BlockSpec((tm, w), lambda i: (i, 0)),
        out_shape=jax.ShapeDtypeStruct((t, w), BF16),
        scratch_shapes=[pltpu.VMEM((2 * len(dilations), w // LANES, tm, LANES), F32)],
        compiler_params=_cparams(1), name="branch_merge",
    )(*os_, *ls_)


def _outproj_kernel(oa_ref, ob_ref, wt_ref, wb_ref, x_ref, o_ref):
    acc = jnp.dot(oa_ref[...], wt_ref[...], preferred_element_type=F32)
    acc = acc + jnp.dot(ob_ref[...], wb_ref[...], preferred_element_type=F32)
    o_ref[...] = x_ref[...] + acc


def _outproj(o_a, o_b, w_out, x2d, tm, tn):
    t, d = x2d.shape
    ha, hb = o_a.shape[1], o_b.shape[1]
    assert ha == hb
    return pl.pallas_call(
        _outproj_kernel, grid=(t // tm, d // tn),
        in_specs=[
            pl.BlockSpec((tm, ha), lambda i, j: (i, 0)),
            pl.BlockSpec((tm, hb), lambda i, j: (i, 0)),
            pl.BlockSpec((ha, tn), lambda i, j: (0, j)),
            pl.BlockSpec((hb, tn), lambda i, j: (1, j)),
            pl.BlockSpec((tm, tn), lambda i, j: (i, j)),
        ],
        out_specs=pl.BlockSpec((tm, tn), lambda i, j: (i, j)),
        out_shape=jax.ShapeDtypeStruct((t, d), F32), compiler_params=_cparams(2), name="outproj",
    )(o_a, o_b, w_out, w_out, x2d)


def _router_kernel(x_ref, g_ref, whi_ref, wlo_ref, b_ref, tri_ref,
                   h_ref, idx_ref, gate_ref, rank_ref, cnt_ref, carry, h_scr):
    i = pl.program_id(0)

    @pl.when(i == 0)
    def _():
        carry[...] = jnp.zeros_like(carry)

    x = x_ref[...]
    ms = jnp.mean(x * x, axis=-1, keepdims=True)
    h = x * lax.rsqrt(ms + EPS) * g_ref[...]
    h_hi = h.astype(BF16)
    h_lo = (h - h_hi.astype(F32)).astype(BF16)
    u = pltpu.bitcast(h, jnp.uint32)
    r = (u + jnp.uint32(0x7FFF) + ((u >> 16) & jnp.uint32(1))) >> 16
    tok, words, _ = h_ref.shape
    for s in range(words):
        lo = r[:, 2 * s * LANES:(2 * s + 1) * LANES]
        hi = r[:, (2 * s + 1) * LANES:(2 * s + 2) * LANES]
        h_scr[pl.ds(s, tok, stride=words), :] = lo | (hi << 16)
    h_ref[...] = h_scr[...].reshape(tok, words, LANES)
    logits = (jnp.dot(h_hi, whi_ref[...], preferred_element_type=F32)
              + jnp.dot(h_lo, whi_ref[...], preferred_element_type=F32)
              + jnp.dot(h_hi, wlo_ref[...], preferred_element_type=F32)) + b_ref[...]

    tm = x.shape[0]
    lane = lax.broadcasted_iota(jnp.int32, (tm, LANES), 1).astype(F32)
    work = logits
    multihot = jnp.zeros((tm, LANES), F32)
    vals, idxs = [], []
    for _ in range(TOP_K):
        m = jnp.max(work, axis=-1, keepdims=True)
        ik = jnp.min(jnp.where(work == m, lane, float(LANES)), axis=-1, keepdims=True)
        sel = lane == ik
        work = jnp.where(sel, -jnp.inf, work)
        multihot = jnp.where(sel, 1.0, multihot)
        vals.append(m)
        idxs.append(ik)
    es = [jnp.exp(v - vals[0]) for v in vals]
    denom = es[0] + es[1] + es[2] + es[3]

    cum = jnp.dot(tri_ref[...], multihot.astype(BF16), preferred_element_type=F32) + carry[0:1, :]
    idx_t = jnp.zeros((tm, LANES), F32)
    gate_t = jnp.zeros((tm, LANES), F32)
    rank_t = jnp.zeros((tm, LANES), F32)
    for k in range(TOP_K):
        rk = jnp.sum(jnp.where(lane == idxs[k], cum, 0.0), axis=-1, keepdims=True)
        here = lane == float(k)
        idx_t = jnp.where(here, idxs[k], idx_t)
        gate_t = jnp.where(here, es[k] / denom, gate_t)
        rank_t = jnp.where(here, rk, rank_t)
    idx_ref[...] = idx_t.astype(jnp.int32)
    gate_ref[...] = gate_t
    rank_ref[...] = rank_t.astype(jnp.int32)
    new_carry = carry[0:1, :] + jnp.sum(multihot, axis=0, keepdims=True)
    carry[...] = jnp.broadcast_to(new_carry, carry.shape)
    cnt_ref[...] = jnp.broadcast_to(new_carry, cnt_ref.shape)


def _router(x2d, g2, w_router, b_router, tm=256):
    t, d = x2d.shape
    ne = w_router.shape[1]
    w_pad = jnp.zeros((d, LANES), F32).at[:, :ne].set(w_router)
    w_hi = w_pad.astype(BF16)
    w_lo = (w_pad - w_hi.astype(F32)).astype(BF16)
    b_pad = jnp.full((1, LANES), NEG_BIG, F32).at[0, :ne].set(b_router)
    tri = jnp.asarray(np.tril(np.ones((tm, tm), np.float32), -1), dtype=BF16)
    tile = lambda dt: jax.ShapeDtypeStruct((t, LANES), dt)
    row = pl.BlockSpec((tm, LANES), lambda i: (i, 0))
    const = lambda shape: pl.BlockSpec(shape, lambda i: (0, 0))
    return pl.pallas_call(
        _router_kernel, grid=(t // tm,),
        in_specs=[pl.BlockSpec((tm, d), lambda i: (i, 0)), const((1, d)), const((d, LANES)),
                  const((d, LANES)), const((1, LANES)), const((tm, tm))],
        out_specs=(pl.BlockSpec((tm, d // (2 * LANES), LANES), lambda i: (i, 0, 0)), row, row, row,
                   const((8, LANES))),
        out_shape=(jax.ShapeDtypeStruct((t, d // (2 * LANES), LANES), jnp.uint32),
                   tile(jnp.int32), tile(F32), tile(jnp.int32),
                   jax.ShapeDtypeStruct((8, LANES), F32)),
        scratch_shapes=[pltpu.VMEM((8, LANES), F32), pltpu.VMEM((tm * (d // (2 * LANES)), LANES), jnp.uint32)],
        compiler_params=_cparams(1), name="router",
    )(x2d, g2, w_hi, w_lo, b_pad, tri)


INDEX_SLOTS = 3
ROW_SLOTS = 2


def _row_prefetch_ring(idx_hbm, idx_smem, isem, start_rows, wait_rows):
    i = pl.program_id(0)
    n = pl.num_programs(0)

    def idx_copy(step):
        slot = lax.rem(step, INDEX_SLOTS)
        return pltpu.make_async_copy(idx_hbm.at[step], idx_smem.at[slot], isem.at[slot])

    @pl.when(i == 0)
    def _():
        idx_copy(0).start()
        idx_copy(0).wait()
        start_rows(0, 0, 0)

        @pl.when(n > 1)
        def _():
            idx_copy(1).start()

    @pl.when(i + 1 < n)
    def _():
        idx_copy(i + 1).wait()
        start_rows(i + 1, lax.rem(i + 1, INDEX_SLOTS), lax.rem(i + 1, ROW_SLOTS))

    @pl.when(i + 2 < n)
    def _():
        idx_copy(i + 2).start()

    row_slot = lax.rem(i, ROW_SLOTS)
    wait_rows(row_slot)
    return row_slot


def _gather_kernel(tok_hbm, h_hbm, xs_ref, tok_smem, buf, isem, rsem):
    g = GATHER_ROWS
    words = h_hbm.shape[1]

    def start_rows(step, idx_slot, row_slot):
        def body(r2, c):
            for prio in range(2):
                r = 2 * r2 + prio
                t = tok_smem[idx_slot, r]
                dst = buf.at[row_slot, pl.ds(pl.multiple_of(r * words, words), words)]
                pltpu.make_async_copy(h_hbm.at[t], dst, rsem.at[row_slot]).start(priority=prio)
            return c
        lax.fori_loop(0, g // 2, body, 0)

    def wait_rows(row_slot):
        pltpu.make_async_copy(buf.at[row_slot], buf.at[row_slot], rsem.at[row_slot]).wait()

    row_slot = _row_prefetch_ring(tok_hbm, tok_smem, isem, start_rows, wait_rows)

    for s in range(words):
        x = buf[row_slot, pl.ds(s, g, stride=words), :]
        lo = pltpu.bitcast(x << 16, F32)
        hi = pltpu.bitcast(x & jnp.uint32(0xFFFF0000), F32)
        xs_ref[:, 2 * s * LANES:(2 * s + 1) * LANES] = lo.astype(BF16)
        xs_ref[:, (2 * s + 1) * LANES:(2 * s + 2) * LANES] = hi.astype(BF16)


def _gather_rows(h_packed, row_tok):
    t, words, _ = h_packed.shape
    r_max = row_tok.shape[0]
    g = GATHER_ROWS
    n_steps = r_max // g
    tok2d = row_tok.reshape(n_steps, g)
    return pl.pallas_call(
        _gather_kernel, grid=(n_steps,),
        in_specs=[pl.BlockSpec(memory_space=pl.ANY), pl.BlockSpec(memory_space=pl.ANY)],
        out_specs=pl.BlockSpec((g, 2 * words * LANES), lambda i: (i, 0)),
        scratch_shapes=[pltpu.SMEM((INDEX_SLOTS, g), jnp.int32),
                        pltpu.VMEM((ROW_SLOTS, g * words, LANES), jnp.uint32),
                        pltpu.SemaphoreType.DMA((INDEX_SLOTS,)), pltpu.SemaphoreType.DMA((ROW_SLOTS,))],
        out_shape=jax.ShapeDtypeStruct((r_max, 2 * words * LANES), BF16),
        compiler_params=_cparams(1), name="row_gather",
    )(tok2d, h_packed)


PREP_ROWS = 1024


def _tile_clamp(i, na):
    return jnp.minimum(i, na[0] - 1)


def _weights_changed(te_ref, na_ref, i):
    cur = _tile_clamp(i, na_ref)
    prev = jnp.maximum(cur - 1, 0)
    return (i < na_ref[0]) & ((i == 0) | (te_ref[cur] != te_ref[prev]))


def _stream_expert_weights(w_hbm, raw, wsem, te_ref, meta_ref, nxt_ref, eidx_ref, convert):
    j = pl.program_id(0)
    i = pl.program_id(1)
    nj = pl.num_programs(0)
    tc = raw.shape[2]
    cur = _tile_clamp(i, meta_ref)

    def copy(e, jj, slot):
        col = pl.multiple_of(jj * tc, tc)
        return pltpu.make_async_copy(w_hbm.at[e, :, pl.ds(col, tc)], raw.at[slot], wsem.at[slot])

    @pl.when(_weights_changed(te_ref, meta_ref, i))
    def _():
        e = te_ref[cur]
        b = j * meta_ref[1] + eidx_ref[cur]
        slot = lax.rem(b, 2)
        nxt = nxt_ref[cur]

        @pl.when(b == 0)
        def _():
            copy(e, j, 0).start()

        @pl.when(nxt >= 0)
        def _():
            copy(nxt, j, 1 - slot).start()

        @pl.when((nxt < 0) & (j + 1 < nj))
        def _():
            copy(meta_ref[2], j + 1, 1 - slot).start()

        copy(e, j, slot).wait()
        convert(raw.at[slot])


def _row_cases(na_ref, tv_ref, i, tm, compute, zero):
    active = i < na_ref[0]
    valid = tv_ref[_tile_clamp(i, na_ref)]

    @pl.when(active & (valid > tm // 2))
    def _():
        compute(tm)

    @pl.when(active & (valid <= tm // 2))
    def _():
        compute(tm // 2)

    @pl.when(jnp.logical_not(active))
    def _():
        zero()


def _cast_rows(dst, src):
    n = dst.shape[0]
    for r0 in range(0, n, PREP_ROWS):
        rows = slice(r0, min(r0 + PREP_ROWS, n))
        dst[rows, :] = src[rows, :].astype(dst.dtype)


def _gateup_kernel(te_ref, na_ref, tv_ref, nxt_ref, eidx_ref, xs_ref, w_hbm, b_ref, pe_ref, h_ref,
                   w_s, raw, wsem):
    i = pl.program_id(1)
    d, tc = w_s.shape
    tm = h_ref.shape[0]
    _stream_expert_weights(w_hbm, raw, wsem, te_ref, na_ref, nxt_ref, eidx_ref,
                           functools.partial(_cast_rows, w_s))

    def compute(nrows):
        gu = jnp.dot(xs_ref[0:nrows, :], w_s[...], preferred_element_type=F32) + b_ref[...]
        acts = []
        for c in range(tc // LANES):
            g = gu[:, c * LANES:(c + 1) * LANES]
            u = pltpu.roll(g, LANES - 1, axis=1)
            gate = jnp.minimum(g, SWIGLU_LIMIT)
            up = jnp.clip(u, -SWIGLU_LIMIT, SWIGLU_LIMIT)
            acts.append(((up + 1.0) * gate * jax.nn.sigmoid(SWIGLU_ALPHA * gate)).astype(BF16))
        act = jnp.concatenate(acts, axis=1)
        h_ref[0:nrows, :] = jnp.dot(act, pe_ref[...], preferred_element_type=F32).astype(h_ref.dtype)
        if nrows < tm:
            h_ref[nrows:tm, :] = jnp.zeros((tm - nrows, h_ref.shape[1]), h_ref.dtype)

    def zero():
        h_ref[...] = jnp.zeros_like(h_ref)

    _row_cases(na_ref, tv_ref, i, tm, compute, zero)


def _down_kernel(te_ref, na_ref, tv_ref, nxt_ref, eidx_ref, h_ref, w_hbm, bd_ref, rw_ref, o_ref,
                 wd_s, o_scr, raw, wsem):
    i = pl.program_id(1)
    tm, chunks, _ = o_ref.shape
    _stream_expert_weights(w_hbm, raw, wsem, te_ref, na_ref, nxt_ref, eidx_ref,
                           functools.partial(_cast_rows, wd_s))

    def compute(nrows):
        out = jnp.dot(h_ref[0:nrows, :], wd_s[...], preferred_element_type=F32)
        out = (out + bd_ref[...]) * rw_ref[0:nrows, :]
        for c in range(chunks):
            o_scr[pl.ds(c, nrows, stride=chunks), :] = out[:, c * LANES:(c + 1) * LANES]
        if nrows < tm:
            o_scr[nrows * chunks:tm * chunks, :] = jnp.zeros(((tm - nrows) * chunks, LANES), F32)
        o_ref[...] = o_scr[...].reshape(tm, chunks, LANES)

    def zero():
        o_ref[...] = jnp.zeros_like(o_ref)

    _row_cases(na_ref, tv_ref, i, tm, compute, zero)


def _experts(xs, row_w, plan, w_gate_up, b_gate_up, w_down, bd, tn_ff, tn_d):
    r_max, d = xs.shape
    ff = w_down.shape[1]
    tm = EXPERT_TM
    n_tiles = r_max // tm
    tc = 2 * tn_ff
    p_even = jnp.asarray(np.arange(tc)[:, None] == 2 * np.arange(tn_ff)[None, :], dtype=BF16)
    n_plan = len(plan)
    hbm = pl.BlockSpec(memory_space=pl.ANY)
    weight_stream = lambda rows, cols: [pltpu.VMEM((2, rows, cols), F32), pltpu.SemaphoreType.DMA((2,))]

    def row_tile(j, i, te, meta, *_):
        return (_tile_clamp(i, meta), 0)

    def expert_cols(j, i, te, meta, *_):
        return (te[_tile_clamp(i, meta)], 0, j)

    gu_spec = pltpu.PrefetchScalarGridSpec(
        num_scalar_prefetch=n_plan, grid=(ff // tn_ff, n_tiles),
        in_specs=[
            pl.BlockSpec((tm, d), row_tile),
            hbm,
            pl.BlockSpec((None, 1, tc), expert_cols),
            pl.BlockSpec((tc, tn_ff), lambda j, i, *_: (0, 0)),
        ],
        out_specs=pl.BlockSpec((tm, tn_ff), lambda j, i, *_: (i, j)),
        scratch_shapes=[pltpu.VMEM((d, tc), BF16)] + weight_stream(d, tc))
    hidden = pl.pallas_call(
        _gateup_kernel, grid_spec=gu_spec, out_shape=jax.ShapeDtypeStruct((r_max, ff), BF16),
        compiler_params=_cparams(2), name="expert_gate_up",
    )(*plan, xs, w_gate_up, b_gate_up, p_even)
    dn_spec = pltpu.PrefetchScalarGridSpec(
        num_scalar_prefetch=n_plan, grid=(d // tn_d, n_tiles),
        in_specs=[
            pl.BlockSpec((tm, ff), row_tile),
            hbm,
            pl.BlockSpec((None, 1, tn_d), expert_cols),
            pl.BlockSpec((tm, 1), row_tile),
        ],
        out_specs=pl.BlockSpec((tm, tn_d // LANES, LANES), lambda j, i, *_: (i, j, 0)),
        scratch_shapes=[pltpu.VMEM((ff, tn_d), BF16), pltpu.VMEM((tm * (tn_d // LANES), LANES), F32)]
        + weight_stream(ff, tn_d))
    return pl.pallas_call(
        _down_kernel, grid_spec=dn_spec, out_shape=jax.ShapeDtypeStruct((r_max, d // LANES, LANES), F32),
        compiler_params=_cparams(2), name="expert_down",
    )(*plan, hidden, w_down, bd, row_w)


def _combine_kernel(dest_hbm, rows_hbm, x_ref, o_ref, dest_smem, buf, sum_scr, isem, rsem):
    tm = COMBINE_TM
    chunks = rows_hbm.shape[1]
    seg = tm * chunks

    def start_rows(step, idx_slot, row_slot):
        def body(r, c):
            for k in range(TOP_K):
                src = dest_smem[idx_slot, r * TOP_K + k]
                off = pl.multiple_of((k * tm + r) * chunks, chunks)
                pltpu.make_async_copy(rows_hbm.at[src], buf.at[row_slot, pl.ds(off, chunks)],
                                      rsem.at[row_slot]).start()
            return c
        lax.fori_loop(0, tm, body, 0, unroll=2)

    def wait_rows(row_slot):
        pltpu.make_async_copy(buf.at[row_slot], buf.at[row_slot], rsem.at[row_slot]).wait()

    row_slot = _row_prefetch_ring(dest_hbm, dest_smem, isem, start_rows, wait_rows)

    acc = buf[row_slot, pl.ds(0, seg), :]
    for k in range(1, TOP_K):
        acc = acc + buf[row_slot, pl.ds(k * seg, seg), :]
    sum_scr[...] = acc
    for c in range(chunks):
        cols = slice(c * LANES, (c + 1) * LANES)
        o_ref[:, cols] = x_ref[:, cols] + sum_scr[pl.ds(c, tm, stride=chunks), :]


def _combine(rows, dest, x2d):
    t, d = x2d.shape
    chunks = rows.shape[1]
    tm = COMBINE_TM
    n_steps = t // tm
    dest2d = dest.reshape(n_steps, tm * TOP_K)
    blk = pl.BlockSpec((tm, d), lambda i: (i, 0))
    return pl.pallas_call(
        _combine_kernel, grid=(n_steps,),
        in_specs=[pl.BlockSpec(memory_space=pl.ANY), pl.BlockSpec(memory_space=pl.ANY), blk],
        out_specs=blk,
        out_shape=jax.ShapeDtypeStruct((t, d), F32),
        scratch_shapes=[pltpu.SMEM((INDEX_SLOTS, tm * TOP_K), jnp.int32),
                        pltpu.VMEM((ROW_SLOTS, TOP_K * tm * chunks, LANES), F32),
                        pltpu.VMEM((tm * chunks, LANES), F32),
                        pltpu.SemaphoreType.DMA((INDEX_SLOTS,)), pltpu.SemaphoreType.DMA((ROW_SLOTS,))],
        compiler_params=_cparams(1), name="combine",
    )(dest2d, rows, x2d)


def _moe(x_mid, g2, w_router, b_router, w_gate_up, b_gate_up, w_down, b_down, tn_ff, tn_d):
    t, d = x_mid.shape
    ne = w_router.shape[1]
    h2, idx_t, gate_t, rank_t, cnt = _router(x_mid, g2, w_router, b_router)

    tm = EXPERT_TM
    r_max = t * TOP_K + ne * tm
    n_tiles = r_max // tm
    counts = cnt[0, :ne].astype(jnp.int32)
    padded = (counts + tm - 1) // tm * tm
    pend = jnp.cumsum(padded)
    pstart = pend - padded
    idx = idx_t[:, :TOP_K]
    dest = (pstart[idx] + rank_t[:, :TOP_K]).astype(jnp.int32)
    n_rows_used = pend[-1:].astype(jnp.int32)
    n_active = n_rows_used // tm
    tok = jnp.repeat(jnp.arange(t, dtype=jnp.int32), TOP_K)
    gate_bits = lax.bitcast_convert_type(gate_t[:, :TOP_K].reshape(-1), jnp.int32)
    table = jnp.zeros((r_max, 2), jnp.int32).at[dest.reshape(-1)].set(
        jnp.stack([tok, gate_bits], axis=1), unique_indices=True)
    row_tok = table[:, 0]
    row_w = lax.bitcast_convert_type(table[:, 1], F32)[:, None]
    tile_start = jnp.arange(n_tiles, dtype=jnp.int32) * tm
    tile_e = jnp.minimum(jnp.sum((pend[None, :] <= tile_start[:, None]).astype(jnp.int32), axis=1), ne - 1)
    tile_valid = jnp.clip((pstart + counts)[tile_e] - tile_start, 0, tm).astype(jnp.int32)
    present = counts > 0
    eids = jnp.arange(ne, dtype=jnp.int32)
    later = present[None, :] & (eids[None, :] > eids[:, None])
    next_e = jnp.min(jnp.where(later, eids[None, :], ne), axis=1)
    next_e = jnp.where(next_e == ne, -1, next_e).astype(jnp.int32)
    rank_e = (jnp.cumsum(present.astype(jnp.int32)) - 1).astype(jnp.int32)
    meta = jnp.stack([n_active[0], jnp.sum(present.astype(jnp.int32)),
                      jnp.argmax(present).astype(jnp.int32)]).astype(jnp.int32)
    plan = (tile_e, meta, tile_valid, next_e[tile_e], rank_e[tile_e])

    xs = _gather_rows(h2, row_tok)

    rows = _experts(xs, row_w, plan, w_gate_up, b_gate_up[:, None, :], w_down, b_down[:, None, :], tn_ff, tn_d)
    return _combine(rows, dest, x_mid)


def _alibi_slopes(n):
    return (2.0 ** (-ALIBI_MAX_BIAS * (np.arange(n, dtype=np.float32) + 1.0) / n)).astype(np.float32)


def _mixer(x2d, batch, seq, norm1_g, w_in, q_norm_swa, k_norm_swa, q_norm_dil, k_norm_dil, sinks, w_out,
           n_swa, n_kv, n_dil, tm_proj, tn_out):
    t, d = x2d.shape
    swa_q, swa_kv, dil_w = n_swa * HEAD_DIM, n_kv * HEAD_DIM, n_dil * HEAD_DIM
    assert n_swa // n_kv == GQA_GROUP and seq % (16 * ATTN_BLOCK) == 0
    s1, s2, s3 = swa_q, swa_q + swa_kv, swa_q + 2 * swa_kv

    def dup(wc):
        wc = wc.reshape(d, n_kv, HEAD_DIM)
        return jnp.concatenate([wc, wc], axis=-1).reshape(d, 2 * swa_kv)

    w = jnp.concatenate([w_in[:, :s1], dup(w_in[:, s1:s2]), dup(w_in[:, s2:s3]), w_in[:, s3:]],
                        axis=1).astype(BF16)
    scale = HEAD_DIM ** -0.5
    ones = lambda n: jnp.ones((n,), F32)
    gains = jnp.concatenate([
        jnp.tile(q_norm_swa * scale, n_swa), jnp.tile(k_norm_swa, 2 * n_kv), ones(2 * swa_kv),
        jnp.tile(q_norm_dil * scale, n_dil), jnp.tile(k_norm_dil, n_dil), ones(dil_w)])[None, :]
    c = w.shape[1]
    bounds = np.cumsum([0, swa_q, 2 * swa_kv, 2 * swa_kv, dil_w, dil_w, dil_w])
    assert all(b % PROJ_TN == 0 for b in bounds)
    seg_norm = [1, 1, 0, 1, 1, 0]
    flags = np.zeros((c // PROJ_TN,), np.int32)
    for sidx in range(6):
        flags[bounds[sidx] // PROJ_TN:bounds[sidx + 1] // PROJ_TN] = seg_norm[sidx]
    qa0, ka0, va0, qb0, kb0, vb0 = (int(b) for b in bounds[:6])
    proj, *residue_major = _inproj(x2d, norm1_g[None, :], w, gains, jnp.asarray(flags), tm_proj, dil_col0=qb0)
    by_dilation = dict(zip(RESIDUE_DILATIONS, residue_major))

    sink_row = jnp.repeat(sinks.astype(F32), HEAD_DIM)[None, :]
    o_a = _band_attention(proj[None], jnp.asarray(_alibi_slopes(n_swa)), batch=batch, seq=seq, dilation=1,
                          q_col0=qa0, k_col0=ka0, v_col0=va0, n_heads=n_swa, kv_shared=True,
                          max_diff=SWA_WINDOW - 1, sinks=sink_row)[0]
    slopes_dil = jnp.asarray(_alibi_slopes(n_dil))
    outs, lses = [], []
    for window, dil in DILATED_BRANCHES:
        src, col0 = (proj[None], qb0) if dil == 1 else (by_dilation[dil], 0)
        o_i, lse_i = _band_attention(src, slopes_dil, batch=batch, seq=seq, dilation=dil,
                                     q_col0=col0, k_col0=col0 + kb0 - qb0, v_col0=col0 + vb0 - qb0,
                                     n_heads=n_dil, kv_shared=False, max_diff=window // dil)
        outs.append(o_i)
        lses.append(lse_i)
    o_b = _merge(outs, lses, [dil for _, dil in DILATED_BRANCHES])
    return _outproj(o_a, o_b, w_out.astype(BF16), x2d, tm_proj, tn_out)


def kernel(x, norm1_g, w_in, q_norm_swa, k_norm_swa, q_norm_dil, k_norm_dil, sinks, w_out, norm2_g,
           w_router, b_router, w_gate_up, b_gate_up, w_down, b_down):
    b, s, d = x.shape
    depth = norm1_g.shape[0]
    n_heads = d // HEAD_DIM
    n_swa = n_heads // 2
    n_kv = n_swa // GQA_GROUP
    n_dil = n_heads - n_swa
    x2d = x.reshape(b * s, d)
    for l in range(depth):
        x_mid = _mixer(x2d, b, s, norm1_g[l], w_in[l], q_norm_swa[l], k_norm_swa[l], q_norm_dil[l],
                       k_norm_dil[l], sinks[l], w_out[l], n_swa, n_kv, n_dil, tm_proj=512, tn_out=1024)
        x2d = _moe(x_mid, norm2_g[l][None, :], w_router[l], b_router[l], w_gate_up[l], b_gate_up[l],
                   w_down[l], b_down[l], tn_ff=512, tn_d=1024)
    return x2d.reshape(b, s, d)
```

```python
import functools
import math

import jax
import jax.numpy as jnp
import numpy as np
from jax import lax
from jax.experimental import pallas as pl
from jax.experimental.pallas import tpu as pltpu

F32 = jnp.float32
BF16 = jnp.bfloat16

HEAD_DIM = 64
LANES = 128
SUBLANES = 8
ATTN_BLOCK = 128
GQA_GROUP = 8
SWA_WINDOW = 128
DILATED_BRANCHES = ((128, 1), (512, 4), (2048, 16))
N_EXPERTS = 32
TOP_K = 4
SWIGLU_LIMIT = 7.0
SWIGLU_ALPHA = 1.702
ALIBI_MAX_BIAS = 8.0
EPS = 1e-6
MASK_DIST = 1e30
LOG2E = 1.4426950408889634
LN2 = 0.6931471805599453
NEG_BIG = -1e30

VMEM_LIMIT = 58 * 1024 * 1024

PROJ_TN = 512
ATTN_W_CHOICES = (2048, 1024, 512)
PAIRS_PER_KV = GQA_GROUP // 2
EXPERT_TM = 512
GATHER_ROWS = 256
COMBINE_TM = 128


def _cparams(n_axes):
    return pltpu.CompilerParams(dimension_semantics=("arbitrary",) * n_axes,
                                vmem_limit_bytes=VMEM_LIMIT)


RESIDUE_DILATIONS = tuple(d for _, d in DILATED_BRANCHES if d > 1)


def _inproj_kernel(flag_ref, x_ref, g_ref, w_ref, gain_ref, ones_ref, o_ref, *rest, dil_tile0):
    res_refs, (h_scr, y_scr) = rest[:len(RESIDUE_DILATIONS)], rest[len(RESIDUE_DILATIONS):]
    j = pl.program_id(1)
    tm, tn = o_ref.shape

    @pl.when(j == 0)
    def _():
        x = x_ref[...]
        ms = jnp.mean(x * x, axis=-1, keepdims=True)
        h_scr[...] = (x * lax.rsqrt(ms + EPS) * g_ref[...]).astype(BF16)

    acc = jnp.dot(h_scr[...], w_ref[...], preferred_element_type=F32)

    def emit(y):
        o_ref[...] = y.astype(BF16)
        for ch in range(tn // LANES):
            y_scr[ch] = y[:, ch * LANES:(ch + 1) * LANES]

    @pl.when(flag_ref[j] == 1)
    def _():
        ss = jnp.dot((acc * acc).astype(BF16), ones_ref[...], preferred_element_type=F32)
        emit(acc * lax.rsqrt(ss * (1.0 / HEAD_DIM) + EPS) * gain_ref[...])

    @pl.when(flag_ref[j] == 0)
    def _():
        emit(acc)

    @pl.when(j >= dil_tile0)
    def _():
        for d, ref in zip(RESIDUE_DILATIONS, res_refs):
            for r in range(d):
                for ch in range(tn // LANES):
                    rows = y_scr[ch, pl.ds(r, tm // d, stride=d), :]
                    ref[r, :, ch * LANES:(ch + 1) * LANES] = rows.astype(BF16)


def _inproj(x2d, g1, w, gains, flags, tm, dil_col0):
    t, d = x2d.shape
    c = w.shape[1]
    tn = PROJ_TN
    dil_tile0 = dil_col0 // tn
    cb = c - dil_col0
    head_id = np.arange(tn) // HEAD_DIM
    ones_bd = jnp.asarray(head_id[:, None] == head_id[None, :], dtype=BF16)
    res_specs = [pl.BlockSpec((dd, tm // dd, tn), lambda i, j, f: (0, i, jnp.maximum(j - dil_tile0, 0)))
                 for dd in RESIDUE_DILATIONS]
    res_shapes = [jax.ShapeDtypeStruct((dd, t // dd, cb), BF16) for dd in RESIDUE_DILATIONS]
    grid_spec = pltpu.PrefetchScalarGridSpec(
        num_scalar_prefetch=1,
        grid=(t // tm, c // tn),
        in_specs=[
            pl.BlockSpec((tm, d), lambda i, j, f: (i, 0)),
            pl.BlockSpec((1, d), lambda i, j, f: (0, 0)),
            pl.BlockSpec((d, tn), lambda i, j, f: (0, j)),
            pl.BlockSpec((1, tn), lambda i, j, f: (0, j)),
            pl.BlockSpec((tn, tn), lambda i, j, f: (0, 0)),
        ],
        out_specs=[pl.BlockSpec((tm, tn), lambda i, j, f: (i, j))] + res_specs,
        scratch_shapes=[pltpu.VMEM((tm, d), BF16), pltpu.VMEM((tn // LANES, tm, LANES), F32)],
    )
    return pl.pallas_call(
        functools.partial(_inproj_kernel, dil_tile0=dil_tile0), grid_spec=grid_spec,
        out_shape=[jax.ShapeDtypeStruct((t, c), BF16)] + res_shapes,
        compiler_params=_cparams(2), name="inproj",
    )(flags, x2d, g1, w, gains, ones_bd)


def _attn_kernel(slope_ref, q_ref, kp_ref, kc_ref, vp_ref, vc_ref, *rest,
                 n_pairs, kv_shared, max_diff, dist_scale, with_sink):
    if with_sink:
        sink_ref, o_ref = rest
        lse_ref = None
    else:
        o_ref, lse_ref = rest
    blk = pl.program_id(2)
    cg = pl.program_id(3)

    qi = lax.broadcasted_iota(jnp.int32, (ATTN_BLOCK, 2 * ATTN_BLOCK), 0)
    kj = lax.broadcasted_iota(jnp.int32, (ATTN_BLOCK, 2 * ATTN_BLOCK), 1)
    dist = qi + ATTN_BLOCK - kj
    valid = (dist >= 0) & (dist <= max_diff) & ((blk > 0) | (kj >= ATTN_BLOCK))
    dist_m = jnp.where(valid, (dist * dist_scale).astype(F32), MASK_DIST)

    lane = lax.broadcasted_iota(jnp.int32, (ATTN_BLOCK, LANES), 1)
    low = lane < HEAD_DIM

    for p in range(n_pairs):
        cols = slice(p * LANES, (p + 1) * LANES)
        kv = p // PAIRS_PER_KV
        kcols = slice(kv * LANES, (kv + 1) * LANES) if kv_shared else cols
        q2 = q_ref[:, cols]
        kk = jnp.concatenate([kp_ref[:, kcols], kc_ref[:, kcols]], axis=0)
        vv = jnp.concatenate([vp_ref[:, kcols], vc_ref[:, kcols]], axis=0)
        outs, lses = [], []
        for hh in range(2):
            slope = slope_ref[cg * (2 * n_pairs) + 2 * p + hh]
            qm = jnp.where(low if hh == 0 else ~low, q2, jnp.zeros_like(q2))
            s = lax.dot_general(qm, kk, (((1,), (1,)), ((), ())),
                                preferred_element_type=F32)
            s = s - slope * dist_m
            m = jnp.max(s, axis=-1, keepdims=True)
            e = jnp.exp2(s - m)
            l = jnp.sum(e, axis=-1, keepdims=True)
            o = jnp.dot(e.astype(BF16), vv, preferred_element_type=F32)
            outs.append(o / l)
            lses.append(jnp.broadcast_to((m + jnp.log2(l)) * LN2, (ATTN_BLOCK, LANES)))
        o2 = jnp.where(low, outs[0], outs[1])
        lse2 = jnp.where(low, lses[0], lses[1])
        if with_sink:
            o2 = o2 * jax.nn.sigmoid(lse2 - sink_ref[:, cols])
        else:
            lse_ref[:, cols] = lse2
        o_ref[:, cols] = o2.astype(o_ref.dtype)


def _band_attention(proj, slopes, *, batch, seq, dilation, q_col0, k_col0, v_col0, n_heads,
                    kv_shared, max_diff, sinks=None):
    d = dilation
    assert proj.shape[0] == d
    lsub = seq // d
    nblk = lsub // ATTN_BLOCK
    width = n_heads * HEAD_DIM

    def kv_width(w):
        return LANES * (w // LANES // PAIRS_PER_KV) if kv_shared else w

    w = next(c for c in ATTN_W_CHOICES
             if width % c == 0 and q_col0 % c == 0 and k_col0 % kv_width(c) == 0 and v_col0 % kv_width(c) == 0)
    n_pairs = w // LANES
    ncg = width // w
    kw = kv_width(w)
    pv = proj

    def qmap(b, r, i, g, s):
        return (r, b * nblk + i, q_col0 // w + g)

    def kvmap(col0, prev):
        def f(b, r, i, g, s):
            blk = jnp.maximum(i - 1, 0) if prev else i
            return (r, b * nblk + blk, col0 // kw + g)
        return f

    def omap(b, r, i, g, s):
        return (r, b * nblk + i, g)

    in_specs = [
        pl.BlockSpec((None, ATTN_BLOCK, w), qmap),
        pl.BlockSpec((None, ATTN_BLOCK, kw), kvmap(k_col0, True)),
        pl.BlockSpec((None, ATTN_BLOCK, kw), kvmap(k_col0, False)),
        pl.BlockSpec((None, ATTN_BLOCK, kw), kvmap(v_col0, True)),
        pl.BlockSpec((None, ATTN_BLOCK, kw), kvmap(v_col0, False)),
    ]
    args = [pv, pv, pv, pv, pv]
    with_sink = sinks is not None
    o_shape = jax.ShapeDtypeStruct((d, batch * lsub, width), BF16)
    o_spec = pl.BlockSpec((None, ATTN_BLOCK, w), omap)
    if with_sink:
        in_specs.append(pl.BlockSpec((1, w), lambda b, r, i, g, s: (0, g)))
        args.append(sinks)
        out_shape, out_specs = o_shape, o_spec
    else:
        out_shape = (o_shape, jax.ShapeDtypeStruct((d, batch * lsub, width), F32))
        out_specs = (o_spec, pl.BlockSpec((None, ATTN_BLOCK, w), omap))
    kern = functools.partial(_attn_kernel, n_pairs=n_pairs, kv_shared=kv_shared,
                             max_diff=max_diff, dist_scale=d, with_sink=with_sink)
    grid_spec = pltpu.PrefetchScalarGridSpec(
        num_scalar_prefetch=1, grid=(batch, d, nblk, ncg),
        in_specs=in_specs, out_specs=out_specs)
    res = pl.pallas_call(kern, grid_spec=grid_spec, out_shape=out_shape,
                         compiler_params=_cparams(4), name=f"band_attn_d{d}" + ("_sink" if with_sink else ""),
                         )(slopes, *args)
    return res


def _merge_kernel(*refs, dilations):
    n = len(dilations)
    o_refs, l_refs, out, scr = refs[:n], refs[n:2 * n], refs[2 * n], refs[2 * n + 1]
    tm, w = out.shape
    nch = w // LANES
    for a, d in enumerate(dilations):
        if d == 1:
            continue
        for b, ref in enumerate((o_refs[a], l_refs[a])):
            for r in range(d):
                for ch in range(nch):
                    scr[2 * a + b, ch, pl.ds(r, tm // d, stride=d), :] = (
                        ref[r, :, ch * LANES:(ch + 1) * LANES].astype(F32))
    for ch in range(nch):
        cols = slice(ch * LANES, (ch + 1) * LANES)
        os_, ls_ = [], []
        for a, d in enumerate(dilations):
            if d == 1:
                os_.append(o_refs[a][0, :, cols].astype(F32))
                ls_.append(l_refs[a][0, :, cols])
            else:
                os_.append(scr[2 * a, ch])
                ls_.append(scr[2 * a + 1, ch])
        m = functools.reduce(jnp.maximum, ls_)
        es = [jnp.exp(l - m) for l in ls_]
        num = functools.reduce(lambda p, q: p + q, [e * o for e, o in zip(es, os_)])
        den = functools.reduce(lambda p, q: p + q, es)
        out[:, cols] = (num / den).astype(out.dtype)


def _merge(os_, ls_, dilations, tm=256):
    w = os_[0].shape[2]
    t = os_[0].shape[0] * os_[0].shape[1]
    specs = [pl.BlockSpec((d, tm // d, w), lambda i: (0, i, 0)) for d in dilations]
    return pl.pallas_call(
        functools.partial(_merge_kernel, dilations=tuple(dilations)), grid=(t // tm,),
        in_specs=specs + specs, out_specs=pl.BlockSpec((tm, w), lambda i: (i, 0)),
        out_shape=jax.ShapeDtypeStruct((t, w), BF16),
        scratch_shapes=[pltpu.VMEM((2 * len(dilations), w // LANES, tm, LANES), F32)],
        compiler_params=_cparams(1), name="branch_merge",
    )(*os_, *ls_)


def _outproj_kernel(oa_ref, ob_ref, wt_ref, wb_ref, x_ref, o_ref):
    acc = jnp.dot(oa_ref[...], wt_ref[...], preferred_element_type=F32)
    acc = acc + jnp.dot(ob_ref[...], wb_ref[...], preferred_element_type=F32)
    o_ref[...] = x_ref[...] + acc


def _outproj(o_a, o_b, w_out, x2d, tm, tn):
    t, d = x2d.shape
    ha, hb = o_a.shape[1], o_b.shape[1]
    assert ha == hb
    return pl.pallas_call(
        _outproj_kernel, grid=(t // tm, d // tn),
        in_specs=[
            pl.BlockSpec((tm, ha), lambda i, j: (i, 0)),
            pl.BlockSpec((tm, hb), lambda i, j: (i, 0)),
            pl.BlockSpec((ha, tn), lambda i, j: (0, j)),
            pl.BlockSpec((hb, tn), lambda i, j: (1, j)),
            pl.BlockSpec((tm, tn), lambda i, j: (i, j)),
        ],
        out_specs=pl.BlockSpec((tm, tn), lambda i, j: (i, j)),
        out_shape=jax.ShapeDtypeStruct((t, d), F32), compiler_params=_cparams(2), name="outproj",
    )(o_a, o_b, w_out, w_out, x2d)


def _router_kernel(x_ref, g_ref, whi_ref, wlo_ref, b_ref, tri_ref,
                   h_ref, idx_ref, gate_ref, rank_ref, cnt_ref, carry, h_scr):
    i = pl.program_id(0)

    @pl.when(i == 0)
    def _():
        carry[...] = jnp.zeros_like(carry)

    x = x_ref[...]
    ms = jnp.mean(x * x, axis=-1, keepdims=True)
    h = x * lax.rsqrt(ms + EPS) * g_ref[...]
    h_hi = h.astype(BF16)
    h_lo = (h - h_hi.astype(F32)).astype(BF16)
    u = pltpu.bitcast(h, jnp.uint32)
    r = (u + jnp.uint32(0x7FFF) + ((u >> 16) & jnp.uint32(1))) >> 16
    tok, words, _ = h_ref.shape
    for s in range(words):
        lo = r[:, 2 * s * LANES:(2 * s + 1) * LANES]
        hi = r[:, (2 * s + 1) * LANES:(2 * s + 2) * LANES]
        h_scr[pl.ds(s, tok, stride=words), :] = lo | (hi << 16)
    h_ref[...] = h_scr[...].reshape(tok, words, LANES)
    logits = (jnp.dot(h_hi, whi_ref[...], preferred_element_type=F32)
              + jnp.dot(h_lo, whi_ref[...], preferred_element_type=F32)
              + jnp.dot(h_hi, wlo_ref[...], preferred_element_type=F32)) + b_ref[...]

    tm = x.shape[0]
    lane = lax.broadcasted_iota(jnp.int32, (tm, LANES), 1).astype(F32)
    work = logits
    multihot = jnp.zeros((tm, LANES), F32)
    vals, idxs = [], []
    for _ in range(TOP_K):
        m = jnp.max(work, axis=-1, keepdims=True)
        ik = jnp.min(jnp.where(work == m, lane, float(LANES)), axis=-1, keepdims=True)
        sel = lane == ik
        work = jnp.where(sel, -jnp.inf, work)
        multihot = jnp.where(sel, 1.0, multihot)
        vals.append(m)
        idxs.append(ik)
    es = [jnp.exp(v - vals[0]) for v in vals]
    denom = es[0] + es[1] + es[2] + es[3]

    cum = jnp.dot(tri_ref[...], multihot.astype(BF16), preferred_element_type=F32) + carry[0:1, :]
    idx_t = jnp.zeros((tm, LANES), F32)
    gate_t = jnp.zeros((tm, LANES), F32)
    rank_t = jnp.zeros((tm, LANES), F32)
    for k in range(TOP_K):
        rk = jnp.sum(jnp.where(lane == idxs[k], cum, 0.0), axis=-1, keepdims=True)
        here = lane == float(k)
        idx_t = jnp.where(here, idxs[k], idx_t)
        gate_t = jnp.where(here, es[k] / denom, gate_t)
        rank_t = jnp.where(here, rk, rank_t)
    idx_ref[...] = idx_t.astype(jnp.int32)
    gate_ref[...] = gate_t
    rank_ref[...] = rank_t.astype(jnp.int32)
    new_carry = carry[0:1, :] + jnp.sum(multihot, axis=0, keepdims=True)
    carry[...] = jnp.broadcast_to(new_carry, carry.shape)
    cnt_ref[...] = jnp.broadcast_to(new_carry, cnt_ref.shape)


def _router(x2d, g2, w_router, b_router, tm=256):
    t, d = x2d.shape
    ne = w_router.shape[1]
    w_pad = jnp.zeros((d, LANES), F32).at[:, :ne].set(w_router)
    w_hi = w_pad.astype(BF16)
    w_lo = (w_pad - w_hi.astype(F32)).astype(BF16)
    b_pad = jnp.full((1, LANES), NEG_BIG, F32).at[0, :ne].set(b_router)
    tri = jnp.asarray(np.tril(np.ones((tm, tm), np.float32), -1), dtype=BF16)
    tile = lambda dt: jax.ShapeDtypeStruct((t, LANES), dt)
    row = pl.BlockSpec((tm, LANES), lambda i: (i, 0))
    const = lambda shape: pl.BlockSpec(shape, lambda i: (0, 0))
    return pl.pallas_call(
        _router_kernel, grid=(t // tm,),
        in_specs=[pl.BlockSpec((tm, d), lambda i: (i, 0)), const((1, d)), const((d, LANES)),
                  const((d, LANES)), const((1, LANES)), const((tm, tm))],
        out_specs=(pl.BlockSpec((tm, d // (2 * LANES), LANES), lambda i: (i, 0, 0)), row, row, row,
                   const((8, LANES))),
        out_shape=(jax.ShapeDtypeStruct((t, d // (2 * LANES), LANES), jnp.uint32),
                   tile(jnp.int32), tile(F32), tile(jnp.int32),
                   jax.ShapeDtypeStruct((8, LANES), F32)),
        scratch_shapes=[pltpu.VMEM((8, LANES), F32), pltpu.VMEM((tm * (d // (2 * LANES)), LANES), jnp.uint32)],
        compiler_params=_cparams(1), name="router",
    )(x2d, g2, w_hi, w_lo, b_pad, tri)


INDEX_SLOTS = 3
ROW_SLOTS = 2


def _row_prefetch_ring(idx_hbm, idx_smem, isem, start_rows, wait_rows):
    i = pl.program_id(0)
    n = pl.num_programs(0)

    def idx_copy(step):
        slot = lax.rem(step, INDEX_SLOTS)
        return pltpu.make_async_copy(idx_hbm.at[step], idx_smem.at[slot], isem.at[slot])

    @pl.when(i == 0)
    def _():
        idx_copy(0).start()
        idx_copy(0).wait()
        start_rows(0, 0, 0)

        @pl.when(n > 1)
        def _():
            idx_copy(1).start()

    @pl.when(i + 1 < n)
    def _():
        idx_copy(i + 1).wait()
        start_rows(i + 1, lax.rem(i + 1, INDEX_SLOTS), lax.rem(i + 1, ROW_SLOTS))

    @pl.when(i + 2 < n)
    def _():
        idx_copy(i + 2).start()

    row_slot = lax.rem(i, ROW_SLOTS)
    wait_rows(row_slot)
    return row_slot


def _gather_kernel(tok_hbm, h_hbm, xs_ref, tok_smem, buf, isem, rsem):
    g = GATHER_ROWS
    words = h_hbm.shape[1]

    def start_rows(step, idx_slot, row_slot):
        def body(r2, c):
            for prio in range(2):
                r = 2 * r2 + prio
                t = tok_smem[idx_slot, r]
                dst = buf.at[row_slot, pl.ds(pl.multiple_of(r * words, words), words)]
                pltpu.make_async_copy(h_hbm.at[t], dst, rsem.at[row_slot]).start(priority=prio)
            return c
        lax.fori_loop(0, g // 2, body, 0)

    def wait_rows(row_slot):
        pltpu.make_async_copy(buf.at[row_slot], buf.at[row_slot], rsem.at[row_slot]).wait()

    row_slot = _row_prefetch_ring(tok_hbm, tok_smem, isem, start_rows, wait_rows)

    for s in range(words):
        x = buf[row_slot, pl.ds(s, g, stride=words), :]
        lo = pltpu.bitcast(x << 16, F32)
        hi = pltpu.bitcast(x & jnp.uint32(0xFFFF0000), F32)
        xs_ref[:, 2 * s * LANES:(2 * s + 1) * LANES] = lo.astype(BF16)
        xs_ref[:, (2 * s + 1) * LANES:(2 * s + 2) * LANES] = hi.astype(BF16)


def _gather_rows(h_packed, row_tok):
    t, words, _ = h_packed.shape
    r_max = row_tok.shape[0]
    g = GATHER_ROWS
    n_steps = r_max // g
    tok2d = row_tok.reshape(n_steps, g)
    return pl.pallas_call(
        _gather_kernel, grid=(n_steps,),
        in_specs=[pl.BlockSpec(memory_space=pl.ANY), pl.BlockSpec(memory_space=pl.ANY)],
        out_specs=pl.BlockSpec((g, 2 * words * LANES), lambda i: (i, 0)),
        scratch_shapes=[pltpu.SMEM((INDEX_SLOTS, g), jnp.int32),
                        pltpu.VMEM((ROW_SLOTS, g * words, LANES), jnp.uint32),
                        pltpu.SemaphoreType.DMA((INDEX_SLOTS,)), pltpu.SemaphoreType.DMA((ROW_SLOTS,))],
        out_shape=jax.ShapeDtypeStruct((r_max, 2 * words * LANES), BF16),
        compiler_params=_cparams(1), name="row_gather",
    )(tok2d, h_packed)


PREP_ROWS = 1024


def _tile_clamp(i, na):
    return jnp.minimum(i, na[0] - 1)


def _weights_changed(te_ref, na_ref, i):
    cur = _tile_clamp(i, na_ref)
    prev = jnp.maximum(cur - 1, 0)
    return (i < na_ref[0]) & ((i == 0) | (te_ref[cur] != te_ref[prev]))


def _stream_expert_weights(w_hbm, raw, wsem, te_ref, meta_ref, nxt_ref, eidx_ref, convert):
    j = pl.program_id(0)
    i = pl.program_id(1)
    nj = pl.num_programs(0)
    tc = raw.shape[2]
    cur = _tile_clamp(i, meta_ref)

    def copy(e, jj, slot):
        col = pl.multiple_of(jj * tc, tc)
        return pltpu.make_async_copy(w_hbm.at[e, :, pl.ds(col, tc)], raw.at[slot], wsem.at[slot])

    @pl.when(_weights_changed(te_ref, meta_ref, i))
    def _():
        e = te_ref[cur]
        b = j * meta_ref[1] + eidx_ref[cur]
        slot = lax.rem(b, 2)
        nxt = nxt_ref[cur]

        @pl.when(b == 0)
        def _():
            copy(e, j, 0).start()

        @pl.when(nxt >= 0)
        def _():
            copy(nxt, j, 1 - slot).start()

        @pl.when((nxt < 0) & (j + 1 < nj))
        def _():
            copy(meta_ref[2], j + 1, 1 - slot).start()

        copy(e, j, slot).wait()
        convert(raw.at[slot])


def _row_cases(na_ref, tv_ref, i, tm, compute, zero):
    active = i < na_ref[0]
    valid = tv_ref[_tile_clamp(i, na_ref)]

    @pl.when(active & (valid > tm // 2))
    def _():
        compute(tm)

    @pl.when(active & (valid <= tm // 2))
    def _():
        compute(tm // 2)

    @pl.when(jnp.logical_not(active))
    def _():
        zero()


def _cast_rows(dst, src):
    n = dst.shape[0]
    for r0 in range(0, n, PREP_ROWS):
        rows = slice(r0, min(r0 + PREP_ROWS, n))
        dst[rows, :] = src[rows, :].astype(dst.dtype)


def _gateup_kernel(te_ref, na_ref, tv_ref, nxt_ref, eidx_ref, xs_ref, w_hbm, b_ref, pe_ref, h_ref,
                   w_s, raw, wsem):
    i = pl.program_id(1)
    d, tc = w_s.shape
    tm = h_ref.shape[0]
    _stream_expert_weights(w_hbm, raw, wsem, te_ref, na_ref, nxt_ref, eidx_ref,
                           functools.partial(_cast_rows, w_s))

    def compute(nrows):
        gu = jnp.dot(xs_ref[0:nrows, :], w_s[...], preferred_element_type=F32) + b_ref[...]
        acts = []
        for c in range(tc // LANES):
            g = gu[:, c * LANES:(c + 1) * LANES]
            u = pltpu.roll(g, LANES - 1, axis=1)
            gate = jnp.minimum(g, SWIGLU_LIMIT)
            up = jnp.clip(u, -SWIGLU_LIMIT, SWIGLU_LIMIT)
            acts.append(((up + 1.0) * gate * jax.nn.sigmoid(SWIGLU_ALPHA * gate)).astype(BF16))
        act = jnp.concatenate(acts, axis=1)
        h_ref[0:nrows, :] = jnp.dot(act, pe_ref[...], preferred_element_type=F32).astype(h_ref.dtype)
        if nrows < tm:
            h_ref[nrows:tm, :] = jnp.zeros((tm - nrows, h_ref.shape[1]), h_ref.dtype)

    def zero():
        h_ref[...] = jnp.zeros_like(h_ref)

    _row_cases(na_ref, tv_ref, i, tm, compute, zero)


def _down_kernel(te_ref, na_ref, tv_ref, nxt_ref, eidx_ref, h_ref, w_hbm, bd_ref, rw_ref, o_ref,
                 wd_s, o_scr, raw, wsem):
    i = pl.program_id(1)
    tm, chunks, _ = o_ref.shape
    _stream_expert_weights(w_hbm, raw, wsem, te_ref, na_ref, nxt_ref, eidx_ref,
                           functools.partial(_cast_rows, wd_s))

    def compute(nrows):
        out = jnp.dot(h_ref[0:nrows, :], wd_s[...], preferred_element_type=F32)
        out = (out + bd_ref[...]) * rw_ref[0:nrows, :]
        for c in range(chunks):
            o_scr[pl.ds(c, nrows, stride=chunks), :] = out[:, c * LANES:(c + 1) * LANES]
        if nrows < tm:
            o_scr[nrows * chunks:tm * chunks, :] = jnp.zeros(((tm - nrows) * chunks, LANES), F32)
        o_ref[...] = o_scr[...].reshape(tm, chunks, LANES)

    def zero():
        o_ref[...] = jnp.zeros_like(o_ref)

    _row_cases(na_ref, tv_ref, i, tm, compute, zero)


def _experts(xs, row_w, plan, w_gate_up, b_gate_up, w_down, bd, tn_ff, tn_d):
    r_max, d = xs.shape
    ff = w_down.shape[1]
    tm = EXPERT_TM
    n_tiles = r_max // tm
    tc = 2 * tn_ff
    p_even = jnp.asarray(np.arange(tc)[:, None] == 2 * np.arange(tn_ff)[None, :], dtype=BF16)
    n_plan = len(plan)
    hbm = pl.BlockSpec(memory_space=pl.ANY)
    weight_stream = lambda rows, cols: [pltpu.VMEM((2, rows, cols), F32), pltpu.SemaphoreType.DMA((2,))]

    def row_tile(j, i, te, meta, *_):
        return (_tile_clamp(i, meta), 0)

    def expert_cols(j, i, te, meta, *_):
        return (te[_tile_clamp(i, meta)], 0, j)

    gu_spec = pltpu.PrefetchScalarGridSpec(
        num_scalar_prefetch=n_plan, grid=(ff // tn_ff, n_tiles),
        in_specs=[
            pl.BlockSpec((tm, d), row_tile),
            hbm,
            pl.BlockSpec((None, 1, tc), expert_cols),
            pl.BlockSpec((tc, tn_ff), lambda j, i, *_: (0, 0)),
        ],
        out_specs=pl.BlockSpec((tm, tn_ff), lambda j, i, *_: (i, j)),
        scratch_shapes=[pltpu.VMEM((d, tc), BF16)] + weight_stream(d, tc))
    hidden = pl.pallas_call(
        _gateup_kernel, grid_spec=gu_spec, out_shape=jax.ShapeDtypeStruct((r_max, ff), BF16),
        compiler_params=_cparams(2), name="expert_gate_up",
    )(*plan, xs, w_gate_up, b_gate_up, p_even)
    dn_spec = pltpu.PrefetchScalarGridSpec(
        num_scalar_prefetch=n_plan, grid=(d // tn_d, n_tiles),
        in_specs=[
            pl.BlockSpec((tm, ff), row_tile),
            hbm,
            pl.BlockSpec((None, 1, tn_d), expert_cols),
            pl.BlockSpec((tm, 1), row_tile),
        ],
        out_specs=pl.BlockSpec((tm, tn_d // LANES, LANES), lambda j, i, *_: (i, j, 0)),
        scratch_shapes=[pltpu.VMEM((ff, tn_d), BF16), pltpu.VMEM((tm * (tn_d // LANES), LANES), F32)]
        + weight_stream(ff, tn_d))
    return pl.pallas_call(
        _down_kernel, grid_spec=dn_spec, out_shape=jax.ShapeDtypeStruct((r_max, d // LANES, LANES), F32),
        compiler_params=_cparams(2), name="expert_down",
    )(*plan, hidden, w_down, bd, row_w)


def _combine_kernel(dest_hbm, rows_hbm, x_ref, o_ref, dest_smem, buf, sum_scr, isem, rsem):
    tm = COMBINE_TM
    chunks = rows_hbm.shape[1]
    pitch = sum_scr.shape[0] // tm
    seg = tm * pitch

    @pl.when(pl.program_id(0) == 0)
    def _():
        buf[...] = jnp.zeros_like(buf)

    def start_rows(step, idx_slot, row_slot):
        def body(r, c):
            for k in range(TOP_K):
                src = dest_smem[idx_slot, r * TOP_K + k]
                off = pl.multiple_of((k * tm + r) * pitch, math.gcd(pitch, SUBLANES))
                pltpu.make_async_copy(rows_hbm.at[src], buf.at[row_slot, pl.ds(off, chunks)],
                                      rsem.at[row_slot]).start()
            return c
        lax.fori_loop(0, tm, body, 0, unroll=2)

    def wait_rows(row_slot):
        landed = buf.at[row_slot, pl.ds(0, TOP_K * tm * chunks)]
        pltpu.make_async_copy(landed, landed, rsem.at[row_slot]).wait()

    row_slot = _row_prefetch_ring(dest_hbm, dest_smem, isem, start_rows, wait_rows)

    acc = buf[row_slot, pl.ds(0, seg), :]
    for k in range(1, TOP_K):
        acc = acc + buf[row_slot, pl.ds(k * seg, seg), :]
    sum_scr[...] = acc
    for c in range(chunks):
        cols = slice(c * LANES, (c + 1) * LANES)
        o_ref[:, cols] = x_ref[:, cols] + sum_scr[pl.ds(c, tm, stride=pitch), :]


def _combine(rows, dest, x2d):
    t, d = x2d.shape
    chunks = rows.shape[1]
    tm = COMBINE_TM
    n_steps = t // tm
    dest2d = dest.reshape(n_steps, tm * TOP_K)
    blk = pl.BlockSpec((tm, d), lambda i: (i, 0))
    pitch = chunks + SUBLANES if chunks % (2 * SUBLANES) == 0 else chunks
    return pl.pallas_call(
        _combine_kernel, grid=(n_steps,),
        in_specs=[pl.BlockSpec(memory_space=pl.ANY), pl.BlockSpec(memory_space=pl.ANY), blk],
        out_specs=blk,
        out_shape=jax.ShapeDtypeStruct((t, d), F32),
        scratch_shapes=[pltpu.SMEM((INDEX_SLOTS, tm * TOP_K), jnp.int32),
                        pltpu.VMEM((ROW_SLOTS, TOP_K * tm * pitch, LANES), F32),
                        pltpu.VMEM((tm * pitch, LANES), F32),
                        pltpu.SemaphoreType.DMA((INDEX_SLOTS,)), pltpu.SemaphoreType.DMA((ROW_SLOTS,))],
        compiler_params=_cparams(1), name="combine",
    )(dest2d, rows, x2d)


def _moe(x_mid, g2, w_router, b_router, w_gate_up, b_gate_up, w_down, b_down, tn_ff, tn_d):
    t, d = x_mid.shape
    ne = w_router.shape[1]
    h2, idx_t, gate_t, rank_t, cnt = _router(x_mid, g2, w_router, b_router)

    tm = EXPERT_TM
    r_max = t * TOP_K + ne * tm
    n_tiles = r_max // tm
    counts = cnt[0, :ne].astype(jnp.int32)
    padded = (counts + tm - 1) // tm * tm
    pend = jnp.cumsum(padded)
    pstart = pend - padded
    idx = idx_t[:, :TOP_K]
    dest = (pstart[idx] + rank_t[:, :TOP_K]).astype(jnp.int32)
    n_rows_used = pend[-1:].astype(jnp.int32)
    n_active = n_rows_used // tm
    tok = jnp.repeat(jnp.arange(t, dtype=jnp.int32), TOP_K)
    gate_bits = lax.bitcast_convert_type(gate_t[:, :TOP_K].reshape(-1), jnp.int32)
    table = jnp.zeros((r_max, 2), jnp.int32).at[dest.reshape(-1)].set(
        jnp.stack([tok, gate_bits], axis=1), unique_indices=True)
    row_tok = table[:, 0]
    row_w = lax.bitcast_convert_type(table[:, 1], F32)[:, None]
    tile_start = jnp.arange(n_tiles, dtype=jnp.int32) * tm
    tile_e = jnp.minimum(jnp.sum((pend[None, :] <= tile_start[:, None]).astype(jnp.int32), axis=1), ne - 1)
    tile_valid = jnp.clip((pstart + counts)[tile_e] - tile_start, 0, tm).astype(jnp.int32)
    present = counts > 0
    eids = jnp.arange(ne, dtype=jnp.int32)
    later = present[None, :] & (eids[None, :] > eids[:, None])
    next_e = jnp.min(jnp.where(later, eids[None, :], ne), axis=1)
    next_e = jnp.where(next_e == ne, -1, next_e).astype(jnp.int32)
    rank_e = (jnp.cumsum(present.astype(jnp.int32)) - 1).astype(jnp.int32)
    meta = jnp.stack([n_active[0], jnp.sum(present.astype(jnp.int32)),
                      jnp.argmax(present).astype(jnp.int32)]).astype(jnp.int32)
    plan = (tile_e, meta, tile_valid, next_e[tile_e], rank_e[tile_e])

    xs = _gather_rows(h2, row_tok)

    rows = _experts(xs, row_w, plan, w_gate_up, b_gate_up[:, None, :], w_down, b_down[:, None, :], tn_ff, tn_d)
    return _combine(rows, dest, x_mid)


def _alibi_slopes(n):
    return (LOG2E * 2.0 ** (-ALIBI_MAX_BIAS * (np.arange(n, dtype=np.float64) + 1.0) / n)).astype(np.float32)


def _mixer(x2d, batch, seq, norm1_g, w_in, q_norm_swa, k_norm_swa, q_norm_dil, k_norm_dil, sinks, w_out,
           n_swa, n_kv, n_dil, tm_proj, tn_out):
    t, d = x2d.shape
    swa_q, swa_kv, dil_w = n_swa * HEAD_DIM, n_kv * HEAD_DIM, n_dil * HEAD_DIM
    assert n_swa // n_kv == GQA_GROUP and seq % (16 * ATTN_BLOCK) == 0
    s1, s2, s3 = swa_q, swa_q + swa_kv, swa_q + 2 * swa_kv

    def dup(wc):
        wc = wc.reshape(d, n_kv, HEAD_DIM)
        return jnp.concatenate([wc, wc], axis=-1).reshape(d, 2 * swa_kv)

    w = jnp.concatenate([w_in[:, :s1], dup(w_in[:, s1:s2]), dup(w_in[:, s2:s3]), w_in[:, s3:]],
                        axis=1).astype(BF16)
    scale = HEAD_DIM ** -0.5 * LOG2E
    ones = lambda n: jnp.ones((n,), F32)
    gains = jnp.concatenate([
        jnp.tile(q_norm_swa * scale, n_swa), jnp.tile(k_norm_swa, 2 * n_kv), ones(2 * swa_kv),
        jnp.tile(q_norm_dil * scale, n_dil), jnp.tile(k_norm_dil, n_dil), ones(dil_w)])[None, :]
    c = w.shape[1]
    bounds = np.cumsum([0, swa_q, 2 * swa_kv, 2 * swa_kv, dil_w, dil_w, dil_w])
    assert all(b % PROJ_TN == 0 for b in bounds)
    seg_norm = [1, 1, 0, 1, 1, 0]
    flags = np.zeros((c // PROJ_TN,), np.int32)
    for sidx in range(6):
        flags[bounds[sidx] // PROJ_TN:bounds[sidx + 1] // PROJ_TN] = seg_norm[sidx]
    qa0, ka0, va0, qb0, kb0, vb0 = (int(b) for b in bounds[:6])
    proj, *residue_major = _inproj(x2d, norm1_g[None, :], w, gains, jnp.asarray(flags), tm_proj, dil_col0=qb0)
    by_dilation = dict(zip(RESIDUE_DILATIONS, residue_major))

    sink_row = jnp.repeat(sinks.astype(F32), HEAD_DIM)[None, :]
    o_a = _band_attention(proj[None], jnp.asarray(_alibi_slopes(n_swa)), batch=batch, seq=seq, dilation=1,
                          q_col0=qa0, k_col0=ka0, v_col0=va0, n_heads=n_swa, kv_shared=True,
                          max_diff=SWA_WINDOW - 1, sinks=sink_row)[0]
    slopes_dil = jnp.asarray(_alibi_slopes(n_dil))
    outs, lses = [], []
    for window, dil in DILATED_BRANCHES:
        src, col0 = (proj[None], qb0) if dil == 1 else (by_dilation[dil], 0)
        o_i, lse_i = _band_attention(src, slopes_dil, batch=batch, seq=seq, dilation=dil,
                                     q_col0=col0, k_col0=col0 + kb0 - qb0, v_col0=col0 + vb0 - qb0,
                                     n_heads=n_dil, kv_shared=False, max_diff=window // dil)
        outs.append(o_i)
        lses.append(lse_i)
    o_b = _merge(outs, lses, [dil for _, dil in DILATED_BRANCHES])
    return _outproj(o_a, o_b, w_out.astype(BF16), x2d, tm_proj, tn_out)


def kernel(x, norm1_g, w_in, q_norm_swa, k_norm_swa, q_norm_dil, k_norm_dil, sinks, w_out, norm2_g,
           w_router, b_router, w_gate_up, b_gate_up, w_down, b_down):
    b, s, d = x.shape
    depth = norm1_g.shape[0]
    n_heads = d // HEAD_DIM
    n_swa = n_heads // 2
    n_kv = n_swa // GQA_GROUP
    n_dil = n_heads - n_swa
    x2d = x.reshape(b * s, d)
    for l in range(depth):
        x_mid = _mixer(x2d, b, s, norm1_g[l], w_in[l], q_norm_swa[l], k_norm_swa[l], q_norm_dil[l],
                       k_norm_dil[l], sinks[l], w_out[l], n_swa, n_kv, n_dil, tm_proj=512, tn_out=1024)
        x2d = _moe(x_mid, norm2_g[l][None, :], w_router[l], b_router[l], w_gate_up[l], b_gate_up[l],
                   w_down[l], b_down[l], tn_ff=512, tn_d=1024)
    return x2d.reshape(b, s, d)
```

```python
import functools
import math

import jax
import jax.numpy as jnp
import numpy as np
from jax import lax
from jax.experimental import pallas as pl
from jax.experimental.pallas import tpu as pltpu

F32 = jnp.float32
BF16 = jnp.bfloat16

HEAD_DIM = 64
LANES = 128
SUBLANES = 8
ATTN_BLOCK = 128
GQA_GROUP = 8
SWA_WINDOW = 128
DILATED_BRANCHES = ((128, 1), (512, 4), (2048, 16))
N_EXPERTS = 32
TOP_K = 4
SWIGLU_LIMIT = 7.0
SWIGLU_ALPHA = 1.702
ALIBI_MAX_BIAS = 8.0
EPS = 1e-6
MASK_DIST = 1e30
LOG2E = 1.4426950408889634
LN2 = 0.6931471805599453
NEG_BIG = -1e30

VMEM_LIMIT = 58 * 1024 * 1024

PROJ_TN = 512
ATTN_W_CHOICES = (2048, 1024, 512)
PAIRS_PER_KV = GQA_GROUP // 2
EXPERT_TM = 512
GATHER_ROWS = 256
COMBINE_TM = 128


def _cparams(n_axes):
    return pltpu.CompilerParams(dimension_semantics=("arbitrary",) * n_axes,
                                vmem_limit_bytes=VMEM_LIMIT)


RESIDUE_DILATIONS = tuple(d for _, d in DILATED_BRANCHES if d > 1)


def _inproj_kernel(flag_ref, x_ref, g_ref, w_ref, gain_ref, ones_ref, o_ref, *rest, dil_tile0):
    res_refs, (h_scr, y_scr) = rest[:len(RESIDUE_DILATIONS)], rest[len(RESIDUE_DILATIONS):]
    j = pl.program_id(1)
    tm, tn = o_ref.shape

    @pl.when(j == 0)
    def _():
        x = x_ref[...]
        ms = jnp.mean(x * x, axis=-1, keepdims=True)
        h_scr[...] = (x * lax.rsqrt(ms + EPS) * g_ref[...]).astype(BF16)

    acc = jnp.dot(h_scr[...], w_ref[...], preferred_element_type=F32)

    def emit(y):
        o_ref[...] = y.astype(BF16)
        for ch in range(tn // LANES):
            y_scr[ch] = y[:, ch * LANES:(ch + 1) * LANES]

    @pl.when(flag_ref[j] == 1)
    def _():
        ss = jnp.dot((acc * acc).astype(BF16), ones_ref[...], preferred_element_type=F32)
        emit(acc * lax.rsqrt(ss * (1.0 / HEAD_DIM) + EPS) * gain_ref[...])

    @pl.when(flag_ref[j] == 0)
    def _():
        emit(acc)

    @pl.when(j >= dil_tile0)
    def _():
        for d, ref in zip(RESIDUE_DILATIONS, res_refs):
            for r in range(d):
                for ch in range(tn // LANES):
                    rows = y_scr[ch, pl.ds(r, tm // d, stride=d), :]
                    ref[r, :, ch * LANES:(ch + 1) * LANES] = rows.astype(BF16)


def _inproj(x2d, g1, w, gains, flags, tm, dil_col0):
    t, d = x2d.shape
    c = w.shape[1]
    tn = PROJ_TN
    dil_tile0 = dil_col0 // tn
    cb = c - dil_col0
    head_id = np.arange(tn) // HEAD_DIM
    ones_bd = jnp.asarray(head_id[:, None] == head_id[None, :], dtype=BF16)
    res_specs = [pl.BlockSpec((dd, tm // dd, tn), lambda i, j, f: (0, i, jnp.maximum(j - dil_tile0, 0)))
                 for dd in RESIDUE_DILATIONS]
    res_shapes = [jax.ShapeDtypeStruct((dd, t // dd, cb), BF16) for dd in RESIDUE_DILATIONS]
    grid_spec = pltpu.PrefetchScalarGridSpec(
        num_scalar_prefetch=1,
        grid=(t // tm, c // tn),
        in_specs=[
            pl.BlockSpec((tm, d), lambda i, j, f: (i, 0)),
            pl.BlockSpec((1, d), lambda i, j, f: (0, 0)),
            pl.BlockSpec((d, tn), lambda i, j, f: (0, j)),
            pl.BlockSpec((1, tn), lambda i, j, f: (0, j)),
            pl.BlockSpec((tn, tn), lambda i, j, f: (0, 0)),
        ],
        out_specs=[pl.BlockSpec((tm, tn), lambda i, j, f: (i, j))] + res_specs,
        scratch_shapes=[pltpu.VMEM((tm, d), BF16), pltpu.VMEM((tn // LANES, tm, LANES), F32)],
    )
    return pl.pallas_call(
        functools.partial(_inproj_kernel, dil_tile0=dil_tile0), grid_spec=grid_spec,
        out_shape=[jax.ShapeDtypeStruct((t, c), BF16)] + res_shapes,
        compiler_params=_cparams(2), name="inproj",
    )(flags, x2d, g1, w, gains, ones_bd)


def _attn_kernel(slope_ref, q_ref, kp_ref, kc_ref, vp_ref, vc_ref, *rest,
                 n_pairs, kv_shared, max_diff, dist_scale, with_sink):
    if with_sink:
        sink_ref, o_ref = rest
        lse_ref = None
    else:
        o_ref, lse_ref = rest
    blk = pl.program_id(2)
    cg = pl.program_id(3)

    qi = lax.broadcasted_iota(jnp.int32, (ATTN_BLOCK, 2 * ATTN_BLOCK), 0)
    kj = lax.broadcasted_iota(jnp.int32, (ATTN_BLOCK, 2 * ATTN_BLOCK), 1)
    dist = qi + ATTN_BLOCK - kj
    valid = (dist >= 0) & (dist <= max_diff) & ((blk > 0) | (kj >= ATTN_BLOCK))
    dist_m = jnp.where(valid, (dist * dist_scale).astype(F32), MASK_DIST)

    lane = lax.broadcasted_iota(jnp.int32, (ATTN_BLOCK, LANES), 1)
    low = lane < HEAD_DIM
    if not with_sink:
        @pl.when(cg == 0)
        def _():
            lse_ref[...] = jnp.zeros_like(lse_ref)
        lse_tile = lse_ref[...]

    for p in range(n_pairs):
        cols = slice(p * LANES, (p + 1) * LANES)
        kv = p // PAIRS_PER_KV
        kcols = slice(kv * LANES, (kv + 1) * LANES) if kv_shared else cols
        q2 = q_ref[:, cols]
        kk = jnp.concatenate([kp_ref[:, kcols], kc_ref[:, kcols]], axis=0)
        vv = jnp.concatenate([vp_ref[:, kcols], vc_ref[:, kcols]], axis=0)
        outs, lses = [], []
        for hh in range(2):
            slope = slope_ref[cg * (2 * n_pairs) + 2 * p + hh]
            qm = jnp.where(low if hh == 0 else ~low, q2, jnp.zeros_like(q2))
            s = lax.dot_general(qm, kk, (((1,), (1,)), ((), ())),
                                preferred_element_type=F32)
            s = s - slope * dist_m
            m = jnp.max(s, axis=-1, keepdims=True)
            e = jnp.exp2(s - m)
            l = jnp.sum(e, axis=-1, keepdims=True)
            o = jnp.dot(e.astype(BF16), vv, preferred_element_type=F32)
            outs.append(o / l)
            lse = (m + jnp.log2(l)) * LN2
            if with_sink:
                lses.append(jnp.broadcast_to(lse, (ATTN_BLOCK, LANES)))
            else:
                head = cg * (2 * n_pairs) + 2 * p + hh
                lse_tile = jnp.where(lane == head, lse, lse_tile)
        o2 = jnp.where(low, outs[0], outs[1])
        if with_sink:
            lse2 = jnp.where(low, lses[0], lses[1])
            o2 = o2 * jax.nn.sigmoid(lse2 - sink_ref[:, cols])
        o_ref[:, cols] = o2.astype(o_ref.dtype)
    if not with_sink:
        lse_ref[...] = lse_tile


def _band_attention(proj, slopes, *, batch, seq, dilation, q_col0, k_col0, v_col0, n_heads,
                    kv_shared, max_diff, sinks=None):
    d = dilation
    assert proj.shape[0] == d
    lsub = seq // d
    nblk = lsub // ATTN_BLOCK
    width = n_heads * HEAD_DIM

    def kv_width(w):
        return LANES * (w // LANES // PAIRS_PER_KV) if kv_shared else w

    w = next(c for c in ATTN_W_CHOICES
             if width % c == 0 and q_col0 % c == 0 and k_col0 % kv_width(c) == 0 and v_col0 % kv_width(c) == 0)
    n_pairs = w // LANES
    ncg = width // w
    kw = kv_width(w)
    pv = proj

    def qmap(b, r, i, g, s):
        return (r, b * nblk + i, q_col0 // w + g)

    def kvmap(col0, prev):
        def f(b, r, i, g, s):
            blk = jnp.maximum(i - 1, 0) if prev else i
            return (r, b * nblk + blk, col0 // kw + g)
        return f

    def omap(b, r, i, g, s):
        return (r, b * nblk + i, g)

    in_specs = [
        pl.BlockSpec((None, ATTN_BLOCK, w), qmap),
        pl.BlockSpec((None, ATTN_BLOCK, kw), kvmap(k_col0, True)),
        pl.BlockSpec((None, ATTN_BLOCK, kw), kvmap(k_col0, False)),
        pl.BlockSpec((None, ATTN_BLOCK, kw), kvmap(v_col0, True)),
        pl.BlockSpec((None, ATTN_BLOCK, kw), kvmap(v_col0, False)),
    ]
    args = [pv, pv, pv, pv, pv]
    with_sink = sinks is not None
    o_shape = jax.ShapeDtypeStruct((d, batch * lsub, width), BF16)
    o_spec = pl.BlockSpec((None, ATTN_BLOCK, w), omap)
    if with_sink:
        in_specs.append(pl.BlockSpec((1, w), lambda b, r, i, g, s: (0, g)))
        args.append(sinks)
        out_shape, out_specs = o_shape, o_spec
    else:
        assert n_heads <= LANES
        out_shape = (o_shape, jax.ShapeDtypeStruct((d, batch * lsub, LANES), F32))
        out_specs = (o_spec, pl.BlockSpec((None, ATTN_BLOCK, LANES),
                                          lambda b, r, i, g, s: (r, b * nblk + i, 0)))
    kern = functools.partial(_attn_kernel, n_pairs=n_pairs, kv_shared=kv_shared,
                             max_diff=max_diff, dist_scale=d, with_sink=with_sink)
    grid_spec = pltpu.PrefetchScalarGridSpec(
        num_scalar_prefetch=1, grid=(batch, d, nblk, ncg),
        in_specs=in_specs, out_specs=out_specs)
    res = pl.pallas_call(kern, grid_spec=grid_spec, out_shape=out_shape,
                         compiler_params=_cparams(4), name=f"band_attn_d{d}" + ("_sink" if with_sink else ""),
                         )(slopes, *args)
    return res


def _merge_kernel(*refs, dilations):
    n = len(dilations)
    o_refs, l_refs = refs[:n], refs[n:2 * n]
    spread_ref, out, scr, lscr = refs[2 * n:]
    tm, w = out.shape
    nch = w // LANES
    for a, d in enumerate(dilations):
        if d == 1:
            continue
        for r in range(d):
            lscr[a, pl.ds(r, tm // d, stride=d), :] = l_refs[a][r]
            for ch in range(nch):
                scr[a, ch, pl.ds(r, tm // d, stride=d), :] = (
                    o_refs[a][r, :, ch * LANES:(ch + 1) * LANES].astype(F32))
    ls_ = [l_refs[a][0] if d == 1 else lscr[a] for a, d in enumerate(dilations)]
    m = functools.reduce(jnp.maximum, ls_)
    es = [jnp.exp(l - m) for l in ls_]
    den = functools.reduce(lambda p, q: p + q, es)
    parts = []
    for e in es:
        wgt = e / den
        hi = wgt.astype(BF16)
        parts.append((hi, (wgt - hi.astype(F32)).astype(BF16)))
    span = 2 * LANES
    for c0 in range(0, w, span):
        sel = spread_ref[:, c0:c0 + span]
        nums = [None] * (span // LANES)
        for a, d in enumerate(dilations):
            hi, lo = parts[a]
            wide = jnp.dot(hi, sel, preferred_element_type=F32) + jnp.dot(lo, sel, preferred_element_type=F32)
            for k in range(span // LANES):
                ch = c0 // LANES + k
                o = o_refs[a][0, :, ch * LANES:(ch + 1) * LANES].astype(F32) if d == 1 else scr[a, ch]
                term = wide[:, k * LANES:(k + 1) * LANES] * o
                nums[k] = term if nums[k] is None else nums[k] + term
        for k in range(span // LANES):
            ch = c0 // LANES + k
            out[:, ch * LANES:(ch + 1) * LANES] = nums[k].astype(out.dtype)


def _merge(os_, ls_, dilations, tm=256):
    w = os_[0].shape[2]
    t = os_[0].shape[0] * os_[0].shape[1]
    o_specs = [pl.BlockSpec((d, tm // d, w), lambda i: (0, i, 0)) for d in dilations]
    l_specs = [pl.BlockSpec((d, tm // d, LANES), lambda i: (0, i, 0)) for d in dilations]
    spread = jnp.asarray(np.arange(LANES)[:, None] == np.arange(w)[None, :] // HEAD_DIM, dtype=BF16)
    return pl.pallas_call(
        functools.partial(_merge_kernel, dilations=tuple(dilations)), grid=(t // tm,),
        in_specs=o_specs + l_specs + [pl.BlockSpec((LANES, w), lambda i: (0, 0))],
        out_specs=pl.BlockSpec((tm, w), lambda i: (i, 0)),
        out_shape=jax.ShapeDtypeStruct((t, w), BF16),
        scratch_shapes=[pltpu.VMEM((len(dilations), w // LANES, tm, LANES), F32),
                        pltpu.VMEM((len(dilations), tm, LANES), F32)],
        compiler_params=_cparams(1), name="branch_merge",
    )(*os_, *ls_, spread)


def _outproj_kernel(oa_ref, ob_ref, wt_ref, wb_ref, x_ref, o_ref):
    acc = jnp.dot(oa_ref[...], wt_ref[...], preferred_element_type=F32)
    acc = acc + jnp.dot(ob_ref[...], wb_ref[...], preferred_element_type=F32)
    o_ref[...] = x_ref[...] + acc


def _outproj(o_a, o_b, w_out, x2d, tm, tn):
    t, d = x2d.shape
    ha, hb = o_a.shape[1], o_b.shape[1]
    assert ha == hb
    return pl.pallas_call(
        _outproj_kernel, grid=(t // tm, d // tn),
        in_specs=[
            pl.BlockSpec((tm, ha), lambda i, j: (i, 0)),
            pl.BlockSpec((tm, hb), lambda i, j: (i, 0)),
            pl.BlockSpec((ha, tn), lambda i, j: (0, j)),
            pl.BlockSpec((hb, tn), lambda i, j: (1, j)),
            pl.BlockSpec((tm, tn), lambda i, j: (i, j)),
        ],
        out_specs=pl.BlockSpec((tm, tn), lambda i, j: (i, j)),
        out_shape=jax.ShapeDtypeStruct((t, d), F32), compiler_params=_cparams(2), name="outproj",
    )(o_a, o_b, w_out, w_out, x2d)


def _router_kernel(x_ref, g_ref, whi_ref, wlo_ref, b_ref, tri_ref,
                   h_ref, idx_ref, gate_ref, rank_ref, cnt_ref, carry, h_scr):
    i = pl.program_id(0)

    @pl.when(i == 0)
    def _():
        carry[...] = jnp.zeros_like(carry)

    x = x_ref[...]
    ms = jnp.mean(x * x, axis=-1, keepdims=True)
    h = x * lax.rsqrt(ms + EPS) * g_ref[...]
    h_hi = h.astype(BF16)
    h_lo = (h - h_hi.astype(F32)).astype(BF16)
    u = pltpu.bitcast(h, jnp.uint32)
    r = (u + jnp.uint32(0x7FFF) + ((u >> 16) & jnp.uint32(1))) >> 16
    tok, words, _ = h_ref.shape
    for s in range(words):
        lo = r[:, 2 * s * LANES:(2 * s + 1) * LANES]
        hi = r[:, (2 * s + 1) * LANES:(2 * s + 2) * LANES]
        h_scr[pl.ds(s, tok, stride=words), :] = lo | (hi << 16)
    h_ref[...] = h_scr[...].reshape(tok, words, LANES)
    logits = (jnp.dot(h_hi, whi_ref[...], preferred_element_type=F32)
              + jnp.dot(h_lo, whi_ref[...], preferred_element_type=F32)
              + jnp.dot(h_hi, wlo_ref[...], preferred_element_type=F32)) + b_ref[...]

    tm = x.shape[0]
    lane = lax.broadcasted_iota(jnp.int32, (tm, LANES), 1).astype(F32)
    work = logits
    multihot = jnp.zeros((tm, LANES), F32)
    vals, idxs = [], []
    for _ in range(TOP_K):
        m = jnp.max(work, axis=-1, keepdims=True)
        ik = jnp.min(jnp.where(work == m, lane, float(LANES)), axis=-1, keepdims=True)
        sel = lane == ik
        work = jnp.where(sel, -jnp.inf, work)
        multihot = jnp.where(sel, 1.0, multihot)
        vals.append(m)
        idxs.append(ik)
    es = [jnp.exp(v - vals[0]) for v in vals]
    denom = es[0] + es[1] + es[2] + es[3]

    cum = jnp.dot(tri_ref[...], multihot.astype(BF16), preferred_element_type=F32) + carry[0:1, :]
    idx_t = jnp.zeros((tm, LANES), F32)
    gate_t = jnp.zeros((tm, LANES), F32)
    rank_t = jnp.zeros((tm, LANES), F32)
    for k in range(TOP_K):
        rk = jnp.sum(jnp.where(lane == idxs[k], cum, 0.0), axis=-1, keepdims=True)
        here = lane == float(k)
        idx_t = jnp.where(here, idxs[k], idx_t)
        gate_t = jnp.where(here, es[k] / denom, gate_t)
        rank_t = jnp.where(here, rk, rank_t)
    idx_ref[...] = idx_t.astype(jnp.int32)
    gate_ref[...] = gate_t
    rank_ref[...] = rank_t.astype(jnp.int32)
    new_carry = carry[0:1, :] + jnp.sum(multihot, axis=0, keepdims=True)
    carry[...] = jnp.broadcast_to(new_carry, carry.shape)
    cnt_ref[...] = jnp.broadcast_to(new_carry, cnt_ref.shape)


def _router(x2d, g2, w_router, b_router, tm=256):
    t, d = x2d.shape
    ne = w_router.shape[1]
    w_pad = jnp.zeros((d, LANES), F32).at[:, :ne].set(w_router)
    w_hi = w_pad.astype(BF16)
    w_lo = (w_pad - w_hi.astype(F32)).astype(BF16)
    b_pad = jnp.full((1, LANES), NEG_BIG, F32).at[0, :ne].set(b_router)
    tri = jnp.asarray(np.tril(np.ones((tm, tm), np.float32), -1), dtype=BF16)
    tile = lambda dt: jax.ShapeDtypeStruct((t, LANES), dt)
    row = pl.BlockSpec((tm, LANES), lambda i: (i, 0))
    const = lambda shape: pl.BlockSpec(shape, lambda i: (0, 0))
    return pl.pallas_call(
        _router_kernel, grid=(t // tm,),
        in_specs=[pl.BlockSpec((tm, d), lambda i: (i, 0)), const((1, d)), const((d, LANES)),
                  const((d, LANES)), const((1, LANES)), const((tm, tm))],
        out_specs=(pl.BlockSpec((tm, d // (2 * LANES), LANES), lambda i: (i, 0, 0)), row, row, row,
                   const((8, LANES))),
        out_shape=(jax.ShapeDtypeStruct((t, d // (2 * LANES), LANES), jnp.uint32),
                   tile(jnp.int32), tile(F32), tile(jnp.int32),
                   jax.ShapeDtypeStruct((8, LANES), F32)),
        scratch_shapes=[pltpu.VMEM((8, LANES), F32), pltpu.VMEM((tm * (d // (2 * LANES)), LANES), jnp.uint32)],
        compiler_params=_cparams(1), name="router",
    )(x2d, g2, w_hi, w_lo, b_pad, tri)


INDEX_SLOTS = 3
ROW_SLOTS = 2


def _row_prefetch_ring(idx_hbm, idx_smem, isem, start_rows, wait_rows):
    i = pl.program_id(0)
    n = pl.num_programs(0)

    def idx_copy(step):
        slot = lax.rem(step, INDEX_SLOTS)
        return pltpu.make_async_copy(idx_hbm.at[step], idx_smem.at[slot], isem.at[slot])

    @pl.when(i == 0)
    def _():
        idx_copy(0).start()
        idx_copy(0).wait()
        start_rows(0, 0, 0)

        @pl.when(n > 1)
        def _():
            idx_copy(1).start()

    @pl.when(i + 1 < n)
    def _():
        idx_copy(i + 1).wait()
        start_rows(i + 1, lax.rem(i + 1, INDEX_SLOTS), lax.rem(i + 1, ROW_SLOTS))

    @pl.when(i + 2 < n)
    def _():
        idx_copy(i + 2).start()

    row_slot = lax.rem(i, ROW_SLOTS)
    wait_rows(row_slot)
    return row_slot


def _gather_kernel(tok_hbm, h_hbm, xs_ref, tok_smem, buf, isem, rsem):
    g = GATHER_ROWS
    words = h_hbm.shape[1]

    def start_rows(step, idx_slot, row_slot):
        def body(r2, c):
            for prio in range(2):
                r = 2 * r2 + prio
                t = tok_smem[idx_slot, r]
                dst = buf.at[row_slot, pl.ds(pl.multiple_of(r * words, words), words)]
                pltpu.make_async_copy(h_hbm.at[t], dst, rsem.at[row_slot]).start(priority=prio)
            return c
        lax.fori_loop(0, g // 2, body, 0)

    def wait_rows(row_slot):
        pltpu.make_async_copy(buf.at[row_slot], buf.at[row_slot], rsem.at[row_slot]).wait()

    row_slot = _row_prefetch_ring(tok_hbm, tok_smem, isem, start_rows, wait_rows)

    for s in range(words):
        x = buf[row_slot, pl.ds(s, g, stride=words), :]
        lo = pltpu.bitcast(x << 16, F32)
        hi = pltpu.bitcast(x & jnp.uint32(0xFFFF0000), F32)
        xs_ref[:, 2 * s * LANES:(2 * s + 1) * LANES] = lo.astype(BF16)
        xs_ref[:, (2 * s + 1) * LANES:(2 * s + 2) * LANES] = hi.astype(BF16)


def _gather_rows(h_packed, row_tok):
    t, words, _ = h_packed.shape
    r_max = row_tok.shape[0]
    g = GATHER_ROWS
    n_steps = r_max // g
    tok2d = row_tok.reshape(n_steps, g)
    return pl.pallas_call(
        _gather_kernel, grid=(n_steps,),
        in_specs=[pl.BlockSpec(memory_space=pl.ANY), pl.BlockSpec(memory_space=pl.ANY)],
        out_specs=pl.BlockSpec((g, 2 * words * LANES), lambda i: (i, 0)),
        scratch_shapes=[pltpu.SMEM((INDEX_SLOTS, g), jnp.int32),
                        pltpu.VMEM((ROW_SLOTS, g * words, LANES), jnp.uint32),
                        pltpu.SemaphoreType.DMA((INDEX_SLOTS,)), pltpu.SemaphoreType.DMA((ROW_SLOTS,))],
        out_shape=jax.ShapeDtypeStruct((r_max, 2 * words * LANES), BF16),
        compiler_params=_cparams(1), name="row_gather",
    )(tok2d, h_packed)


PREP_ROWS = 1024


def _tile_clamp(i, na):
    return jnp.minimum(i, na[0] - 1)


def _weights_changed(te_ref, na_ref, i):
    cur = _tile_clamp(i, na_ref)
    prev = jnp.maximum(cur - 1, 0)
    return (i < na_ref[0]) & ((i == 0) | (te_ref[cur] != te_ref[prev]))


def _stream_expert_weights(w_hbm, raw, wsem, te_ref, meta_ref, nxt_ref, eidx_ref, convert):
    j = pl.program_id(0)
    i = pl.program_id(1)
    nj = pl.num_programs(0)
    tc = raw.shape[2]
    cur = _tile_clamp(i, meta_ref)

    def copy(e, jj, slot):
        col = pl.multiple_of(jj * tc, tc)
        return pltpu.make_async_copy(w_hbm.at[e, :, pl.ds(col, tc)], raw.at[slot], wsem.at[slot])

    @pl.when(_weights_changed(te_ref, meta_ref, i))
    def _():
        e = te_ref[cur]
        b = j * meta_ref[1] + eidx_ref[cur]
        slot = lax.rem(b, 2)
        nxt = nxt_ref[cur]

        @pl.when(b == 0)
        def _():
            copy(e, j, 0).start()

        @pl.when(nxt >= 0)
        def _():
            copy(nxt, j, 1 - slot).start()

        @pl.when((nxt < 0) & (j + 1 < nj))
        def _():
            copy(meta_ref[2], j + 1, 1 - slot).start()

        copy(e, j, slot).wait()
        convert(raw.at[slot])


def _row_cases(na_ref, tv_ref, i, tm, compute, zero):
    active = i < na_ref[0]
    valid = tv_ref[_tile_clamp(i, na_ref)]

    @pl.when(active & (valid > tm // 2))
    def _():
        compute(tm)

    @pl.when(active & (valid <= tm // 2))
    def _():
        compute(tm // 2)

    @pl.when(jnp.logical_not(active))
    def _():
        zero()


def _cast_rows(dst, src):
    n = dst.shape[0]
    for r0 in range(0, n, PREP_ROWS):
        rows = slice(r0, min(r0 + PREP_ROWS, n))
        dst[rows, :] = src[rows, :].astype(dst.dtype)


def _gateup_kernel(te_ref, na_ref, tv_ref, nxt_ref, eidx_ref, xs_ref, w_hbm, b_ref, pe_ref, h_ref,
                   w_s, raw, wsem):
    i = pl.program_id(1)
    d, tc = w_s.shape
    tm = h_ref.shape[0]
    _stream_expert_weights(w_hbm, raw, wsem, te_ref, na_ref, nxt_ref, eidx_ref,
                           functools.partial(_cast_rows, w_s))

    def compute(nrows):
        gu = jnp.dot(xs_ref[0:nrows, :], w_s[...], preferred_element_type=F32) + b_ref[...]
        acts = []
        for c in range(tc // LANES):
            g = gu[:, c * LANES:(c + 1) * LANES]
            u = pltpu.roll(g, LANES - 1, axis=1)
            gate = jnp.minimum(g, SWIGLU_LIMIT)
            up = jnp.clip(u, -SWIGLU_LIMIT, SWIGLU_LIMIT)
            acts.append(((up + 1.0) * gate * jax.nn.sigmoid(SWIGLU_ALPHA * gate)).astype(BF16))
        act = jnp.concatenate(acts, axis=1)
        h_ref[0:nrows, :] = jnp.dot(act, pe_ref[...], preferred_element_type=F32).astype(h_ref.dtype)
        if nrows < tm:
            h_ref[nrows:tm, :] = jnp.zeros((tm - nrows, h_ref.shape[1]), h_ref.dtype)

    def zero():
        h_ref[...] = jnp.zeros_like(h_ref)

    _row_cases(na_ref, tv_ref, i, tm, compute, zero)


def _down_kernel(te_ref, na_ref, tv_ref, nxt_ref, eidx_ref, h_ref, w_hbm, bd_ref, rw_ref, o_ref,
                 wd_s, o_scr, raw, wsem):
    i = pl.program_id(1)
    tm, chunks, _ = o_ref.shape
    _stream_expert_weights(w_hbm, raw, wsem, te_ref, na_ref, nxt_ref, eidx_ref,
                           functools.partial(_cast_rows, wd_s))

    def compute(nrows):
        out = jnp.dot(h_ref[0:nrows, :], wd_s[...], preferred_element_type=F32)
        out = (out + bd_ref[...]) * rw_ref[0:nrows, :]
        for c in range(chunks):
            o_scr[pl.ds(c, nrows, stride=chunks), :] = out[:, c * LANES:(c + 1) * LANES]
        if nrows < tm:
            o_scr[nrows * chunks:tm * chunks, :] = jnp.zeros(((tm - nrows) * chunks, LANES), F32)
        o_ref[...] = o_scr[...].reshape(tm, chunks, LANES)

    def zero():
        o_ref[...] = jnp.zeros_like(o_ref)

    _row_cases(na_ref, tv_ref, i, tm, compute, zero)


def _experts(xs, row_w, plan, w_gate_up, b_gate_up, w_down, bd, tn_ff, tn_d):
    r_max, d = xs.shape
    ff = w_down.shape[1]
    tm = EXPERT_TM
    n_tiles = r_max // tm
    tc = 2 * tn_ff
    p_even = jnp.asarray(np.arange(tc)[:, None] == 2 * np.arange(tn_ff)[None, :], dtype=BF16)
    n_plan = len(plan)
    hbm = pl.BlockSpec(memory_space=pl.ANY)
    weight_stream = lambda rows, cols: [pltpu.VMEM((2, rows, cols), F32), pltpu.SemaphoreType.DMA((2,))]

    def row_tile(j, i, te, meta, *_):
        return (_tile_clamp(i, meta), 0)

    def expert_cols(j, i, te, meta, *_):
        return (te[_tile_clamp(i, meta)], 0, j)

    gu_spec = pltpu.PrefetchScalarGridSpec(
        num_scalar_prefetch=n_plan, grid=(ff // tn_ff, n_tiles),
        in_specs=[
            pl.BlockSpec((tm, d), row_tile),
            hbm,
            pl.BlockSpec((None, 1, tc), expert_cols),
            pl.BlockSpec((tc, tn_ff), lambda j, i, *_: (0, 0)),
        ],
        out_specs=pl.BlockSpec((tm, tn_ff), lambda j, i, *_: (i, j)),
        scratch_shapes=[pltpu.VMEM((d, tc), BF16)] + weight_stream(d, tc))
    hidden = pl.pallas_call(
        _gateup_kernel, grid_spec=gu_spec, out_shape=jax.ShapeDtypeStruct((r_max, ff), BF16),
        compiler_params=_cparams(2), name="expert_gate_up",
    )(*plan, xs, w_gate_up, b_gate_up, p_even)
    dn_spec = pltpu.PrefetchScalarGridSpec(
        num_scalar_prefetch=n_plan, grid=(d // tn_d, n_tiles),
        in_specs=[
            pl.BlockSpec((tm, ff), row_tile),
            hbm,
            pl.BlockSpec((None, 1, tn_d), expert_cols),
            pl.BlockSpec((tm, 1), row_tile),
        ],
        out_specs=pl.BlockSpec((tm, tn_d // LANES, LANES), lambda j, i, *_: (i, j, 0)),
        scratch_shapes=[pltpu.VMEM((ff, tn_d), BF16), pltpu.VMEM((tm * (tn_d // LANES), LANES), F32)]
        + weight_stream(ff, tn_d))
    return pl.pallas_call(
        _down_kernel, grid_spec=dn_spec, out_shape=jax.ShapeDtypeStruct((r_max, d // LANES, LANES), F32),
        compiler_params=_cparams(2), name="expert_down",
    )(*plan, hidden, w_down, bd, row_w)


def _combine_kernel(dest_hbm, rows_hbm, x_ref, o_ref, dest_smem, buf, sum_scr, isem, rsem):
    tm = COMBINE_TM
    chunks = rows_hbm.shape[1]
    pitch = sum_scr.shape[0] // tm
    seg = tm * pitch

    @pl.when(pl.program_id(0) == 0)
    def _():
        buf[...] = jnp.zeros_like(buf)

    def start_rows(step, idx_slot, row_slot):
        def body(r, c):
            for k in range(TOP_K):
                src = dest_smem[idx_slot, r * TOP_K + k]
                off = pl.multiple_of((k * tm + r) * pitch, math.gcd(pitch, SUBLANES))
                pltpu.make_async_copy(rows_hbm.at[src], buf.at[row_slot, pl.ds(off, chunks)],
                                      rsem.at[row_slot]).start()
            return c
        lax.fori_loop(0, tm, body, 0, unroll=2)

    def wait_rows(row_slot):
        landed = buf.at[row_slot, pl.ds(0, TOP_K * tm * chunks)]
        pltpu.make_async_copy(landed, landed, rsem.at[row_slot]).wait()

    row_slot = _row_prefetch_ring(dest_hbm, dest_smem, isem, start_rows, wait_rows)

    acc = buf[row_slot, pl.ds(0, seg), :]
    for k in range(1, TOP_K):
        acc = acc + buf[row_slot, pl.ds(k * seg, seg), :]
    sum_scr[...] = acc
    for c in range(chunks):
        cols = slice(c * LANES, (c + 1) * LANES)
        o_ref[:, cols] = x_ref[:, cols] + sum_scr[pl.ds(c, tm, stride=pitch), :]


def _combine(rows, dest, x2d):
    t, d = x2d.shape
    chunks = rows.shape[1]
    tm = COMBINE_TM
    n_steps = t // tm
    dest2d = dest.reshape(n_steps, tm * TOP_K)
    blk = pl.BlockSpec((tm, d), lambda i: (i, 0))
    pitch = chunks + SUBLANES if chunks % (2 * SUBLANES) == 0 else chunks
    return pl.pallas_call(
        _combine_kernel, grid=(n_steps,),
        in_specs=[pl.BlockSpec(memory_space=pl.ANY), pl.BlockSpec(memory_space=pl.ANY), blk],
        out_specs=blk,
        out_shape=jax.ShapeDtypeStruct((t, d), F32),
        scratch_shapes=[pltpu.SMEM((INDEX_SLOTS, tm * TOP_K), jnp.int32),
                        pltpu.VMEM((ROW_SLOTS, TOP_K * tm * pitch, LANES), F32),
                        pltpu.VMEM((tm * pitch, LANES), F32),
                        pltpu.SemaphoreType.DMA((INDEX_SLOTS,)), pltpu.SemaphoreType.DMA((ROW_SLOTS,))],
        compiler_params=_cparams(1), name="combine",
    )(dest2d, rows, x2d)


def _moe(x_mid, g2, w_router, b_router, w_gate_up, b_gate_up, w_down, b_down, tn_ff, tn_d):
    t, d = x_mid.shape
    ne = w_router.shape[1]
    h2, idx_t, gate_t, rank_t, cnt = _router(x_mid, g2, w_router, b_router)

    tm = EXPERT_TM
    r_max = t * TOP_K + ne * tm
    n_tiles = r_max // tm
    counts = cnt[0, :ne].astype(jnp.int32)
    padded = (counts + tm - 1) // tm * tm
    pend = jnp.cumsum(padded)
    pstart = pend - padded
    idx = idx_t[:, :TOP_K]
    dest = (pstart[idx] + rank_t[:, :TOP_K]).astype(jnp.int32)
    n_rows_used = pend[-1:].astype(jnp.int32)
    n_active = n_rows_used // tm
    tok = jnp.repeat(jnp.arange(t, dtype=jnp.int32), TOP_K)
    gate_bits = lax.bitcast_convert_type(gate_t[:, :TOP_K].reshape(-1), jnp.int32)
    table = jnp.zeros((r_max, 2), jnp.int32).at[dest.reshape(-1)].set(
        jnp.stack([tok, gate_bits], axis=1), unique_indices=True)
    row_tok = table[:, 0]
    row_w = lax.bitcast_convert_type(table[:, 1], F32)[:, None]
    tile_start = jnp.arange(n_tiles, dtype=jnp.int32) * tm
    tile_e = jnp.minimum(jnp.sum((pend[None, :] <= tile_start[:, None]).astype(jnp.int32), axis=1), ne - 1)
    tile_valid = jnp.clip((pstart + counts)[tile_e] - tile_start, 0, tm).astype(jnp.int32)
    present = counts > 0
    eids = jnp.arange(ne, dtype=jnp.int32)
    later = present[None, :] & (eids[None, :] > eids[:, None])
    next_e = jnp.min(jnp.where(later, eids[None, :], ne), axis=1)
    next_e = jnp.where(next_e == ne, -1, next_e).astype(jnp.int32)
    rank_e = (jnp.cumsum(present.astype(jnp.int32)) - 1).astype(jnp.int32)
    meta = jnp.stack([n_active[0], jnp.sum(present.astype(jnp.int32)),
                      jnp.argmax(present).astype(jnp.int32)]).astype(jnp.int32)
    plan = (tile_e, meta, tile_valid, next_e[tile_e], rank_e[tile_e])

    xs = _gather_rows(h2, row_tok)

    rows = _experts(xs, row_w, plan, w_gate_up, b_gate_up[:, None, :], w_down, b_down[:, None, :], tn_ff, tn_d)
    return _combine(rows, dest, x_mid)


def _alibi_slopes(n):
    return (LOG2E * 2.0 ** (-ALIBI_MAX_BIAS * (np.arange(n, dtype=np.float64) + 1.0) / n)).astype(np.float32)


def _mixer(x2d, batch, seq, norm1_g, w_in, q_norm_swa, k_norm_swa, q_norm_dil, k_norm_dil, sinks, w_out,
           n_swa, n_kv, n_dil, tm_proj, tn_out):
    t, d = x2d.shape
    swa_q, swa_kv, dil_w = n_swa * HEAD_DIM, n_kv * HEAD_DIM, n_dil * HEAD_DIM
    assert n_swa // n_kv == GQA_GROUP and seq % (16 * ATTN_BLOCK) == 0
    s1, s2, s3 = swa_q, swa_q + swa_kv, swa_q + 2 * swa_kv

    def dup(wc):
        wc = wc.reshape(d, n_kv, HEAD_DIM)
        return jnp.concatenate([wc, wc], axis=-1).reshape(d, 2 * swa_kv)

    w = jnp.concatenate([w_in[:, :s1], dup(w_in[:, s1:s2]), dup(w_in[:, s2:s3]), w_in[:, s3:]],
                        axis=1).astype(BF16)
    scale = HEAD_DIM ** -0.5 * LOG2E
    ones = lambda n: jnp.ones((n,), F32)
    gains = jnp.concatenate([
        jnp.tile(q_norm_swa * scale, n_swa), jnp.tile(k_norm_swa, 2 * n_kv), ones(2 * swa_kv),
        jnp.tile(q_norm_dil * scale, n_dil), jnp.tile(k_norm_dil, n_dil), ones(dil_w)])[None, :]
    c = w.shape[1]
    bounds = np.cumsum([0, swa_q, 2 * swa_kv, 2 * swa_kv, dil_w, dil_w, dil_w])
    assert all(b % PROJ_TN == 0 for b in bounds)
    seg_norm = [1, 1, 0, 1, 1, 0]
    flags = np.zeros((c // PROJ_TN,), np.int32)
    for sidx in range(6):
        flags[bounds[sidx] // PROJ_TN:bounds[sidx + 1] // PROJ_TN] = seg_norm[sidx]
    qa0, ka0, va0, qb0, kb0, vb0 = (int(b) for b in bounds[:6])
    proj, *residue_major = _inproj(x2d, norm1_g[None, :], w, gains, jnp.asarray(flags), tm_proj, dil_col0=qb0)
    by_dilation = dict(zip(RESIDUE_DILATIONS, residue_major))

    sink_row = jnp.repeat(sinks.astype(F32), HEAD_DIM)[None, :]
    o_a = _band_attention(proj[None], jnp.asarray(_alibi_slopes(n_swa)), batch=batch, seq=seq, dilation=1,
                          q_col0=qa0, k_col0=ka0, v_col0=va0, n_heads=n_swa, kv_shared=True,
                          max_diff=SWA_WINDOW - 1, sinks=sink_row)[0]
    slopes_dil = jnp.asarray(_alibi_slopes(n_dil))
    outs, lses = [], []
    for window, dil in DILATED_BRANCHES:
        src, col0 = (proj[None], qb0) if dil == 1 else (by_dilation[dil], 0)
        o_i, lse_i = _band_attention(src, slopes_dil, batch=batch, seq=seq, dilation=dil,
                                     q_col0=col0, k_col0=col0 + kb0 - qb0, v_col0=col0 + vb0 - qb0,
                                     n_heads=n_dil, kv_shared=False, max_diff=window // dil)
        outs.append(o_i)
        lses.append(lse_i)
    o_b = _merge(outs, lses, [dil for _, dil in DILATED_BRANCHES])
    return _outproj(o_a, o_b, w_out.astype(BF16), x2d, tm_proj, tn_out)


def kernel(x, norm1_g, w_in, q_norm_swa, k_norm_swa, q_norm_dil, k_norm_dil, sinks, w_out, norm2_g,
           w_router, b_router, w_gate_up, b_gate_up, w_down, b_down):
    b, s, d = x.shape
    depth = norm1_g.shape[0]
    n_heads = d // HEAD_DIM
    n_swa = n_heads // 2
    n_kv = n_swa // GQA_GROUP
    n_dil = n_heads - n_swa
    x2d = x.reshape(b * s, d)
    for l in range(depth):
        x_mid = _mixer(x2d, b, s, norm1_g[l], w_in[l], q_norm_swa[l], k_norm_swa[l], q_norm_dil[l],
                       k_norm_dil[l], sinks[l], w_out[l], n_swa, n_kv, n_dil, tm_proj=512, tn_out=1024)
        x2d = _moe(x_mid, norm2_g[l][None, :], w_router[l], b_router[l], w_gate_up[l], b_gate_up[l],
                   w_down[l], b_down[l], tn_ff=512, tn_d=1024)
    return x2d.reshape(b, s, d)
```

```python
import functools
import math

import jax
import jax.numpy as jnp
import numpy as np
from jax import lax
from jax.experimental import pallas as pl
from jax.experimental.pallas import tpu as pltpu

F32 = jnp.float32
BF16 = jnp.bfloat16

HEAD_DIM = 64
LANES = 128
SUBLANES = 8
ATTN_BLOCK = 128
GQA_GROUP = 8
SWA_WINDOW = 128
DILATED_BRANCHES = ((128, 1), (512, 4), (2048, 16))
N_EXPERTS = 32
TOP_K = 4
SWIGLU_LIMIT = 7.0
SWIGLU_ALPHA = 1.702
ALIBI_MAX_BIAS = 8.0
EPS = 1e-6
MASK_DIST = 1e30
LOG2E = 1.4426950408889634
LN2 = 0.6931471805599453
NEG_BIG = -1e30

VMEM_LIMIT = 58 * 1024 * 1024

PROJ_TN = 512
ATTN_W_CHOICES = (2048, 1024, 512)
PAIRS_PER_KV = GQA_GROUP // 2
EXPERT_TM = 512
GATHER_ROWS = 256
COMBINE_TM = 128


def _cparams(n_axes):
    return pltpu.CompilerParams(dimension_semantics=("arbitrary",) * n_axes,
                                vmem_limit_bytes=VMEM_LIMIT)


RESIDUE_DILATIONS = tuple(d for _, d in DILATED_BRANCHES if d > 1)


def _inproj_kernel(flag_ref, x_ref, g_ref, w_ref, gain_ref, ones_ref, o_ref, *rest, dil_tile0):
    res_refs, (h_scr, y_scr) = rest[:len(RESIDUE_DILATIONS)], rest[len(RESIDUE_DILATIONS):]
    j = pl.program_id(1)
    tm, tn = o_ref.shape

    @pl.when(j == 0)
    def _():
        x = x_ref[...]
        ms = jnp.mean(x * x, axis=-1, keepdims=True)
        h_scr[...] = (x * lax.rsqrt(ms + EPS) * g_ref[...]).astype(BF16)

    acc = jnp.dot(h_scr[...], w_ref[...], preferred_element_type=F32)

    def emit(y):
        o_ref[...] = y.astype(BF16)
        for ch in range(tn // LANES):
            y_scr[ch] = y[:, ch * LANES:(ch + 1) * LANES]

    @pl.when(flag_ref[j] == 1)
    def _():
        ss = jnp.dot((acc * acc).astype(BF16), ones_ref[...], preferred_element_type=F32)
        emit(acc * lax.rsqrt(ss * (1.0 / HEAD_DIM) + EPS) * gain_ref[...])

    @pl.when(flag_ref[j] == 0)
    def _():
        emit(acc)

    @pl.when(j >= dil_tile0)
    def _():
        for d, ref in zip(RESIDUE_DILATIONS, res_refs):
            for r in range(d):
                for ch in range(tn // LANES):
                    rows = y_scr[ch, pl.ds(r, tm // d, stride=d), :]
                    ref[r, :, ch * LANES:(ch + 1) * LANES] = rows.astype(BF16)


def _inproj(x2d, g1, w, gains, flags, tm, dil_col0):
    t, d = x2d.shape
    c = w.shape[1]
    tn = PROJ_TN
    dil_tile0 = dil_col0 // tn
    cb = c - dil_col0
    head_id = np.arange(tn) // HEAD_DIM
    ones_bd = jnp.asarray(head_id[:, None] == head_id[None, :], dtype=BF16)
    res_specs = [pl.BlockSpec((dd, tm // dd, tn), lambda i, j, f: (0, i, jnp.maximum(j - dil_tile0, 0)))
                 for dd in RESIDUE_DILATIONS]
    res_shapes = [jax.ShapeDtypeStruct((dd, t // dd, cb), BF16) for dd in RESIDUE_DILATIONS]
    grid_spec = pltpu.PrefetchScalarGridSpec(
        num_scalar_prefetch=1,
        grid=(t // tm, c // tn),
        in_specs=[
            pl.BlockSpec((tm, d), lambda i, j, f: (i, 0)),
            pl.BlockSpec((1, d), lambda i, j, f: (0, 0)),
            pl.BlockSpec((d, tn), lambda i, j, f: (0, j)),
            pl.BlockSpec((1, tn), lambda i, j, f: (0, j)),
            pl.BlockSpec((tn, tn), lambda i, j, f: (0, 0)),
        ],
        out_specs=[pl.BlockSpec((tm, tn), lambda i, j, f: (i, j))] + res_specs,
        scratch_shapes=[pltpu.VMEM((tm, d), BF16), pltpu.VMEM((tn // LANES, tm, LANES), F32)],
    )
    return pl.pallas_call(
        functools.partial(_inproj_kernel, dil_tile0=dil_tile0), grid_spec=grid_spec,
        out_shape=[jax.ShapeDtypeStruct((t, c), BF16)] + res_shapes,
        compiler_params=_cparams(2), name="inproj",
    )(flags, x2d, g1, w, gains, ones_bd)


def _attn_kernel(slope_ref, q_ref, kp_ref, kc_ref, vp_ref, vc_ref, *rest,
                 n_pairs, kv_shared, max_diff, dist_scale, with_sink):
    if with_sink:
        sink_ref, o_ref = rest
        lse_ref = None
    else:
        o_ref, lse_ref = rest
    blk = pl.program_id(2)
    cg = pl.program_id(3)

    qi = lax.broadcasted_iota(jnp.int32, (ATTN_BLOCK, 2 * ATTN_BLOCK), 0)
    kj = lax.broadcasted_iota(jnp.int32, (ATTN_BLOCK, 2 * ATTN_BLOCK), 1)
    dist = qi + ATTN_BLOCK - kj
    valid = (dist >= 0) & (dist <= max_diff) & ((blk > 0) | (kj >= ATTN_BLOCK))
    dist_m = jnp.where(valid, (dist * dist_scale).astype(F32), MASK_DIST)

    lane = lax.broadcasted_iota(jnp.int32, (ATTN_BLOCK, LANES), 1)
    low = lane < HEAD_DIM
    if not with_sink:
        @pl.when(cg == 0)
        def _():
            lse_ref[...] = jnp.zeros_like(lse_ref)
        lse_tile = lse_ref[...]

    for p in range(n_pairs):
        cols = slice(p * LANES, (p + 1) * LANES)
        kv = p // PAIRS_PER_KV
        kcols = slice(kv * LANES, (kv + 1) * LANES) if kv_shared else cols
        q2 = q_ref[:, cols]
        kk = jnp.concatenate([kp_ref[:, kcols], kc_ref[:, kcols]], axis=0)
        vv = jnp.concatenate([vp_ref[:, kcols], vc_ref[:, kcols]], axis=0)
        outs, lses = [], []
        for hh in range(2):
            slope = slope_ref[cg * (2 * n_pairs) + 2 * p + hh]
            qm = jnp.where(low if hh == 0 else ~low, q2, jnp.zeros_like(q2))
            s = lax.dot_general(qm, kk, (((1,), (1,)), ((), ())),
                                preferred_element_type=F32)
            s = s - slope * dist_m
            m = jnp.max(s, axis=-1, keepdims=True)
            e = jnp.exp2(s - m)
            l = jnp.sum(e, axis=-1, keepdims=True)
            o = jnp.dot(e.astype(BF16), vv, preferred_element_type=F32)
            outs.append(o / l)
            lse = (m + jnp.log2(l)) * LN2
            if with_sink:
                lses.append(jnp.broadcast_to(lse, (ATTN_BLOCK, LANES)))
            else:
                head = cg * (2 * n_pairs) + 2 * p + hh
                lse_tile = jnp.where(lane == head, lse, lse_tile)
        o2 = jnp.where(low, outs[0], outs[1])
        if with_sink:
            lse2 = jnp.where(low, lses[0], lses[1])
            o2 = o2 * jax.nn.sigmoid(lse2 - sink_ref[:, cols])
        o_ref[:, cols] = o2.astype(o_ref.dtype)
    if not with_sink:
        lse_ref[...] = lse_tile


def _band_attention(proj, slopes, *, batch, seq, dilation, q_col0, k_col0, v_col0, n_heads,
                    kv_shared, max_diff, sinks=None):
    d = dilation
    assert proj.shape[0] == d
    lsub = seq // d
    nblk = lsub // ATTN_BLOCK
    width = n_heads * HEAD_DIM

    def kv_width(w):
        return LANES * (w // LANES // PAIRS_PER_KV) if kv_shared else w

    w = next(c for c in ATTN_W_CHOICES
             if width % c == 0 and q_col0 % c == 0 and k_col0 % kv_width(c) == 0 and v_col0 % kv_width(c) == 0)
    n_pairs = w // LANES
    ncg = width // w
    kw = kv_width(w)
    pv = proj

    def qmap(b, r, i, g, s):
        return (r, b * nblk + i, q_col0 // w + g)

    def kvmap(col0, prev):
        def f(b, r, i, g, s):
            blk = jnp.maximum(i - 1, 0) if prev else i
            return (r, b * nblk + blk, col0 // kw + g)
        return f

    def omap(b, r, i, g, s):
        return (r, b * nblk + i, g)

    in_specs = [
        pl.BlockSpec((None, ATTN_BLOCK, w), qmap),
        pl.BlockSpec((None, ATTN_BLOCK, kw), kvmap(k_col0, True)),
        pl.BlockSpec((None, ATTN_BLOCK, kw), kvmap(k_col0, False)),
        pl.BlockSpec((None, ATTN_BLOCK, kw), kvmap(v_col0, True)),
        pl.BlockSpec((None, ATTN_BLOCK, kw), kvmap(v_col0, False)),
    ]
    args = [pv, pv, pv, pv, pv]
    with_sink = sinks is not None
    o_shape = jax.ShapeDtypeStruct((d, batch * lsub, width), BF16)
    o_spec = pl.BlockSpec((None, ATTN_BLOCK, w), omap)
    if with_sink:
        in_specs.append(pl.BlockSpec((1, w), lambda b, r, i, g, s: (0, g)))
        args.append(sinks)
        out_shape, out_specs = o_shape, o_spec
    else:
        assert n_heads <= LANES
        out_shape = (o_shape, jax.ShapeDtypeStruct((d, batch * lsub, LANES), F32))
        out_specs = (o_spec, pl.BlockSpec((None, ATTN_BLOCK, LANES),
                                          lambda b, r, i, g, s: (r, b * nblk + i, 0)))
    kern = functools.partial(_attn_kernel, n_pairs=n_pairs, kv_shared=kv_shared,
                             max_diff=max_diff, dist_scale=d, with_sink=with_sink)
    grid_spec = pltpu.PrefetchScalarGridSpec(
        num_scalar_prefetch=1, grid=(batch, d, nblk, ncg),
        in_specs=in_specs, out_specs=out_specs)
    res = pl.pallas_call(kern, grid_spec=grid_spec, out_shape=out_shape,
                         compiler_params=_cparams(4), name=f"band_attn_d{d}" + ("_sink" if with_sink else ""),
                         )(slopes, *args)
    return res


def _merge_kernel(*refs, dilations):
    n = len(dilations)
    o_refs, l_refs = refs[:n], refs[n:2 * n]
    spread_ref, out, scr, lscr = refs[2 * n:]
    tm, w = out.shape
    nch = w // LANES
    for a, d in enumerate(dilations):
        if d == 1:
            continue
        for r in range(d):
            lscr[a, pl.ds(r, tm // d, stride=d), :] = l_refs[a][r]
            for ch in range(nch):
                scr[a, ch, pl.ds(r, tm // d, stride=d), :] = (
                    o_refs[a][r, :, ch * LANES:(ch + 1) * LANES].astype(F32))
    ls_ = [l_refs[a][0] if d == 1 else lscr[a] for a, d in enumerate(dilations)]
    m = functools.reduce(jnp.maximum, ls_)
    es = [jnp.exp(l - m) for l in ls_]
    den = functools.reduce(lambda p, q: p + q, es)
    parts = []
    for e in es:
        wgt = e / den
        hi = wgt.astype(BF16)
        parts.append((hi, (wgt - hi.astype(F32)).astype(BF16)))
    span = 2 * LANES
    for c0 in range(0, w, span):
        sel = spread_ref[:, c0:c0 + span]
        nums = [None] * (span // LANES)
        for a, d in enumerate(dilations):
            hi, lo = parts[a]
            wide = jnp.dot(hi, sel, preferred_element_type=F32) + jnp.dot(lo, sel, preferred_element_type=F32)
            for k in range(span // LANES):
                ch = c0 // LANES + k
                o = o_refs[a][0, :, ch * LANES:(ch + 1) * LANES].astype(F32) if d == 1 else scr[a, ch]
                term = wide[:, k * LANES:(k + 1) * LANES] * o
                nums[k] = term if nums[k] is None else nums[k] + term
        for k in range(span // LANES):
            ch = c0 // LANES + k
            out[:, ch * LANES:(ch + 1) * LANES] = nums[k].astype(out.dtype)


def _merge(os_, ls_, dilations, tm=256):
    w = os_[0].shape[2]
    t = os_[0].shape[0] * os_[0].shape[1]
    o_specs = [pl.BlockSpec((d, tm // d, w), lambda i: (0, i, 0)) for d in dilations]
    l_specs = [pl.BlockSpec((d, tm // d, LANES), lambda i: (0, i, 0)) for d in dilations]
    spread = jnp.asarray(np.arange(LANES)[:, None] == np.arange(w)[None, :] // HEAD_DIM, dtype=BF16)
    return pl.pallas_call(
        functools.partial(_merge_kernel, dilations=tuple(dilations)), grid=(t // tm,),
        in_specs=o_specs + l_specs + [pl.BlockSpec((LANES, w), lambda i: (0, 0))],
        out_specs=pl.BlockSpec((tm, w), lambda i: (i, 0)),
        out_shape=jax.ShapeDtypeStruct((t, w), BF16),
        scratch_shapes=[pltpu.VMEM((len(dilations), w // LANES, tm, LANES), F32),
                        pltpu.VMEM((len(dilations), tm, LANES), F32)],
        compiler_params=_cparams(1), name="branch_merge",
    )(*os_, *ls_, spread)


def _outproj_kernel(oa_ref, ob_ref, wt_ref, wb_ref, x_ref, o_ref):
    acc = jnp.dot(oa_ref[...], wt_ref[...], preferred_element_type=F32)
    acc = acc + jnp.dot(ob_ref[...], wb_ref[...], preferred_element_type=F32)
    o_ref[...] = x_ref[...] + acc


def _outproj(o_a, o_b, w_out, x2d, tm, tn):
    t, d = x2d.shape
    ha, hb = o_a.shape[1], o_b.shape[1]
    assert ha == hb
    return pl.pallas_call(
        _outproj_kernel, grid=(t // tm, d // tn),
        in_specs=[
            pl.BlockSpec((tm, ha), lambda i, j: (i, 0)),
            pl.BlockSpec((tm, hb), lambda i, j: (i, 0)),
            pl.BlockSpec((ha, tn), lambda i, j: (0, j)),
            pl.BlockSpec((hb, tn), lambda i, j: (1, j)),
            pl.BlockSpec((tm, tn), lambda i, j: (i, j)),
        ],
        out_specs=pl.BlockSpec((tm, tn), lambda i, j: (i, j)),
        out_shape=jax.ShapeDtypeStruct((t, d), F32), compiler_params=_cparams(2), name="outproj",
    )(o_a, o_b, w_out, w_out, x2d)


def _router_kernel(x_ref, g_ref, whi_ref, wlo_ref, b_ref, tri_ref,
                   h_ref, idx_ref, gate_ref, rank_ref, cnt_ref, carry, h_scr):
    i = pl.program_id(0)

    @pl.when(i == 0)
    def _():
        carry[...] = jnp.zeros_like(carry)

    x = x_ref[...]
    ms = jnp.mean(x * x, axis=-1, keepdims=True)
    h = x * lax.rsqrt(ms + EPS) * g_ref[...]
    h_hi = h.astype(BF16)
    h_lo = (h - h_hi.astype(F32)).astype(BF16)
    u = pltpu.bitcast(h, jnp.uint32)
    r = (u + jnp.uint32(0x7FFF) + ((u >> 16) & jnp.uint32(1))) >> 16
    tok, words, _ = h_ref.shape
    for s in range(words):
        lo = r[:, 2 * s * LANES:(2 * s + 1) * LANES]
        hi = r[:, (2 * s + 1) * LANES:(2 * s + 2) * LANES]
        h_scr[pl.ds(s, tok, stride=words), :] = lo | (hi << 16)
    h_ref[...] = h_scr[...].reshape(tok, words, LANES)
    logits = (jnp.dot(h_hi, whi_ref[...], preferred_element_type=F32)
              + jnp.dot(h_lo, whi_ref[...], preferred_element_type=F32)
              + jnp.dot(h_hi, wlo_ref[...], preferred_element_type=F32)) + b_ref[...]

    tm = x.shape[0]
    lane = lax.broadcasted_iota(jnp.int32, (tm, LANES), 1).astype(F32)
    work = logits
    multihot = jnp.zeros((tm, LANES), F32)
    vals, idxs = [], []
    for _ in range(TOP_K):
        m = jnp.max(work, axis=-1, keepdims=True)
        ik = jnp.min(jnp.where(work == m, lane, float(LANES)), axis=-1, keepdims=True)
        sel = lane == ik
        work = jnp.where(sel, -jnp.inf, work)
        multihot = jnp.where(sel, 1.0, multihot)
        vals.append(m)
        idxs.append(ik)
    es = [jnp.exp(v - vals[0]) for v in vals]
    denom = es[0] + es[1] + es[2] + es[3]

    cum = jnp.dot(tri_ref[...], multihot.astype(BF16), preferred_element_type=F32) + carry[0:1, :]
    idx_t = jnp.zeros((tm, LANES), F32)
    gate_t = jnp.zeros((tm, LANES), F32)
    rank_t = jnp.zeros((tm, LANES), F32)
    for k in range(TOP_K):
        rk = jnp.sum(jnp.where(lane == idxs[k], cum, 0.0), axis=-1, keepdims=True)
        here = lane == float(k)
        idx_t = jnp.where(here, idxs[k], idx_t)
        gate_t = jnp.where(here, es[k] / denom, gate_t)
        rank_t = jnp.where(here, rk, rank_t)
    idx_ref[...] = idx_t.astype(jnp.int32)
    gate_ref[...] = gate_t
    rank_ref[...] = rank_t.astype(jnp.int32)
    new_carry = carry[0:1, :] + jnp.sum(multihot, axis=0, keepdims=True)
    carry[...] = jnp.broadcast_to(new_carry, carry.shape)
    cnt_ref[...] = jnp.broadcast_to(new_carry, cnt_ref.shape)


def _router(x2d, g2, w_router, b_router, tm=256):
    t, d = x2d.shape
    ne = w_router.shape[1]
    w_pad = jnp.zeros((d, LANES), F32).at[:, :ne].set(w_router)
    w_hi = w_pad.astype(BF16)
    w_lo = (w_pad - w_hi.astype(F32)).astype(BF16)
    b_pad = jnp.full((1, LANES), NEG_BIG, F32).at[0, :ne].set(b_router)
    tri = jnp.asarray(np.tril(np.ones((tm, tm), np.float32), -1), dtype=BF16)
    tile = lambda dt: jax.ShapeDtypeStruct((t, LANES), dt)
    row = pl.BlockSpec((tm, LANES), lambda i: (i, 0))
    const = lambda shape: pl.BlockSpec(shape, lambda i: (0, 0))
    return pl.pallas_call(
        _router_kernel, grid=(t // tm,),
        in_specs=[pl.BlockSpec((tm, d), lambda i: (i, 0)), const((1, d)), const((d, LANES)),
                  const((d, LANES)), const((1, LANES)), const((tm, tm))],
        out_specs=(pl.BlockSpec((tm, d // (2 * LANES), LANES), lambda i: (i, 0, 0)), row, row, row,
                   const((8, LANES))),
        out_shape=(jax.ShapeDtypeStruct((t, d // (2 * LANES), LANES), jnp.uint32),
                   tile(jnp.int32), tile(F32), tile(jnp.int32),
                   jax.ShapeDtypeStruct((8, LANES), F32)),
        scratch_shapes=[pltpu.VMEM((8, LANES), F32), pltpu.VMEM((tm * (d // (2 * LANES)), LANES), jnp.uint32)],
        compiler_params=_cparams(1), name="router",
    )(x2d, g2, w_hi, w_lo, b_pad, tri)


INDEX_SLOTS = 3
ROW_SLOTS = 2


def _row_prefetch_ring(idx_hbm, idx_smem, isem, start_rows, wait_rows):
    i = pl.program_id(0)
    n = pl.num_programs(0)

    def idx_copy(step):
        slot = lax.rem(step, INDEX_SLOTS)
        return pltpu.make_async_copy(idx_hbm.at[step], idx_smem.at[slot], isem.at[slot])

    @pl.when(i == 0)
    def _():
        idx_copy(0).start()
        idx_copy(0).wait()
        start_rows(0, 0, 0)

        @pl.when(n > 1)
        def _():
            idx_copy(1).start()

    @pl.when(i + 1 < n)
    def _():
        idx_copy(i + 1).wait()
        start_rows(i + 1, lax.rem(i + 1, INDEX_SLOTS), lax.rem(i + 1, ROW_SLOTS))

    @pl.when(i + 2 < n)
    def _():
        idx_copy(i + 2).start()

    row_slot = lax.rem(i, ROW_SLOTS)
    wait_rows(row_slot)
    return row_slot


def _gather_kernel(nvalid_ref, tok_hbm, h_hbm, xs_ref, tok_smem, buf, isem, rsem):
    g = GATHER_ROWS
    words = h_hbm.shape[1]

    @pl.when(pl.program_id(0) == 0)
    def _():
        buf[...] = jnp.zeros_like(buf)

    def row_copy(t, r, row_slot):
        dst = buf.at[row_slot, pl.ds(pl.multiple_of(r * words, words), words)]
        return pltpu.make_async_copy(h_hbm.at[t], dst, rsem.at[row_slot])

    def start_rows(step, idx_slot, row_slot):
        def body(r2, c):
            for prio in range(2):
                r = 2 * r2 + prio
                t = tok_smem[idx_slot, r]

                @pl.when(t >= 0)
                def _():
                    row_copy(t, r, row_slot).start(priority=prio)
            return c
        lax.fori_loop(0, g // 2, body, 0)

    def wait_rows(row_slot):
        def body(r, c):
            row_copy(0, 0, row_slot).wait()
            return c
        lax.fori_loop(0, nvalid_ref[pl.program_id(0)], body, 0)

    row_slot = _row_prefetch_ring(tok_hbm, tok_smem, isem, start_rows, wait_rows)

    for s in range(words):
        x = buf[row_slot, pl.ds(s, g, stride=words), :]
        lo = pltpu.bitcast(x << 16, F32)
        hi = pltpu.bitcast(x & jnp.uint32(0xFFFF0000), F32)
        xs_ref[:, 2 * s * LANES:(2 * s + 1) * LANES] = lo.astype(BF16)
        xs_ref[:, (2 * s + 1) * LANES:(2 * s + 2) * LANES] = hi.astype(BF16)


def _gather_rows(h_packed, row_tok):
    t, words, _ = h_packed.shape
    r_max = row_tok.shape[0]
    g = GATHER_ROWS
    n_steps = r_max // g
    tok2d = row_tok.reshape(n_steps, g)
    nvalid = jnp.sum((tok2d >= 0).astype(jnp.int32), axis=1)
    grid_spec = pltpu.PrefetchScalarGridSpec(
        num_scalar_prefetch=1, grid=(n_steps,),
        in_specs=[pl.BlockSpec(memory_space=pl.ANY), pl.BlockSpec(memory_space=pl.ANY)],
        out_specs=pl.BlockSpec((g, 2 * words * LANES), lambda i, nv: (i, 0)),
        scratch_shapes=[pltpu.SMEM((INDEX_SLOTS, g), jnp.int32),
                        pltpu.VMEM((ROW_SLOTS, g * words, LANES), jnp.uint32),
                        pltpu.SemaphoreType.DMA((INDEX_SLOTS,)), pltpu.SemaphoreType.DMA((ROW_SLOTS,))])
    return pl.pallas_call(
        _gather_kernel, grid_spec=grid_spec,
        out_shape=jax.ShapeDtypeStruct((r_max, 2 * words * LANES), BF16),
        compiler_params=_cparams(1), name="row_gather",
    )(nvalid, tok2d, h_packed)


PREP_ROWS = 1024


def _tile_clamp(i, na):
    return jnp.minimum(i, na[0] - 1)


def _weights_changed(te_ref, na_ref, i):
    cur = _tile_clamp(i, na_ref)
    prev = jnp.maximum(cur - 1, 0)
    return (i < na_ref[0]) & ((i == 0) | (te_ref[cur] != te_ref[prev]))


def _stream_expert_weights(w_hbm, raw, wsem, te_ref, meta_ref, nxt_ref, eidx_ref, convert):
    j = pl.program_id(0)
    i = pl.program_id(1)
    nj = pl.num_programs(0)
    tc = raw.shape[2]
    cur = _tile_clamp(i, meta_ref)

    def copy(e, jj, slot):
        col = pl.multiple_of(jj * tc, tc)
        return pltpu.make_async_copy(w_hbm.at[e, :, pl.ds(col, tc)], raw.at[slot], wsem.at[slot])

    @pl.when(_weights_changed(te_ref, meta_ref, i))
    def _():
        e = te_ref[cur]
        b = j * meta_ref[1] + eidx_ref[cur]
        slot = lax.rem(b, 2)
        nxt = nxt_ref[cur]

        @pl.when(b == 0)
        def _():
            copy(e, j, 0).start()

        @pl.when(nxt >= 0)
        def _():
            copy(nxt, j, 1 - slot).start()

        @pl.when((nxt < 0) & (j + 1 < nj))
        def _():
            copy(meta_ref[2], j + 1, 1 - slot).start()

        copy(e, j, slot).wait()
        convert(raw.at[slot])


def _row_cases(na_ref, tv_ref, i, tm, compute, zero):
    active = i < na_ref[0]
    valid = tv_ref[_tile_clamp(i, na_ref)]

    @pl.when(active & (valid > tm // 2))
    def _():
        compute(tm)

    @pl.when(active & (valid <= tm // 2))
    def _():
        compute(tm // 2)

    @pl.when(jnp.logical_not(active))
    def _():
        zero()


def _cast_rows(dst, src):
    n = dst.shape[0]
    for r0 in range(0, n, PREP_ROWS):
        rows = slice(r0, min(r0 + PREP_ROWS, n))
        dst[rows, :] = src[rows, :].astype(dst.dtype)


def _gateup_kernel(te_ref, na_ref, tv_ref, nxt_ref, eidx_ref, xs_ref, w_hbm, b_ref, pe_ref, h_ref,
                   w_s, raw, wsem):
    i = pl.program_id(1)
    d, tc = w_s.shape
    tm = h_ref.shape[0]
    _stream_expert_weights(w_hbm, raw, wsem, te_ref, na_ref, nxt_ref, eidx_ref,
                           functools.partial(_cast_rows, w_s))

    def compute(nrows):
        gu = jnp.dot(xs_ref[0:nrows, :], w_s[...], preferred_element_type=F32) + b_ref[...]
        acts = []
        for c in range(tc // LANES):
            g = gu[:, c * LANES:(c + 1) * LANES]
            u = pltpu.roll(g, LANES - 1, axis=1)
            gate = jnp.minimum(g, SWIGLU_LIMIT)
            up = jnp.clip(u, -SWIGLU_LIMIT, SWIGLU_LIMIT)
            acts.append(((up + 1.0) * gate * jax.nn.sigmoid(SWIGLU_ALPHA * gate)).astype(BF16))
        act = jnp.concatenate(acts, axis=1)
        h_ref[0:nrows, :] = jnp.dot(act, pe_ref[...], preferred_element_type=F32).astype(h_ref.dtype)
        if nrows < tm:
            h_ref[nrows:tm, :] = jnp.zeros((tm - nrows, h_ref.shape[1]), h_ref.dtype)

    def zero():
        h_ref[...] = jnp.zeros_like(h_ref)

    _row_cases(na_ref, tv_ref, i, tm, compute, zero)


def _down_kernel(te_ref, na_ref, tv_ref, nxt_ref, eidx_ref, h_ref, w_hbm, bd_ref, rw_ref, o_ref,
                 wd_s, o_scr, raw, wsem):
    i = pl.program_id(1)
    tm, chunks, _ = o_ref.shape
    _stream_expert_weights(w_hbm, raw, wsem, te_ref, na_ref, nxt_ref, eidx_ref,
                           functools.partial(_cast_rows, wd_s))

    def compute(nrows):
        out = jnp.dot(h_ref[0:nrows, :], wd_s[...], preferred_element_type=F32)
        out = (out + bd_ref[...]) * rw_ref[0:nrows, :]
        for c in range(chunks):
            o_scr[pl.ds(c, nrows, stride=chunks), :] = out[:, c * LANES:(c + 1) * LANES]
        if nrows < tm:
            o_scr[nrows * chunks:tm * chunks, :] = jnp.zeros(((tm - nrows) * chunks, LANES), F32)
        o_ref[...] = o_scr[...].reshape(tm, chunks, LANES)

    def zero():
        o_ref[...] = jnp.zeros_like(o_ref)

    _row_cases(na_ref, tv_ref, i, tm, compute, zero)


def _experts(xs, row_w, plan, w_gate_up, b_gate_up, w_down, bd, tn_ff, tn_d):
    r_max, d = xs.shape
    ff = w_down.shape[1]
    tm = EXPERT_TM
    n_tiles = r_max // tm
    tc = 2 * tn_ff
    p_even = jnp.asarray(np.arange(tc)[:, None] == 2 * np.arange(tn_ff)[None, :], dtype=BF16)
    n_plan = len(plan)
    hbm = pl.BlockSpec(memory_space=pl.ANY)
    weight_stream = lambda rows, cols: [pltpu.VMEM((2, rows, cols), F32), pltpu.SemaphoreType.DMA((2,))]

    def row_tile(j, i, te, meta, *_):
        return (_tile_clamp(i, meta), 0)

    def expert_cols(j, i, te, meta, *_):
        return (te[_tile_clamp(i, meta)], 0, j)

    gu_spec = pltpu.PrefetchScalarGridSpec(
        num_scalar_prefetch=n_plan, grid=(ff // tn_ff, n_tiles),
        in_specs=[
            pl.BlockSpec((tm, d), row_tile),
            hbm,
            pl.BlockSpec((None, 1, tc), expert_cols),
            pl.BlockSpec((tc, tn_ff), lambda j, i, *_: (0, 0)),
        ],
        out_specs=pl.BlockSpec((tm, tn_ff), lambda j, i, *_: (i, j)),
        scratch_shapes=[pltpu.VMEM((d, tc), BF16)] + weight_stream(d, tc))
    hidden = pl.pallas_call(
        _gateup_kernel, grid_spec=gu_spec, out_shape=jax.ShapeDtypeStruct((r_max, ff), BF16),
        compiler_params=_cparams(2), name="expert_gate_up",
    )(*plan, xs, w_gate_up, b_gate_up, p_even)
    dn_spec = pltpu.PrefetchScalarGridSpec(
        num_scalar_prefetch=n_plan, grid=(d // tn_d, n_tiles),
        in_specs=[
            pl.BlockSpec((tm, ff), row_tile),
            hbm,
            pl.BlockSpec((None, 1, tn_d), expert_cols),
            pl.BlockSpec((tm, 1), row_tile),
        ],
        out_specs=pl.BlockSpec((tm, tn_d // LANES, LANES), lambda j, i, *_: (i, j, 0)),
        scratch_shapes=[pltpu.VMEM((ff, tn_d), BF16), pltpu.VMEM((tm * (tn_d // LANES), LANES), F32)]
        + weight_stream(ff, tn_d))
    return pl.pallas_call(
        _down_kernel, grid_spec=dn_spec, out_shape=jax.ShapeDtypeStruct((r_max, d // LANES, LANES), F32),
        compiler_params=_cparams(2), name="expert_down",
    )(*plan, hidden, w_down, bd, row_w)


def _combine_kernel(dest_hbm, rows_hbm, x_ref, o_ref, dest_smem, buf, sum_scr, isem, rsem):
    tm = COMBINE_TM
    chunks = rows_hbm.shape[1]
    pitch = sum_scr.shape[0] // tm
    seg = tm * pitch

    @pl.when(pl.program_id(0) == 0)
    def _():
        buf[...] = jnp.zeros_like(buf)

    def start_rows(step, idx_slot, row_slot):
        def body(r, c):
            for k in range(TOP_K):
                src = dest_smem[idx_slot, r * TOP_K + k]
                off = pl.multiple_of((k * tm + r) * pitch, math.gcd(pitch, SUBLANES))
                pltpu.make_async_copy(rows_hbm.at[src], buf.at[row_slot, pl.ds(off, chunks)],
                                      rsem.at[row_slot]).start()
            return c
        lax.fori_loop(0, tm, body, 0, unroll=2)

    def wait_rows(row_slot):
        landed = buf.at[row_slot, pl.ds(0, TOP_K * tm * chunks)]
        pltpu.make_async_copy(landed, landed, rsem.at[row_slot]).wait()

    row_slot = _row_prefetch_ring(dest_hbm, dest_smem, isem, start_rows, wait_rows)

    acc = buf[row_slot, pl.ds(0, seg), :]
    for k in range(1, TOP_K):
        acc = acc + buf[row_slot, pl.ds(k * seg, seg), :]
    sum_scr[...] = acc
    for c in range(chunks):
        cols = slice(c * LANES, (c + 1) * LANES)
        o_ref[:, cols] = x_ref[:, cols] + sum_scr[pl.ds(c, tm, stride=pitch), :]


def _combine(rows, dest, x2d):
    t, d = x2d.shape
    chunks = rows.shape[1]
    tm = COMBINE_TM
    n_steps = t // tm
    dest2d = dest.reshape(n_steps, tm * TOP_K)
    blk = pl.BlockSpec((tm, d), lambda i: (i, 0))
    pitch = chunks + SUBLANES if chunks % (2 * SUBLANES) == 0 else chunks
    return pl.pallas_call(
        _combine_kernel, grid=(n_steps,),
        in_specs=[pl.BlockSpec(memory_space=pl.ANY), pl.BlockSpec(memory_space=pl.ANY), blk],
        out_specs=blk,
        out_shape=jax.ShapeDtypeStruct((t, d), F32),
        scratch_shapes=[pltpu.SMEM((INDEX_SLOTS, tm * TOP_K), jnp.int32),
                        pltpu.VMEM((ROW_SLOTS, TOP_K * tm * pitch, LANES), F32),
                        pltpu.VMEM((tm * pitch, LANES), F32),
                        pltpu.SemaphoreType.DMA((INDEX_SLOTS,)), pltpu.SemaphoreType.DMA((ROW_SLOTS,))],
        compiler_params=_cparams(1), name="combine",
    )(dest2d, rows, x2d)


def _moe(x_mid, g2, w_router, b_router, w_gate_up, b_gate_up, w_down, b_down, tn_ff, tn_d):
    t, d = x_mid.shape
    ne = w_router.shape[1]
    h2, idx_t, gate_t, rank_t, cnt = _router(x_mid, g2, w_router, b_router)

    tm = EXPERT_TM
    r_max = t * TOP_K + ne * tm
    n_tiles = r_max // tm
    counts = cnt[0, :ne].astype(jnp.int32)
    padded = (counts + tm - 1) // tm * tm
    pend = jnp.cumsum(padded)
    pstart = pend - padded
    idx = idx_t[:, :TOP_K]
    dest = (pstart[idx] + rank_t[:, :TOP_K]).astype(jnp.int32)
    n_rows_used = pend[-1:].astype(jnp.int32)
    n_active = n_rows_used // tm
    tok = jnp.repeat(jnp.arange(t, dtype=jnp.int32), TOP_K)
    gate_bits = lax.bitcast_convert_type(gate_t[:, :TOP_K].reshape(-1), jnp.int32)
    empty = jnp.broadcast_to(jnp.asarray([-1, 0], jnp.int32), (r_max, 2))
    table = empty.at[dest.reshape(-1)].set(
        jnp.stack([tok, gate_bits], axis=1), unique_indices=True)
    row_tok = table[:, 0]
    row_w = lax.bitcast_convert_type(table[:, 1], F32)[:, None]
    tile_start = jnp.arange(n_tiles, dtype=jnp.int32) * tm
    tile_e = jnp.minimum(jnp.sum((pend[None, :] <= tile_start[:, None]).astype(jnp.int32), axis=1), ne - 1)
    tile_valid = jnp.clip((pstart + counts)[tile_e] - tile_start, 0, tm).astype(jnp.int32)
    present = counts > 0
    eids = jnp.arange(ne, dtype=jnp.int32)
    later = present[None, :] & (eids[None, :] > eids[:, None])
    next_e = jnp.min(jnp.where(later, eids[None, :], ne), axis=1)
    next_e = jnp.where(next_e == ne, -1, next_e).astype(jnp.int32)
    rank_e = (jnp.cumsum(present.astype(jnp.int32)) - 1).astype(jnp.int32)
    meta = jnp.stack([n_active[0], jnp.sum(present.astype(jnp.int32)),
                      jnp.argmax(present).astype(jnp.int32)]).astype(jnp.int32)
    plan = (tile_e, meta, tile_valid, next_e[tile_e], rank_e[tile_e])

    xs = _gather_rows(h2, row_tok)

    rows = _experts(xs, row_w, plan, w_gate_up, b_gate_up[:, None, :], w_down, b_down[:, None, :], tn_ff, tn_d)
    return _combine(rows, dest, x_mid)


def _alibi_slopes(n):
    return (LOG2E * 2.0 ** (-ALIBI_MAX_BIAS * (np.arange(n, dtype=np.float64) + 1.0) / n)).astype(np.float32)


def _mixer(x2d, batch, seq, norm1_g, w_in, q_norm_swa, k_norm_swa, q_norm_dil, k_norm_dil, sinks, w_out,
           n_swa, n_kv, n_dil, tm_proj, tn_out):
    t, d = x2d.shape
    swa_q, swa_kv, dil_w = n_swa * HEAD_DIM, n_kv * HEAD_DIM, n_dil * HEAD_DIM
    assert n_swa // n_kv == GQA_GROUP and seq % (16 * ATTN_BLOCK) == 0
    s1, s2, s3 = swa_q, swa_q + swa_kv, swa_q + 2 * swa_kv

    def dup(wc):
        wc = wc.reshape(d, n_kv, HEAD_DIM)
        return jnp.concatenate([wc, wc], axis=-1).reshape(d, 2 * swa_kv)

    w = jnp.concatenate([w_in[:, :s1], dup(w_in[:, s1:s2]), dup(w_in[:, s2:s3]), w_in[:, s3:]],
                        axis=1).astype(BF16)
    scale = HEAD_DIM ** -0.5 * LOG2E
    ones = lambda n: jnp.ones((n,), F32)
    gains = jnp.concatenate([
        jnp.tile(q_norm_swa * scale, n_swa), jnp.tile(k_norm_swa, 2 * n_kv), ones(2 * swa_kv),
        jnp.tile(q_norm_dil * scale, n_dil), jnp.tile(k_norm_dil, n_dil), ones(dil_w)])[None, :]
    c = w.shape[1]
    bounds = np.cumsum([0, swa_q, 2 * swa_kv, 2 * swa_kv, dil_w, dil_w, dil_w])
    assert all(b % PROJ_TN == 0 for b in bounds)
    seg_norm = [1, 1, 0, 1, 1, 0]
    flags = np.zeros((c // PROJ_TN,), np.int32)
    for sidx in range(6):
        flags[bounds[sidx] // PROJ_TN:bounds[sidx + 1] // PROJ_TN] = seg_norm[sidx]
    qa0, ka0, va0, qb0, kb0, vb0 = (int(b) for b in bounds[:6])
    proj, *residue_major = _inproj(x2d, norm1_g[None, :], w, gains, jnp.asarray(flags), tm_proj, dil_col0=qb0)
    by_dilation = dict(zip(RESIDUE_DILATIONS, residue_major))

    sink_row = jnp.repeat(sinks.astype(F32), HEAD_DIM)[None, :]
    o_a = _band_attention(proj[None], jnp.asarray(_alibi_slopes(n_swa)), batch=batch, seq=seq, dilation=1,
                          q_col0=qa0, k_col0=ka0, v_col0=va0, n_heads=n_swa, kv_shared=True,
                          max_diff=SWA_WINDOW - 1, sinks=sink_row)[0]
    slopes_dil = jnp.asarray(_alibi_slopes(n_dil))
    outs, lses = [], []
    for window, dil in DILATED_BRANCHES:
        src, col0 = (proj[None], qb0) if dil == 1 else (by_dilation[dil], 0)
        o_i, lse_i = _band_attention(src, slopes_dil, batch=batch, seq=seq, dilation=dil,
                                     q_col0=col0, k_col0=col0 + kb0 - qb0, v_col0=col0 + vb0 - qb0,
                                     n_heads=n_dil, kv_shared=False, max_diff=window // dil)
        outs.append(o_i)
        lses.append(lse_i)
    o_b = _merge(outs, lses, [dil for _, dil in DILATED_BRANCHES])
    return _outproj(o_a, o_b, w_out.astype(BF16), x2d, tm_proj, tn_out)


def kernel(x, norm1_g, w_in, q_norm_swa, k_norm_swa, q_norm_dil, k_norm_dil, sinks, w_out, norm2_g,
           w_router, b_router, w_gate_up, b_gate_up, w_down, b_down):
    b, s, d = x.shape
    depth = norm1_g.shape[0]
    n_heads = d // HEAD_DIM
    n_swa = n_heads // 2
    n_kv = n_swa // GQA_GROUP
    n_dil = n_heads - n_swa
    x2d = x.reshape(b * s, d)
    for l in range(depth):
        x_mid = _mixer(x2d, b, s, norm1_g[l], w_in[l], q_norm_swa[l], k_norm_swa[l], q_norm_dil[l],
                       k_norm_dil[l], sinks[l], w_out[l], n_swa, n_kv, n_dil, tm_proj=512, tn_out=1024)
        x2d = _moe(x_mid, norm2_g[l][None, :], w_router[l], b_router[l], w_gate_up[l], b_gate_up[l],
                   w_down[l], b_down[l], tn_ff=512, tn_d=1024)
    return x2d.reshape(b, s, d)
```

```python
import functools
import math

import jax
import jax.numpy as jnp
import numpy as np
from jax import lax
from jax.experimental import pallas as pl
from jax.experimental.pallas import tpu as pltpu

F32 = jnp.float32
BF16 = jnp.bfloat16

HEAD_DIM = 64
LANES = 128
SUBLANES = 8
ATTN_BLOCK = 128
GQA_GROUP = 8
SWA_WINDOW = 128
DILATED_BRANCHES = ((128, 1), (512, 4), (2048, 16))
N_EXPERTS = 32
TOP_K = 4
SWIGLU_LIMIT = 7.0
SWIGLU_ALPHA = 1.702
ALIBI_MAX_BIAS = 8.0
EPS = 1e-6
MASK_DIST = 1e30
LOG2E = 1.4426950408889634
LN2 = 0.6931471805599453
NEG_BIG = -1e30

VMEM_LIMIT = 58 * 1024 * 1024

PROJ_TN = 512
ATTN_W_CHOICES = (2048, 1024, 512)
PAIRS_PER_KV = GQA_GROUP // 2
EXPERT_TM = 512
GATHER_ROWS = 512
COMBINE_TM = 128


def _cparams(n_axes):
    return pltpu.CompilerParams(dimension_semantics=("arbitrary",) * n_axes,
                                vmem_limit_bytes=VMEM_LIMIT)


RESIDUE_DILATIONS = tuple(d for _, d in DILATED_BRANCHES if d > 1)


def _inproj_kernel(flag_ref, x_ref, g_ref, w_ref, gain_ref, ones_ref, o_ref, *rest, dil_tile0):
    res_refs, (h_scr, y_scr) = rest[:len(RESIDUE_DILATIONS)], rest[len(RESIDUE_DILATIONS):]
    j = pl.program_id(1)
    tm, tn = o_ref.shape

    @pl.when(j == 0)
    def _():
        x = x_ref[...]
        ms = jnp.mean(x * x, axis=-1, keepdims=True)
        h_scr[...] = (x * lax.rsqrt(ms + EPS) * g_ref[...]).astype(BF16)

    acc = jnp.dot(h_scr[...], w_ref[...], preferred_element_type=F32)

    def emit(y):
        o_ref[...] = y.astype(BF16)
        for ch in range(tn // LANES):
            y_scr[ch] = y[:, ch * LANES:(ch + 1) * LANES]

    @pl.when(flag_ref[j] == 1)
    def _():
        ss = jnp.dot((acc * acc).astype(BF16), ones_ref[...], preferred_element_type=F32)
        emit(acc * lax.rsqrt(ss * (1.0 / HEAD_DIM) + EPS) * gain_ref[...])

    @pl.when(flag_ref[j] == 0)
    def _():
        emit(acc)

    @pl.when(j >= dil_tile0)
    def _():
        for d, ref in zip(RESIDUE_DILATIONS, res_refs):
            for r in range(d):
                for ch in range(tn // LANES):
                    rows = y_scr[ch, pl.ds(r, tm // d, stride=d), :]
                    ref[r, :, ch * LANES:(ch + 1) * LANES] = rows.astype(BF16)


def _inproj(x2d, g1, w, gains, flags, tm, dil_col0):
    t, d = x2d.shape
    c = w.shape[1]
    tn = PROJ_TN
    dil_tile0 = dil_col0 // tn
    cb = c - dil_col0
    head_id = np.arange(tn) // HEAD_DIM
    ones_bd = jnp.asarray(head_id[:, None] == head_id[None, :], dtype=BF16)
    res_specs = [pl.BlockSpec((dd, tm // dd, tn), lambda i, j, f: (0, i, jnp.maximum(j - dil_tile0, 0)))
                 for dd in RESIDUE_DILATIONS]
    res_shapes = [jax.ShapeDtypeStruct((dd, t // dd, cb), BF16) for dd in RESIDUE_DILATIONS]
    grid_spec = pltpu.PrefetchScalarGridSpec(
        num_scalar_prefetch=1,
        grid=(t // tm, c // tn),
        in_specs=[
            pl.BlockSpec((tm, d), lambda i, j, f: (i, 0)),
            pl.BlockSpec((1, d), lambda i, j, f: (0, 0)),
            pl.BlockSpec((d, tn), lambda i, j, f: (0, j)),
            pl.BlockSpec((1, tn), lambda i, j, f: (0, j)),
            pl.BlockSpec((tn, tn), lambda i, j, f: (0, 0)),
        ],
        out_specs=[pl.BlockSpec((tm, tn), lambda i, j, f: (i, j))] + res_specs,
        scratch_shapes=[pltpu.VMEM((tm, d), BF16), pltpu.VMEM((tn // LANES, tm, LANES), F32)],
    )
    return pl.pallas_call(
        functools.partial(_inproj_kernel, dil_tile0=dil_tile0), grid_spec=grid_spec,
        out_shape=[jax.ShapeDtypeStruct((t, c), BF16)] + res_shapes,
        compiler_params=_cparams(2), name="inproj",
    )(flags, x2d, g1, w, gains, ones_bd)


def _attn_kernel(slope_ref, q_ref, kp_ref, kc_ref, vp_ref, vc_ref, *rest,
                 n_pairs, kv_shared, max_diff, dist_scale, with_sink):
    if with_sink:
        sink_ref, o_ref = rest
        lse_ref = None
    else:
        o_ref, lse_ref = rest
    blk = pl.program_id(2)
    cg = pl.program_id(3)

    qi = lax.broadcasted_iota(jnp.int32, (ATTN_BLOCK, 2 * ATTN_BLOCK), 0)
    kj = lax.broadcasted_iota(jnp.int32, (ATTN_BLOCK, 2 * ATTN_BLOCK), 1)
    dist = qi + ATTN_BLOCK - kj
    valid = (dist >= 0) & (dist <= max_diff) & ((blk > 0) | (kj >= ATTN_BLOCK))
    dist_m = jnp.where(valid, (dist * dist_scale).astype(F32), MASK_DIST)

    lane = lax.broadcasted_iota(jnp.int32, (ATTN_BLOCK, LANES), 1)
    low = lane < HEAD_DIM
    if not with_sink:
        @pl.when(cg == 0)
        def _():
            lse_ref[...] = jnp.zeros_like(lse_ref)
        lse_tile = lse_ref[...]

    for p in range(n_pairs):
        cols = slice(p * LANES, (p + 1) * LANES)
        kv = p // PAIRS_PER_KV
        kcols = slice(kv * LANES, (kv + 1) * LANES) if kv_shared else cols
        q2 = q_ref[:, cols]
        kk = jnp.concatenate([kp_ref[:, kcols], kc_ref[:, kcols]], axis=0)
        vv = jnp.concatenate([vp_ref[:, kcols], vc_ref[:, kcols]], axis=0)
        outs, lses = [], []
        for hh in range(2):
            slope = slope_ref[cg * (2 * n_pairs) + 2 * p + hh]
            qm = jnp.where(low if hh == 0 else ~low, q2, jnp.zeros_like(q2))
            s = lax.dot_general(qm, kk, (((1,), (1,)), ((), ())),
                                preferred_element_type=F32)
            s = s - slope * dist_m
            m = jnp.max(s, axis=-1, keepdims=True)
            e = jnp.exp2(s - m)
            l = jnp.sum(e, axis=-1, keepdims=True)
            o = jnp.dot(e.astype(BF16), vv, preferred_element_type=F32)
            outs.append(o / l)
            lse = (m + jnp.log2(l)) * LN2
            if with_sink:
                lses.append(jnp.broadcast_to(lse, (ATTN_BLOCK, LANES)))
            else:
                head = cg * (2 * n_pairs) + 2 * p + hh
                lse_tile = jnp.where(lane == head, lse, lse_tile)
        o2 = jnp.where(low, outs[0], outs[1])
        if with_sink:
            lse2 = jnp.where(low, lses[0], lses[1])
            o2 = o2 * jax.nn.sigmoid(lse2 - sink_ref[:, cols])
        o_ref[:, cols] = o2.astype(o_ref.dtype)
    if not with_sink:
        lse_ref[...] = lse_tile


def _band_attention(proj, slopes, *, batch, seq, dilation, q_col0, k_col0, v_col0, n_heads,
                    kv_shared, max_diff, sinks=None):
    d = dilation
    assert proj.shape[0] == d
    lsub = seq // d
    nblk = lsub // ATTN_BLOCK
    width = n_heads * HEAD_DIM

    def kv_width(w):
        return LANES * (w // LANES // PAIRS_PER_KV) if kv_shared else w

    w = next(c for c in ATTN_W_CHOICES
             if width % c == 0 and q_col0 % c == 0 and k_col0 % kv_width(c) == 0 and v_col0 % kv_width(c) == 0)
    n_pairs = w // LANES
    ncg = width // w
    kw = kv_width(w)
    pv = proj

    def qmap(b, r, i, g, s):
        return (r, b * nblk + i, q_col0 // w + g)

    def kvmap(col0, prev):
        def f(b, r, i, g, s):
            blk = jnp.maximum(i - 1, 0) if prev else i
            return (r, b * nblk + blk, col0 // kw + g)
        return f

    def omap(b, r, i, g, s):
        return (r, b * nblk + i, g)

    in_specs = [
        pl.BlockSpec((None, ATTN_BLOCK, w), qmap),
        pl.BlockSpec((None, ATTN_BLOCK, kw), kvmap(k_col0, True)),
        pl.BlockSpec((None, ATTN_BLOCK, kw), kvmap(k_col0, False)),
        pl.BlockSpec((None, ATTN_BLOCK, kw), kvmap(v_col0, True)),
        pl.BlockSpec((None, ATTN_BLOCK, kw), kvmap(v_col0, False)),
    ]
    args = [pv, pv, pv, pv, pv]
    with_sink = sinks is not None
    o_shape = jax.ShapeDtypeStruct((d, batch * lsub, width), BF16)
    o_spec = pl.BlockSpec((None, ATTN_BLOCK, w), omap)
    if with_sink:
        in_specs.append(pl.BlockSpec((1, w), lambda b, r, i, g, s: (0, g)))
        args.append(sinks)
        out_shape, out_specs = o_shape, o_spec
    else:
        assert n_heads <= LANES
        out_shape = (o_shape, jax.ShapeDtypeStruct((d, batch * lsub, LANES), F32))
        out_specs = (o_spec, pl.BlockSpec((None, ATTN_BLOCK, LANES),
                                          lambda b, r, i, g, s: (r, b * nblk + i, 0)))
    kern = functools.partial(_attn_kernel, n_pairs=n_pairs, kv_shared=kv_shared,
                             max_diff=max_diff, dist_scale=d, with_sink=with_sink)
    grid_spec = pltpu.PrefetchScalarGridSpec(
        num_scalar_prefetch=1, grid=(batch, d, nblk, ncg),
        in_specs=in_specs, out_specs=out_specs)
    res = pl.pallas_call(kern, grid_spec=grid_spec, out_shape=out_shape,
                         compiler_params=_cparams(4), name=f"band_attn_d{d}" + ("_sink" if with_sink else ""),
                         )(slopes, *args)
    return res


def _merge_kernel(*refs, dilations):
    n = len(dilations)
    o_refs, l_refs = refs[:n], refs[n:2 * n]
    spread_ref, out, scr, lscr = refs[2 * n:]
    tm, w = out.shape
    nch = w // LANES
    for a, d in enumerate(dilations):
        if d == 1:
            continue
        for r in range(d):
            lscr[a, pl.ds(r, tm // d, stride=d), :] = l_refs[a][r]
            for ch in range(nch):
                scr[a, ch, pl.ds(r, tm // d, stride=d), :] = (
                    o_refs[a][r, :, ch * LANES:(ch + 1) * LANES].astype(F32))
    ls_ = [l_refs[a][0] if d == 1 else lscr[a] for a, d in enumerate(dilations)]
    m = functools.reduce(jnp.maximum, ls_)
    es = [jnp.exp(l - m) for l in ls_]
    den = functools.reduce(lambda p, q: p + q, es)
    parts = []
    for e in es:
        wgt = e / den
        hi = wgt.astype(BF16)
        parts.append((hi, (wgt - hi.astype(F32)).astype(BF16)))
    span = 2 * LANES
    for c0 in range(0, w, span):
        sel = spread_ref[:, c0:c0 + span]
        nums = [None] * (span // LANES)
        for a, d in enumerate(dilations):
            hi, lo = parts[a]
            wide = jnp.dot(hi, sel, preferred_element_type=F32) + jnp.dot(lo, sel, preferred_element_type=F32)
            for k in range(span // LANES):
                ch = c0 // LANES + k
                o = o_refs[a][0, :, ch * LANES:(ch + 1) * LANES].astype(F32) if d == 1 else scr[a, ch]
                term = wide[:, k * LANES:(k + 1) * LANES] * o
                nums[k] = term if nums[k] is None else nums[k] + term
        for k in range(span // LANES):
            ch = c0 // LANES + k
            out[:, ch * LANES:(ch + 1) * LANES] = nums[k].astype(out.dtype)


def _merge(os_, ls_, dilations, tm=256):
    w = os_[0].shape[2]
    t = os_[0].shape[0] * os_[0].shape[1]
    o_specs = [pl.BlockSpec((d, tm // d, w), lambda i: (0, i, 0)) for d in dilations]
    l_specs = [pl.BlockSpec((d, tm // d, LANES), lambda i: (0, i, 0)) for d in dilations]
    spread = jnp.asarray(np.arange(LANES)[:, None] == np.arange(w)[None, :] // HEAD_DIM, dtype=BF16)
    return pl.pallas_call(
        functools.partial(_merge_kernel, dilations=tuple(dilations)), grid=(t // tm,),
        in_specs=o_specs + l_specs + [pl.BlockSpec((LANES, w), lambda i: (0, 0))],
        out_specs=pl.BlockSpec((tm, w), lambda i: (i, 0)),
        out_shape=jax.ShapeDtypeStruct((t, w), BF16),
        scratch_shapes=[pltpu.VMEM((len(dilations), w // LANES, tm, LANES), F32),
                        pltpu.VMEM((len(dilations), tm, LANES), F32)],
        compiler_params=_cparams(1), name="branch_merge",
    )(*os_, *ls_, spread)


def _outproj_kernel(oa_ref, ob_ref, wt_ref, wb_ref, x_ref, o_ref):
    acc = jnp.dot(oa_ref[...], wt_ref[...], preferred_element_type=F32)
    acc = acc + jnp.dot(ob_ref[...], wb_ref[...], preferred_element_type=F32)
    o_ref[...] = x_ref[...] + acc


def _outproj(o_a, o_b, w_out, x2d, tm, tn):
    t, d = x2d.shape
    ha, hb = o_a.shape[1], o_b.shape[1]
    assert ha == hb
    return pl.pallas_call(
        _outproj_kernel, grid=(t // tm, d // tn),
        in_specs=[
            pl.BlockSpec((tm, ha), lambda i, j: (i, 0)),
            pl.BlockSpec((tm, hb), lambda i, j: (i, 0)),
            pl.BlockSpec((ha, tn), lambda i, j: (0, j)),
            pl.BlockSpec((hb, tn), lambda i, j: (1, j)),
            pl.BlockSpec((tm, tn), lambda i, j: (i, j)),
        ],
        out_specs=pl.BlockSpec((tm, tn), lambda i, j: (i, j)),
        out_shape=jax.ShapeDtypeStruct((t, d), F32), compiler_params=_cparams(2), name="outproj",
    )(o_a, o_b, w_out, w_out, x2d)


def _router_kernel(x_ref, g_ref, whi_ref, wlo_ref, b_ref, tri_ref,
                   h_ref, idx_ref, gate_ref, rank_ref, cnt_ref, carry, h_scr):
    i = pl.program_id(0)

    @pl.when(i == 0)
    def _():
        carry[...] = jnp.zeros_like(carry)

    x = x_ref[...]
    ms = jnp.mean(x * x, axis=-1, keepdims=True)
    h = x * lax.rsqrt(ms + EPS) * g_ref[...]
    h_hi = h.astype(BF16)
    h_lo = (h - h_hi.astype(F32)).astype(BF16)
    u = pltpu.bitcast(h, jnp.uint32)
    r = (u + jnp.uint32(0x7FFF) + ((u >> 16) & jnp.uint32(1))) >> 16
    tok, words, _ = h_ref.shape
    for s in range(words):
        lo = r[:, 2 * s * LANES:(2 * s + 1) * LANES]
        hi = r[:, (2 * s + 1) * LANES:(2 * s + 2) * LANES]
        h_scr[pl.ds(s, tok, stride=words), :] = lo | (hi << 16)
    h_ref[...] = h_scr[...].reshape(tok, words, LANES)
    logits = (jnp.dot(h_hi, whi_ref[...], preferred_element_type=F32)
              + jnp.dot(h_lo, whi_ref[...], preferred_element_type=F32)
              + jnp.dot(h_hi, wlo_ref[...], preferred_element_type=F32)) + b_ref[...]

    tm = x.shape[0]
    lane = lax.broadcasted_iota(jnp.int32, (tm, LANES), 1).astype(F32)
    work = logits
    multihot = jnp.zeros((tm, LANES), F32)
    vals, idxs = [], []
    for _ in range(TOP_K):
        m = jnp.max(work, axis=-1, keepdims=True)
        ik = jnp.min(jnp.where(work == m, lane, float(LANES)), axis=-1, keepdims=True)
        sel = lane == ik
        work = jnp.where(sel, -jnp.inf, work)
        multihot = jnp.where(sel, 1.0, multihot)
        vals.append(m)
        idxs.append(ik)
    es = [jnp.exp(v - vals[0]) for v in vals]
    denom = es[0] + es[1] + es[2] + es[3]

    cum = jnp.dot(tri_ref[...], multihot.astype(BF16), preferred_element_type=F32) + carry[0:1, :]
    idx_t = jnp.zeros((tm, LANES), F32)
    gate_t = jnp.zeros((tm, LANES), F32)
    rank_t = jnp.zeros((tm, LANES), F32)
    for k in range(TOP_K):
        rk = jnp.sum(jnp.where(lane == idxs[k], cum, 0.0), axis=-1, keepdims=True)
        here = lane == float(k)
        idx_t = jnp.where(here, idxs[k], idx_t)
        gate_t = jnp.where(here, es[k] / denom, gate_t)
        rank_t = jnp.where(here, rk, rank_t)
    idx_ref[...] = idx_t.astype(jnp.int32)
    gate_ref[...] = gate_t
    rank_ref[...] = rank_t.astype(jnp.int32)
    new_carry = carry[0:1, :] + jnp.sum(multihot, axis=0, keepdims=True)
    carry[...] = jnp.broadcast_to(new_carry, carry.shape)
    cnt_ref[...] = jnp.broadcast_to(new_carry, cnt_ref.shape)


def _router(x2d, g2, w_router, b_router, tm=256):
    t, d = x2d.shape
    ne = w_router.shape[1]
    w_pad = jnp.zeros((d, LANES), F32).at[:, :ne].set(w_router)
    w_hi = w_pad.astype(BF16)
    w_lo = (w_pad - w_hi.astype(F32)).astype(BF16)
    b_pad = jnp.full((1, LANES), NEG_BIG, F32).at[0, :ne].set(b_router)
    tri = jnp.asarray(np.tril(np.ones((tm, tm), np.float32), -1), dtype=BF16)
    tile = lambda dt: jax.ShapeDtypeStruct((t, LANES), dt)
    row = pl.BlockSpec((tm, LANES), lambda i: (i, 0))
    const = lambda shape: pl.BlockSpec(shape, lambda i: (0, 0))
    return pl.pallas_call(
        _router_kernel, grid=(t // tm,),
        in_specs=[pl.BlockSpec((tm, d), lambda i: (i, 0)), const((1, d)), const((d, LANES)),
                  const((d, LANES)), const((1, LANES)), const((tm, tm))],
        out_specs=(pl.BlockSpec((tm, d // (2 * LANES), LANES), lambda i: (i, 0, 0)), row, row, row,
                   const((8, LANES))),
        out_shape=(jax.ShapeDtypeStruct((t, d // (2 * LANES), LANES), jnp.uint32),
                   tile(jnp.int32), tile(F32), tile(jnp.int32),
                   jax.ShapeDtypeStruct((8, LANES), F32)),
        scratch_shapes=[pltpu.VMEM((8, LANES), F32), pltpu.VMEM((tm * (d // (2 * LANES)), LANES), jnp.uint32)],
        compiler_params=_cparams(1), name="router",
    )(x2d, g2, w_hi, w_lo, b_pad, tri)


INDEX_SLOTS = 3
ROW_SLOTS = 2


def _row_prefetch_ring(idx_hbm, idx_smem, isem, start_rows, wait_rows):
    i = pl.program_id(0)
    n = pl.num_programs(0)

    def idx_copy(step):
        slot = lax.rem(step, INDEX_SLOTS)
        return pltpu.make_async_copy(idx_hbm.at[step], idx_smem.at[slot], isem.at[slot])

    @pl.when(i == 0)
    def _():
        idx_copy(0).start()
        idx_copy(0).wait()
        start_rows(0, 0, 0)

        @pl.when(n > 1)
        def _():
            idx_copy(1).start()

    @pl.when(i + 1 < n)
    def _():
        idx_copy(i + 1).wait()
        start_rows(i + 1, lax.rem(i + 1, INDEX_SLOTS), lax.rem(i + 1, ROW_SLOTS))

    @pl.when(i + 2 < n)
    def _():
        idx_copy(i + 2).start()

    row_slot = lax.rem(i, ROW_SLOTS)
    wait_rows(row_slot)
    return row_slot


def _gather_kernel(nvalid_ref, tok_hbm, h_hbm, xs_ref, tok_smem, buf, isem, rsem):
    g = GATHER_ROWS
    words = h_hbm.shape[1]

    @pl.when(pl.program_id(0) == 0)
    def _():
        buf[...] = jnp.zeros_like(buf)

    def row_copy(t, r, row_slot):
        dst = buf.at[row_slot, pl.ds(pl.multiple_of(r * words, words), words)]
        return pltpu.make_async_copy(h_hbm.at[t], dst, rsem.at[row_slot])

    def start_rows(step, idx_slot, row_slot):
        def body(r2, c):
            for prio in range(2):
                r = 2 * r2 + prio
                t = tok_smem[idx_slot, r]

                @pl.when(t >= 0)
                def _():
                    row_copy(t, r, row_slot).start(priority=prio)
            return c
        lax.fori_loop(0, g // 2, body, 0)

    def wait_rows(row_slot):
        nvalid = nvalid_ref[pl.program_id(0)]

        @pl.when(nvalid == g)
        def _():
            pltpu.make_async_copy(buf.at[row_slot], buf.at[row_slot], rsem.at[row_slot]).wait()

        @pl.when(nvalid < g)
        def _():
            def body(r, c):
                row_copy(0, 0, row_slot).wait()
                return c
            lax.fori_loop(0, nvalid, body, 0)

    row_slot = _row_prefetch_ring(tok_hbm, tok_smem, isem, start_rows, wait_rows)

    for s in range(words):
        x = buf[row_slot, pl.ds(s, g, stride=words), :]
        lo = pltpu.bitcast(x << 16, F32)
        hi = pltpu.bitcast(x & jnp.uint32(0xFFFF0000), F32)
        xs_ref[:, 2 * s * LANES:(2 * s + 1) * LANES] = lo.astype(BF16)
        xs_ref[:, (2 * s + 1) * LANES:(2 * s + 2) * LANES] = hi.astype(BF16)


def _gather_rows(h_packed, row_tok):
    t, words, _ = h_packed.shape
    r_max = row_tok.shape[0]
    g = GATHER_ROWS
    n_steps = r_max // g
    tok2d = row_tok.reshape(n_steps, g)
    nvalid = jnp.sum((tok2d >= 0).astype(jnp.int32), axis=1)
    grid_spec = pltpu.PrefetchScalarGridSpec(
        num_scalar_prefetch=1, grid=(n_steps,),
        in_specs=[pl.BlockSpec(memory_space=pl.ANY), pl.BlockSpec(memory_space=pl.ANY)],
        out_specs=pl.BlockSpec((g, 2 * words * LANES), lambda i, nv: (i, 0)),
        scratch_shapes=[pltpu.SMEM((INDEX_SLOTS, g), jnp.int32),
                        pltpu.VMEM((ROW_SLOTS, g * words, LANES), jnp.uint32),
                        pltpu.SemaphoreType.DMA((INDEX_SLOTS,)), pltpu.SemaphoreType.DMA((ROW_SLOTS,))])
    return pl.pallas_call(
        _gather_kernel, grid_spec=grid_spec,
        out_shape=jax.ShapeDtypeStruct((r_max, 2 * words * LANES), BF16),
        compiler_params=_cparams(1), name="row_gather",
    )(nvalid, tok2d, h_packed)


PREP_ROWS = 1024


def _tile_clamp(i, na):
    return jnp.minimum(i, na[0] - 1)


def _weights_changed(te_ref, na_ref, i):
    cur = _tile_clamp(i, na_ref)
    prev = jnp.maximum(cur - 1, 0)
    return (i < na_ref[0]) & ((i == 0) | (te_ref[cur] != te_ref[prev]))


def _stream_expert_weights(w_hbm, raw, wsem, te_ref, meta_ref, nxt_ref, eidx_ref, convert):
    j = pl.program_id(0)
    i = pl.program_id(1)
    nj = pl.num_programs(0)
    tc = raw.shape[2]
    cur = _tile_clamp(i, meta_ref)

    def copy(e, jj, slot):
        col = pl.multiple_of(jj * tc, tc)
        return pltpu.make_async_copy(w_hbm.at[e, :, pl.ds(col, tc)], raw.at[slot], wsem.at[slot])

    @pl.when(_weights_changed(te_ref, meta_ref, i))
    def _():
        e = te_ref[cur]
        b = j * meta_ref[1] + eidx_ref[cur]
        slot = lax.rem(b, 2)
        nxt = nxt_ref[cur]

        @pl.when(b == 0)
        def _():
            copy(e, j, 0).start()

        @pl.when(nxt >= 0)
        def _():
            copy(nxt, j, 1 - slot).start()

        @pl.when((nxt < 0) & (j + 1 < nj))
        def _():
            copy(meta_ref[2], j + 1, 1 - slot).start()

        copy(e, j, slot).wait()
        convert(raw.at[slot])


def _row_cases(na_ref, tv_ref, i, tm, compute, zero):
    active = i < na_ref[0]
    valid = tv_ref[_tile_clamp(i, na_ref)]

    @pl.when(active & (valid > tm // 2))
    def _():
        compute(tm)

    @pl.when(active & (valid <= tm // 2))
    def _():
        compute(tm // 2)

    @pl.when(jnp.logical_not(active))
    def _():
        zero()


def _cast_rows(dst, src):
    n = dst.shape[0]
    for r0 in range(0, n, PREP_ROWS):
        rows = slice(r0, min(r0 + PREP_ROWS, n))
        dst[rows, :] = src[rows, :].astype(dst.dtype)


def _gateup_kernel(te_ref, na_ref, tv_ref, nxt_ref, eidx_ref, xs_ref, w_hbm, b_ref, pe_ref, h_ref,
                   w_s, raw, wsem):
    i = pl.program_id(1)
    d, tc = w_s.shape
    tm = h_ref.shape[0]
    _stream_expert_weights(w_hbm, raw, wsem, te_ref, na_ref, nxt_ref, eidx_ref,
                           functools.partial(_cast_rows, w_s))

    def compute(nrows):
        gu = jnp.dot(xs_ref[0:nrows, :], w_s[...], preferred_element_type=F32) + b_ref[...]
        acts = []
        for c in range(tc // LANES):
            g = gu[:, c * LANES:(c + 1) * LANES]
            u = pltpu.roll(g, LANES - 1, axis=1)
            gate = jnp.minimum(g, SWIGLU_LIMIT)
            up = jnp.clip(u, -SWIGLU_LIMIT, SWIGLU_LIMIT)
            acts.append(((up + 1.0) * gate * jax.nn.sigmoid(SWIGLU_ALPHA * gate)).astype(BF16))
        act = jnp.concatenate(acts, axis=1)
        h_ref[0:nrows, :] = jnp.dot(act, pe_ref[...], preferred_element_type=F32).astype(h_ref.dtype)
        if nrows < tm:
            h_ref[nrows:tm, :] = jnp.zeros((tm - nrows, h_ref.shape[1]), h_ref.dtype)

    def zero():
        h_ref[...] = jnp.zeros_like(h_ref)

    _row_cases(na_ref, tv_ref, i, tm, compute, zero)


def _down_kernel(te_ref, na_ref, tv_ref, nxt_ref, eidx_ref, h_ref, w_hbm, bd_ref, rw_ref, o_ref,
                 wd_s, o_scr, raw, wsem):
    i = pl.program_id(1)
    tm, chunks, _ = o_ref.shape
    _stream_expert_weights(w_hbm, raw, wsem, te_ref, na_ref, nxt_ref, eidx_ref,
                           functools.partial(_cast_rows, wd_s))

    def compute(nrows):
        out = jnp.dot(h_ref[0:nrows, :], wd_s[...], preferred_element_type=F32)
        out = (out + bd_ref[...]) * rw_ref[0:nrows, :]
        for c in range(chunks):
            o_scr[pl.ds(c, nrows, stride=chunks), :] = out[:, c * LANES:(c + 1) * LANES]
        if nrows < tm:
            o_scr[nrows * chunks:tm * chunks, :] = jnp.zeros(((tm - nrows) * chunks, LANES), F32)
        o_ref[...] = o_scr[...].reshape(tm, chunks, LANES)

    def zero():
        o_ref[...] = jnp.zeros_like(o_ref)

    _row_cases(na_ref, tv_ref, i, tm, compute, zero)


def _experts(xs, row_w, plan, w_gate_up, b_gate_up, w_down, bd, tn_ff, tn_d):
    r_max, d = xs.shape
    ff = w_down.shape[1]
    tm = EXPERT_TM
    n_tiles = r_max // tm
    tc = 2 * tn_ff
    p_even = jnp.asarray(np.arange(tc)[:, None] == 2 * np.arange(tn_ff)[None, :], dtype=BF16)
    n_plan = len(plan)
    hbm = pl.BlockSpec(memory_space=pl.ANY)
    weight_stream = lambda rows, cols: [pltpu.VMEM((2, rows, cols), F32), pltpu.SemaphoreType.DMA((2,))]

    def row_tile(j, i, te, meta, *_):
        return (_tile_clamp(i, meta), 0)

    def expert_cols(j, i, te, meta, *_):
        return (te[_tile_clamp(i, meta)], 0, j)

    gu_spec = pltpu.PrefetchScalarGridSpec(
        num_scalar_prefetch=n_plan, grid=(ff // tn_ff, n_tiles),
        in_specs=[
            pl.BlockSpec((tm, d), row_tile),
            hbm,
            pl.BlockSpec((None, 1, tc), expert_cols),
            pl.BlockSpec((tc, tn_ff), lambda j, i, *_: (0, 0)),
        ],
        out_specs=pl.BlockSpec((tm, tn_ff), lambda j, i, *_: (i, j)),
        scratch_shapes=[pltpu.VMEM((d, tc), BF16)] + weight_stream(d, tc))
    hidden = pl.pallas_call(
        _gateup_kernel, grid_spec=gu_spec, out_shape=jax.ShapeDtypeStruct((r_max, ff), BF16),
        compiler_params=_cparams(2), name="expert_gate_up",
    )(*plan, xs, w_gate_up, b_gate_up, p_even)
    dn_spec = pltpu.PrefetchScalarGridSpec(
        num_scalar_prefetch=n_plan, grid=(d // tn_d, n_tiles),
        in_specs=[
            pl.BlockSpec((tm, ff), row_tile),
            hbm,
            pl.BlockSpec((None, 1, tn_d), expert_cols),
            pl.BlockSpec((tm, 1), row_tile),
        ],
        out_specs=pl.BlockSpec((tm, tn_d // LANES, LANES), lambda j, i, *_: (i, j, 0)),
        scratch_shapes=[pltpu.VMEM((ff, tn_d), BF16), pltpu.VMEM((tm * (tn_d // LANES), LANES), F32)]
        + weight_stream(ff, tn_d))
    return pl.pallas_call(
        _down_kernel, grid_spec=dn_spec, out_shape=jax.ShapeDtypeStruct((r_max, d // LANES, LANES), F32),
        compiler_params=_cparams(2), name="expert_down",
    )(*plan, hidden, w_down, bd, row_w)


def _combine_kernel(dest_hbm, rows_hbm, x_ref, o_ref, dest_smem, buf, sum_scr, isem, rsem):
    tm = COMBINE_TM
    chunks = rows_hbm.shape[1]
    pitch = sum_scr.shape[0] // tm
    seg = tm * pitch

    @pl.when(pl.program_id(0) == 0)
    def _():
        buf[...] = jnp.zeros_like(buf)

    def start_rows(step, idx_slot, row_slot):
        def body(r, c):
            for k in range(TOP_K):
                src = dest_smem[idx_slot, r * TOP_K + k]
                off = pl.multiple_of((k * tm + r) * pitch, math.gcd(pitch, SUBLANES))
                pltpu.make_async_copy(rows_hbm.at[src], buf.at[row_slot, pl.ds(off, chunks)],
                                      rsem.at[row_slot]).start()
            return c
        lax.fori_loop(0, tm, body, 0, unroll=2)

    def wait_rows(row_slot):
        landed = buf.at[row_slot, pl.ds(0, TOP_K * tm * chunks)]
        pltpu.make_async_copy(landed, landed, rsem.at[row_slot]).wait()

    row_slot = _row_prefetch_ring(dest_hbm, dest_smem, isem, start_rows, wait_rows)

    acc = buf[row_slot, pl.ds(0, seg), :]
    for k in range(1, TOP_K):
        acc = acc + buf[row_slot, pl.ds(k * seg, seg), :]
    sum_scr[...] = acc
    for c in range(chunks):
        cols = slice(c * LANES, (c + 1) * LANES)
        o_ref[:, cols] = x_ref[:, cols] + sum_scr[pl.ds(c, tm, stride=pitch), :]


def _combine(rows, dest, x2d):
    t, d = x2d.shape
    chunks = rows.shape[1]
    tm = COMBINE_TM
    n_steps = t // tm
    dest2d = dest.reshape(n_steps, tm * TOP_K)
    blk = pl.BlockSpec((tm, d), lambda i: (i, 0))
    pitch = chunks + SUBLANES if chunks % (2 * SUBLANES) == 0 else chunks
    return pl.pallas_call(
        _combine_kernel, grid=(n_steps,),
        in_specs=[pl.BlockSpec(memory_space=pl.ANY), pl.BlockSpec(memory_space=pl.ANY), blk],
        out_specs=blk,
        out_shape=jax.ShapeDtypeStruct((t, d), F32),
        scratch_shapes=[pltpu.SMEM((INDEX_SLOTS, tm * TOP_K), jnp.int32),
                        pltpu.VMEM((ROW_SLOTS, TOP_K * tm * pitch, LANES), F32),
                        pltpu.VMEM((tm * pitch, LANES), F32),
                        pltpu.SemaphoreType.DMA((INDEX_SLOTS,)), pltpu.SemaphoreType.DMA((ROW_SLOTS,))],
        compiler_params=_cparams(1), name="combine",
    )(dest2d, rows, x2d)


def _moe(x_mid, g2, w_router, b_router, w_gate_up, b_gate_up, w_down, b_down, tn_ff, tn_d):
    t, d = x_mid.shape
    ne = w_router.shape[1]
    h2, idx_t, gate_t, rank_t, cnt = _router(x_mid, g2, w_router, b_router)

    tm = EXPERT_TM
    r_max = t * TOP_K + ne * tm
    n_tiles = r_max // tm
    counts = cnt[0, :ne].astype(jnp.int32)
    padded = (counts + tm - 1) // tm * tm
    pend = jnp.cumsum(padded)
    pstart = pend - padded
    idx = idx_t[:, :TOP_K]
    dest = (pstart[idx] + rank_t[:, :TOP_K]).astype(jnp.int32)
    n_rows_used = pend[-1:].astype(jnp.int32)
    n_active = n_rows_used // tm
    tok = jnp.repeat(jnp.arange(t, dtype=jnp.int32), TOP_K)
    gate_bits = lax.bitcast_convert_type(gate_t[:, :TOP_K].reshape(-1), jnp.int32)
    empty = jnp.broadcast_to(jnp.asarray([-1, 0], jnp.int32), (r_max, 2))
    table = empty.at[dest.reshape(-1)].set(
        jnp.stack([tok, gate_bits], axis=1), unique_indices=True)
    row_tok = table[:, 0]
    row_w = lax.bitcast_convert_type(table[:, 1], F32)[:, None]
    tile_start = jnp.arange(n_tiles, dtype=jnp.int32) * tm
    tile_e = jnp.minimum(jnp.sum((pend[None, :] <= tile_start[:, None]).astype(jnp.int32), axis=1), ne - 1)
    tile_valid = jnp.clip((pstart + counts)[tile_e] - tile_start, 0, tm).astype(jnp.int32)
    present = counts > 0
    eids = jnp.arange(ne, dtype=jnp.int32)
    later = present[None, :] & (eids[None, :] > eids[:, None])
    next_e = jnp.min(jnp.where(later, eids[None, :], ne), axis=1)
    next_e = jnp.where(next_e == ne, -1, next_e).astype(jnp.int32)
    rank_e = (jnp.cumsum(present.astype(jnp.int32)) - 1).astype(jnp.int32)
    meta = jnp.stack([n_active[0], jnp.sum(present.astype(jnp.int32)),
                      jnp.argmax(present).astype(jnp.int32)]).astype(jnp.int32)
    plan = (tile_e, meta, tile_valid, next_e[tile_e], rank_e[tile_e])

    xs = _gather_rows(h2, row_tok)

    rows = _experts(xs, row_w, plan, w_gate_up, b_gate_up[:, None, :], w_down, b_down[:, None, :], tn_ff, tn_d)
    return _combine(rows, dest, x_mid)


def _alibi_slopes(n):
    return (LOG2E * 2.0 ** (-ALIBI_MAX_BIAS * (np.arange(n, dtype=np.float64) + 1.0) / n)).astype(np.float32)


def _mixer(x2d, batch, seq, norm1_g, w_in, q_norm_swa, k_norm_swa, q_norm_dil, k_norm_dil, sinks, w_out,
           n_swa, n_kv, n_dil, tm_proj, tn_out):
    t, d = x2d.shape
    swa_q, swa_kv, dil_w = n_swa * HEAD_DIM, n_kv * HEAD_DIM, n_dil * HEAD_DIM
    assert n_swa // n_kv == GQA_GROUP and seq % (16 * ATTN_BLOCK) == 0
    s1, s2, s3 = swa_q, swa_q + swa_kv, swa_q + 2 * swa_kv

    def dup(wc):
        wc = wc.reshape(d, n_kv, HEAD_DIM)
        return jnp.concatenate([wc, wc], axis=-1).reshape(d, 2 * swa_kv)

    w = jnp.concatenate([w_in[:, :s1], dup(w_in[:, s1:s2]), dup(w_in[:, s2:s3]), w_in[:, s3:]],
                        axis=1).astype(BF16)
    scale = HEAD_DIM ** -0.5 * LOG2E
    ones = lambda n: jnp.ones((n,), F32)
    gains = jnp.concatenate([
        jnp.tile(q_norm_swa * scale, n_swa), jnp.tile(k_norm_swa, 2 * n_kv), ones(2 * swa_kv),
        jnp.tile(q_norm_dil * scale, n_dil), jnp.tile(k_norm_dil, n_dil), ones(dil_w)])[None, :]
    c = w.shape[1]
    bounds = np.cumsum([0, swa_q, 2 * swa_kv, 2 * swa_kv, dil_w, dil_w, dil_w])
    assert all(b % PROJ_TN == 0 for b in bounds)
    seg_norm = [1, 1, 0, 1, 1, 0]
    flags = np.zeros((c // PROJ_TN,), np.int32)
    for sidx in range(6):
        flags[bounds[sidx] // PROJ_TN:bounds[sidx + 1] // PROJ_TN] = seg_norm[sidx]
    qa0, ka0, va0, qb0, kb0, vb0 = (int(b) for b in bounds[:6])
    proj, *residue_major = _inproj(x2d, norm1_g[None, :], w, gains, jnp.asarray(flags), tm_proj, dil_col0=qb0)
    by_dilation = dict(zip(RESIDUE_DILATIONS, residue_major))

    sink_row = jnp.repeat(sinks.astype(F32), HEAD_DIM)[None, :]
    o_a = _band_attention(proj[None], jnp.asarray(_alibi_slopes(n_swa)), batch=batch, seq=seq, dilation=1,
                          q_col0=qa0, k_col0=ka0, v_col0=va0, n_heads=n_swa, kv_shared=True,
                          max_diff=SWA_WINDOW - 1, sinks=sink_row)[0]
    slopes_dil = jnp.asarray(_alibi_slopes(n_dil))
    outs, lses = [], []
    for window, dil in DILATED_BRANCHES:
        src, col0 = (proj[None], qb0) if dil == 1 else (by_dilation[dil], 0)
        o_i, lse_i = _band_attention(src, slopes_dil, batch=batch, seq=seq, dilation=dil,
                                     q_col0=col0, k_col0=col0 + kb0 - qb0, v_col0=col0 + vb0 - qb0,
                                     n_heads=n_dil, kv_shared=False, max_diff=window // dil)
        outs.append(o_i)
        lses.append(lse_i)
    o_b = _merge(outs, lses, [dil for _, dil in DILATED_BRANCHES])
    return _outproj(o_a, o_b, w_out.astype(BF16), x2d, tm_proj, tn_out)


def kernel(x, norm1_g, w_in, q_norm_swa, k_norm_swa, q_norm_dil, k_norm_dil, sinks, w_out, norm2_g,
           w_router, b_router, w_gate_up, b_gate_up, w_down, b_down):
    b, s, d = x.shape
    depth = norm1_g.shape[0]
    n_heads = d // HEAD_DIM
    n_swa = n_heads // 2
    n_kv = n_swa // GQA_GROUP
    n_dil = n_heads - n_swa
    x2d = x.reshape(b * s, d)
    for l in range(depth):
        x_mid = _mixer(x2d, b, s, norm1_g[l], w_in[l], q_norm_swa[l], k_norm_swa[l], q_norm_dil[l],
                       k_norm_dil[l], sinks[l], w_out[l], n_swa, n_kv, n_dil, tm_proj=512, tn_out=1024)
        x2d = _moe(x_mid, norm2_g[l][None, :], w_router[l], b_router[l], w_gate_up[l], b_gate_up[l],
                   w_down[l], b_down[l], tn_ff=512, tn_d=1024)
    return x2d.reshape(b, s, d)
```

```python
import functools
import math

import jax
import jax.numpy as jnp
import numpy as np
from jax import lax
from jax.experimental import pallas as pl
from jax.experimental.pallas import tpu as pltpu

F32 = jnp.float32
BF16 = jnp.bfloat16

HEAD_DIM = 64
LANES = 128
SUBLANES = 8
ATTN_BLOCK = 128
GQA_GROUP = 8
SWA_WINDOW = 128
DILATED_BRANCHES = ((128, 1), (512, 4), (2048, 16))
N_EXPERTS = 32
TOP_K = 4
SWIGLU_LIMIT = 7.0
SWIGLU_ALPHA = 1.702
ALIBI_MAX_BIAS = 8.0
EPS = 1e-6
MASK_DIST = 1e30
LOG2E = 1.4426950408889634
LN2 = 0.6931471805599453
NEG_BIG = -1e30

VMEM_LIMIT = 58 * 1024 * 1024

PROJ_TN = 512
ATTN_W_CHOICES = (2048, 1024, 512)
PAIRS_PER_KV = GQA_GROUP // 2
EXPERT_TM = 512
GATHER_ROWS = 1024
GATHER_WAIT_ROWS = 512
COMBINE_TM = 128


def _cparams(n_axes):
    return pltpu.CompilerParams(dimension_semantics=("arbitrary",) * n_axes,
                                vmem_limit_bytes=VMEM_LIMIT)


RESIDUE_DILATIONS = tuple(d for _, d in DILATED_BRANCHES if d > 1)


def _inproj_kernel(flag_ref, x_ref, g_ref, w_ref, gain_ref, ones_ref, o_ref, *rest, dil_tile0):
    res_refs, (h_scr, y_scr) = rest[:len(RESIDUE_DILATIONS)], rest[len(RESIDUE_DILATIONS):]
    j = pl.program_id(1)
    tm, tn = o_ref.shape

    @pl.when(j == 0)
    def _():
        x = x_ref[...]
        ms = jnp.mean(x * x, axis=-1, keepdims=True)
        h_scr[...] = (x * lax.rsqrt(ms + EPS) * g_ref[...]).astype(BF16)

    acc = jnp.dot(h_scr[...], w_ref[...], preferred_element_type=F32)

    def emit(y):
        o_ref[...] = y.astype(BF16)
        for ch in range(tn // LANES):
            y_scr[ch] = y[:, ch * LANES:(ch + 1) * LANES]

    @pl.when(flag_ref[j] == 1)
    def _():
        ss = jnp.dot((acc * acc).astype(BF16), ones_ref[...], preferred_element_type=F32)
        emit(acc * lax.rsqrt(ss * (1.0 / HEAD_DIM) + EPS) * gain_ref[...])

    @pl.when(flag_ref[j] == 0)
    def _():
        emit(acc)

    @pl.when(j >= dil_tile0)
    def _():
        for d, ref in zip(RESIDUE_DILATIONS, res_refs):
            for r in range(d):
                for ch in range(tn // LANES):
                    rows = y_scr[ch, pl.ds(r, tm // d, stride=d), :]
                    ref[r, :, ch * LANES:(ch + 1) * LANES] = rows.astype(BF16)


def _inproj(x2d, g1, w, gains, flags, tm, dil_col0):
    t, d = x2d.shape
    c = w.shape[1]
    tn = PROJ_TN
    dil_tile0 = dil_col0 // tn
    cb = c - dil_col0
    head_id = np.arange(tn) // HEAD_DIM
    ones_bd = jnp.asarray(head_id[:, None] == head_id[None, :], dtype=BF16)
    res_specs = [pl.BlockSpec((dd, tm // dd, tn), lambda i, j, f: (0, i, jnp.maximum(j - dil_tile0, 0)))
                 for dd in RESIDUE_DILATIONS]
    res_shapes = [jax.ShapeDtypeStruct((dd, t // dd, cb), BF16) for dd in RESIDUE_DILATIONS]
    grid_spec = pltpu.PrefetchScalarGridSpec(
        num_scalar_prefetch=1,
        grid=(t // tm, c // tn),
        in_specs=[
            pl.BlockSpec((tm, d), lambda i, j, f: (i, 0)),
            pl.BlockSpec((1, d), lambda i, j, f: (0, 0)),
            pl.BlockSpec((d, tn), lambda i, j, f: (0, j)),
            pl.BlockSpec((1, tn), lambda i, j, f: (0, j)),
            pl.BlockSpec((tn, tn), lambda i, j, f: (0, 0)),
        ],
        out_specs=[pl.BlockSpec((tm, tn), lambda i, j, f: (i, j))] + res_specs,
        scratch_shapes=[pltpu.VMEM((tm, d), BF16), pltpu.VMEM((tn // LANES, tm, LANES), F32)],
    )
    return pl.pallas_call(
        functools.partial(_inproj_kernel, dil_tile0=dil_tile0), grid_spec=grid_spec,
        out_shape=[jax.ShapeDtypeStruct((t, c), BF16)] + res_shapes,
        compiler_params=_cparams(2), name="inproj",
    )(flags, x2d, g1, w, gains, ones_bd)


def _attn_kernel(slope_ref, q_ref, kp_ref, kc_ref, vp_ref, vc_ref, *rest,
                 n_pairs, kv_shared, max_diff, dist_scale, with_sink):
    if with_sink:
        sink_ref, o_ref = rest
        lse_ref = None
    else:
        o_ref, lse_ref = rest
    blk = pl.program_id(2)
    cg = pl.program_id(3)

    qi = lax.broadcasted_iota(jnp.int32, (ATTN_BLOCK, 2 * ATTN_BLOCK), 0)
    kj = lax.broadcasted_iota(jnp.int32, (ATTN_BLOCK, 2 * ATTN_BLOCK), 1)
    dist = qi + ATTN_BLOCK - kj
    valid = (dist >= 0) & (dist <= max_diff) & ((blk > 0) | (kj >= ATTN_BLOCK))
    dist_m = jnp.where(valid, (dist * dist_scale).astype(F32), MASK_DIST)

    lane = lax.broadcasted_iota(jnp.int32, (ATTN_BLOCK, LANES), 1)
    low = lane < HEAD_DIM
    if not with_sink:
        @pl.when(cg == 0)
        def _():
            lse_ref[...] = jnp.zeros_like(lse_ref)
        lse_tile = lse_ref[...]

    for p in range(n_pairs):
        cols = slice(p * LANES, (p + 1) * LANES)
        kv = p // PAIRS_PER_KV
        kcols = slice(kv * LANES, (kv + 1) * LANES) if kv_shared else cols
        q2 = q_ref[:, cols]
        kk = jnp.concatenate([kp_ref[:, kcols], kc_ref[:, kcols]], axis=0)
        vv = jnp.concatenate([vp_ref[:, kcols], vc_ref[:, kcols]], axis=0)
        outs, lses = [], []
        for hh in range(2):
            slope = slope_ref[cg * (2 * n_pairs) + 2 * p + hh]
            qm = jnp.where(low if hh == 0 else ~low, q2, jnp.zeros_like(q2))
            s = lax.dot_general(qm, kk, (((1,), (1,)), ((), ())),
                                preferred_element_type=F32)
            s = s - slope * dist_m
            m = jnp.max(s, axis=-1, keepdims=True)
            e = jnp.exp2(s - m)
            l = jnp.sum(e, axis=-1, keepdims=True)
            o = jnp.dot(e.astype(BF16), vv, preferred_element_type=F32)
            outs.append(o / l)
            lse = (m + jnp.log2(l)) * LN2
            if with_sink:
                lses.append(jnp.broadcast_to(lse, (ATTN_BLOCK, LANES)))
            else:
                head = cg * (2 * n_pairs) + 2 * p + hh
                lse_tile = jnp.where(lane == head, lse, lse_tile)
        o2 = jnp.where(low, outs[0], outs[1])
        if with_sink:
            lse2 = jnp.where(low, lses[0], lses[1])
            o2 = o2 * jax.nn.sigmoid(lse2 - sink_ref[:, cols])
        o_ref[:, cols] = o2.astype(o_ref.dtype)
    if not with_sink:
        lse_ref[...] = lse_tile


def _band_attention(proj, slopes, *, batch, seq, dilation, q_col0, k_col0, v_col0, n_heads,
                    kv_shared, max_diff, sinks=None):
    d = dilation
    assert proj.shape[0] == d
    lsub = seq // d
    nblk = lsub // ATTN_BLOCK
    width = n_heads * HEAD_DIM

    def kv_width(w):
        return LANES * (w // LANES // PAIRS_PER_KV) if kv_shared else w

    w = next(c for c in ATTN_W_CHOICES
             if width % c == 0 and q_col0 % c == 0 and k_col0 % kv_width(c) == 0 and v_col0 % kv_width(c) == 0)
    n_pairs = w // LANES
    ncg = width // w
    kw = kv_width(w)
    pv = proj

    def qmap(b, r, i, g, s):
        return (r, b * nblk + i, q_col0 // w + g)

    def kvmap(col0, prev):
        def f(b, r, i, g, s):
            blk = jnp.maximum(i - 1, 0) if prev else i
            return (r, b * nblk + blk, col0 // kw + g)
        return f

    def omap(b, r, i, g, s):
        return (r, b * nblk + i, g)

    in_specs = [
        pl.BlockSpec((None, ATTN_BLOCK, w), qmap),
        pl.BlockSpec((None, ATTN_BLOCK, kw), kvmap(k_col0, True)),
        pl.BlockSpec((None, ATTN_BLOCK, kw), kvmap(k_col0, False)),
        pl.BlockSpec((None, ATTN_BLOCK, kw), kvmap(v_col0, True)),
        pl.BlockSpec((None, ATTN_BLOCK, kw), kvmap(v_col0, False)),
    ]
    args = [pv, pv, pv, pv, pv]
    with_sink = sinks is not None
    o_shape = jax.ShapeDtypeStruct((d, batch * lsub, width), BF16)
    o_spec = pl.BlockSpec((None, ATTN_BLOCK, w), omap)
    if with_sink:
        in_specs.append(pl.BlockSpec((1, w), lambda b, r, i, g, s: (0, g)))
        args.append(sinks)
        out_shape, out_specs = o_shape, o_spec
    else:
        assert n_heads <= LANES
        out_shape = (o_shape, jax.ShapeDtypeStruct((d, batch * lsub, LANES), F32))
        out_specs = (o_spec, pl.BlockSpec((None, ATTN_BLOCK, LANES),
                                          lambda b, r, i, g, s: (r, b * nblk + i, 0)))
    kern = functools.partial(_attn_kernel, n_pairs=n_pairs, kv_shared=kv_shared,
                             max_diff=max_diff, dist_scale=d, with_sink=with_sink)
    grid_spec = pltpu.PrefetchScalarGridSpec(
        num_scalar_prefetch=1, grid=(batch, d, nblk, ncg),
        in_specs=in_specs, out_specs=out_specs)
    res = pl.pallas_call(kern, grid_spec=grid_spec, out_shape=out_shape,
                         compiler_params=_cparams(4), name=f"band_attn_d{d}" + ("_sink" if with_sink else ""),
                         )(slopes, *args)
    return res


def _merge_kernel(*refs, dilations):
    n = len(dilations)
    o_refs, l_refs = refs[:n], refs[n:2 * n]
    spread_ref, out, scr, lscr = refs[2 * n:]
    tm, w = out.shape
    nch = w // LANES
    for a, d in enumerate(dilations):
        if d == 1:
            continue
        for r in range(d):
            lscr[a, pl.ds(r, tm // d, stride=d), :] = l_refs[a][r]
            for ch in range(nch):
                scr[a, ch, pl.ds(r, tm // d, stride=d), :] = (
                    o_refs[a][r, :, ch * LANES:(ch + 1) * LANES].astype(F32))
    ls_ = [l_refs[a][0] if d == 1 else lscr[a] for a, d in enumerate(dilations)]
    m = functools.reduce(jnp.maximum, ls_)
    es = [jnp.exp(l - m) for l in ls_]
    den = functools.reduce(lambda p, q: p + q, es)
    parts = []
    for e in es:
        wgt = e / den
        hi = wgt.astype(BF16)
        parts.append((hi, (wgt - hi.astype(F32)).astype(BF16)))
    span = 2 * LANES
    for c0 in range(0, w, span):
        sel = spread_ref[:, c0:c0 + span]
        nums = [None] * (span // LANES)
        for a, d in enumerate(dilations):
            hi, lo = parts[a]
            wide = jnp.dot(hi, sel, preferred_element_type=F32) + jnp.dot(lo, sel, preferred_element_type=F32)
            for k in range(span // LANES):
                ch = c0 // LANES + k
                o = o_refs[a][0, :, ch * LANES:(ch + 1) * LANES].astype(F32) if d == 1 else scr[a, ch]
                term = wide[:, k * LANES:(k + 1) * LANES] * o
                nums[k] = term if nums[k] is None else nums[k] + term
        for k in range(span // LANES):
            ch = c0 // LANES + k
            out[:, ch * LANES:(ch + 1) * LANES] = nums[k].astype(out.dtype)


def _merge(os_, ls_, dilations, tm=256):
    w = os_[0].shape[2]
    t = os_[0].shape[0] * os_[0].shape[1]
    o_specs = [pl.BlockSpec((d, tm // d, w), lambda i: (0, i, 0)) for d in dilations]
    l_specs = [pl.BlockSpec((d, tm // d, LANES), lambda i: (0, i, 0)) for d in dilations]
    spread = jnp.asarray(np.arange(LANES)[:, None] == np.arange(w)[None, :] // HEAD_DIM, dtype=BF16)
    return pl.pallas_call(
        functools.partial(_merge_kernel, dilations=tuple(dilations)), grid=(t // tm,),
        in_specs=o_specs + l_specs + [pl.BlockSpec((LANES, w), lambda i: (0, 0))],
        out_specs=pl.BlockSpec((tm, w), lambda i: (i, 0)),
        out_shape=jax.ShapeDtypeStruct((t, w), BF16),
        scratch_shapes=[pltpu.VMEM((len(dilations), w // LANES, tm, LANES), F32),
                        pltpu.VMEM((len(dilations), tm, LANES), F32)],
        compiler_params=_cparams(1), name="branch_merge",
    )(*os_, *ls_, spread)


def _outproj_kernel(oa_ref, ob_ref, wt_ref, wb_ref, x_ref, o_ref):
    acc = jnp.dot(oa_ref[...], wt_ref[...], preferred_element_type=F32)
    acc = acc + jnp.dot(ob_ref[...], wb_ref[...], preferred_element_type=F32)
    o_ref[...] = x_ref[...] + acc


def _outproj(o_a, o_b, w_out, x2d, tm, tn):
    t, d = x2d.shape
    ha, hb = o_a.shape[1], o_b.shape[1]
    assert ha == hb
    return pl.pallas_call(
        _outproj_kernel, grid=(t // tm, d // tn),
        in_specs=[
            pl.BlockSpec((tm, ha), lambda i, j: (i, 0)),
            pl.BlockSpec((tm, hb), lambda i, j: (i, 0)),
            pl.BlockSpec((ha, tn), lambda i, j: (0, j)),
            pl.BlockSpec((hb, tn), lambda i, j: (1, j)),
            pl.BlockSpec((tm, tn), lambda i, j: (i, j)),
        ],
        out_specs=pl.BlockSpec((tm, tn), lambda i, j: (i, j)),
        out_shape=jax.ShapeDtypeStruct((t, d), F32), compiler_params=_cparams(2), name="outproj",
    )(o_a, o_b, w_out, w_out, x2d)


def _router_kernel(x_ref, g_ref, whi_ref, wlo_ref, b_ref, tri_ref,
                   h_ref, idx_ref, gate_ref, rank_ref, cnt_ref, carry, h_scr):
    i = pl.program_id(0)

    @pl.when(i == 0)
    def _():
        carry[...] = jnp.zeros_like(carry)

    x = x_ref[...]
    ms = jnp.mean(x * x, axis=-1, keepdims=True)
    h = x * lax.rsqrt(ms + EPS) * g_ref[...]
    h_hi = h.astype(BF16)
    h_lo = (h - h_hi.astype(F32)).astype(BF16)
    u = pltpu.bitcast(h, jnp.uint32)
    r = (u + jnp.uint32(0x7FFF) + ((u >> 16) & jnp.uint32(1))) >> 16
    tok, words, _ = h_ref.shape
    for s in range(words):
        lo = r[:, 2 * s * LANES:(2 * s + 1) * LANES]
        hi = r[:, (2 * s + 1) * LANES:(2 * s + 2) * LANES]
        h_scr[pl.ds(s, tok, stride=words), :] = lo | (hi << 16)
    h_ref[...] = h_scr[...].reshape(tok, words, LANES)
    logits = (jnp.dot(h_hi, whi_ref[...], preferred_element_type=F32)
              + jnp.dot(h_lo, whi_ref[...], preferred_element_type=F32)
              + jnp.dot(h_hi, wlo_ref[...], preferred_element_type=F32)) + b_ref[...]

    tm = x.shape[0]
    lane = lax.broadcasted_iota(jnp.int32, (tm, LANES), 1).astype(F32)
    work = logits
    multihot = jnp.zeros((tm, LANES), F32)
    vals, idxs = [], []
    for _ in range(TOP_K):
        m = jnp.max(work, axis=-1, keepdims=True)
        ik = jnp.min(jnp.where(work == m, lane, float(LANES)), axis=-1, keepdims=True)
        sel = lane == ik
        work = jnp.where(sel, -jnp.inf, work)
        multihot = jnp.where(sel, 1.0, multihot)
        vals.append(m)
        idxs.append(ik)
    es = [jnp.exp(v - vals[0]) for v in vals]
    denom = es[0] + es[1] + es[2] + es[3]

    cum = jnp.dot(tri_ref[...], multihot.astype(BF16), preferred_element_type=F32) + carry[0:1, :]
    idx_t = jnp.zeros((tm, LANES), F32)
    gate_t = jnp.zeros((tm, LANES), F32)
    rank_t = jnp.zeros((tm, LANES), F32)
    for k in range(TOP_K):
        rk = jnp.sum(jnp.where(lane == idxs[k], cum, 0.0), axis=-1, keepdims=True)
        here = lane == float(k)
        idx_t = jnp.where(here, idxs[k], idx_t)
        gate_t = jnp.where(here, es[k] / denom, gate_t)
        rank_t = jnp.where(here, rk, rank_t)
    idx_ref[...] = idx_t.astype(jnp.int32)
    gate_ref[...] = gate_t
    rank_ref[...] = rank_t.astype(jnp.int32)
    new_carry = carry[0:1, :] + jnp.sum(multihot, axis=0, keepdims=True)
    carry[...] = jnp.broadcast_to(new_carry, carry.shape)
    cnt_ref[...] = jnp.broadcast_to(new_carry, cnt_ref.shape)


def _router(x2d, g2, w_router, b_router, tm=256):
    t, d = x2d.shape
    ne = w_router.shape[1]
    w_pad = jnp.zeros((d, LANES), F32).at[:, :ne].set(w_router)
    w_hi = w_pad.astype(BF16)
    w_lo = (w_pad - w_hi.astype(F32)).astype(BF16)
    b_pad = jnp.full((1, LANES), NEG_BIG, F32).at[0, :ne].set(b_router)
    tri = jnp.asarray(np.tril(np.ones((tm, tm), np.float32), -1), dtype=BF16)
    tile = lambda dt: jax.ShapeDtypeStruct((t, LANES), dt)
    row = pl.BlockSpec((tm, LANES), lambda i: (i, 0))
    const = lambda shape: pl.BlockSpec(shape, lambda i: (0, 0))
    return pl.pallas_call(
        _router_kernel, grid=(t // tm,),
        in_specs=[pl.BlockSpec((tm, d), lambda i: (i, 0)), const((1, d)), const((d, LANES)),
                  const((d, LANES)), const((1, LANES)), const((tm, tm))],
        out_specs=(pl.BlockSpec((tm, d // (2 * LANES), LANES), lambda i: (i, 0, 0)), row, row, row,
                   const((8, LANES))),
        out_shape=(jax.ShapeDtypeStruct((t, d // (2 * LANES), LANES), jnp.uint32),
                   tile(jnp.int32), tile(F32), tile(jnp.int32),
                   jax.ShapeDtypeStruct((8, LANES), F32)),
        scratch_shapes=[pltpu.VMEM((8, LANES), F32), pltpu.VMEM((tm * (d // (2 * LANES)), LANES), jnp.uint32)],
        compiler_params=_cparams(1), name="router",
    )(x2d, g2, w_hi, w_lo, b_pad, tri)


INDEX_SLOTS = 3
ROW_SLOTS = 2


def _row_prefetch_ring(idx_hbm, idx_smem, isem, start_rows, wait_rows):
    i = pl.program_id(0)
    n = pl.num_programs(0)

    def idx_copy(step):
        slot = lax.rem(step, INDEX_SLOTS)
        return pltpu.make_async_copy(idx_hbm.at[step], idx_smem.at[slot], isem.at[slot])

    @pl.when(i == 0)
    def _():
        idx_copy(0).start()
        idx_copy(0).wait()
        start_rows(0, 0, 0)

        @pl.when(n > 1)
        def _():
            idx_copy(1).start()

    @pl.when(i + 1 < n)
    def _():
        idx_copy(i + 1).wait()
        start_rows(i + 1, lax.rem(i + 1, INDEX_SLOTS), lax.rem(i + 1, ROW_SLOTS))

    @pl.when(i + 2 < n)
    def _():
        idx_copy(i + 2).start()

    row_slot = lax.rem(i, ROW_SLOTS)
    wait_rows(row_slot)
    return row_slot


def _gather_kernel(nvalid_ref, tok_hbm, h_hbm, xs_ref, tok_smem, buf, isem, rsem):
    g = GATHER_ROWS
    words = h_hbm.shape[1]

    @pl.when(pl.program_id(0) == 0)
    def _():
        buf[...] = jnp.zeros_like(buf)

    def row_copy(t, r, row_slot):
        dst = buf.at[row_slot, pl.ds(pl.multiple_of(r * words, words), words)]
        return pltpu.make_async_copy(h_hbm.at[t], dst, rsem.at[row_slot])

    def start_rows(step, idx_slot, row_slot):
        def body(r2, c):
            for prio in range(2):
                r = 2 * r2 + prio
                t = tok_smem[idx_slot, r]

                @pl.when(t >= 0)
                def _():
                    row_copy(t, r, row_slot).start(priority=prio)
            return c
        lax.fori_loop(0, g // 2, body, 0)

    def wait_rows(row_slot):
        groups = g // GATHER_WAIT_ROWS
        for q in range(groups):
            nvalid = nvalid_ref[pl.program_id(0) * groups + q]

            @pl.when(nvalid == GATHER_WAIT_ROWS)
            def _():
                part = buf.at[row_slot, pl.ds(0, GATHER_WAIT_ROWS * words)]
                pltpu.make_async_copy(part, part, rsem.at[row_slot]).wait()

            @pl.when(nvalid < GATHER_WAIT_ROWS)
            def _():
                def body(r, c):
                    row_copy(0, 0, row_slot).wait()
                    return c
                lax.fori_loop(0, nvalid, body, 0)

    row_slot = _row_prefetch_ring(tok_hbm, tok_smem, isem, start_rows, wait_rows)

    for s in range(words):
        x = buf[row_slot, pl.ds(s, g, stride=words), :]
        lo = pltpu.bitcast(x << 16, F32)
        hi = pltpu.bitcast(x & jnp.uint32(0xFFFF0000), F32)
        xs_ref[:, 2 * s * LANES:(2 * s + 1) * LANES] = lo.astype(BF16)
        xs_ref[:, (2 * s + 1) * LANES:(2 * s + 2) * LANES] = hi.astype(BF16)


def _gather_rows(h_packed, row_tok):
    t, words, _ = h_packed.shape
    r_max = row_tok.shape[0]
    g = GATHER_ROWS
    n_steps = r_max // g
    tok2d = row_tok.reshape(n_steps, g)
    nvalid = jnp.sum((row_tok.reshape(-1, GATHER_WAIT_ROWS) >= 0).astype(jnp.int32), axis=1)
    grid_spec = pltpu.PrefetchScalarGridSpec(
        num_scalar_prefetch=1, grid=(n_steps,),
        in_specs=[pl.BlockSpec(memory_space=pl.ANY), pl.BlockSpec(memory_space=pl.ANY)],
        out_specs=pl.BlockSpec((g, 2 * words * LANES), lambda i, nv: (i, 0)),
        scratch_shapes=[pltpu.SMEM((INDEX_SLOTS, g), jnp.int32),
                        pltpu.VMEM((ROW_SLOTS, g * words, LANES), jnp.uint32),
                        pltpu.SemaphoreType.DMA((INDEX_SLOTS,)), pltpu.SemaphoreType.DMA((ROW_SLOTS,))])
    return pl.pallas_call(
        _gather_kernel, grid_spec=grid_spec,
        out_shape=jax.ShapeDtypeStruct((r_max, 2 * words * LANES), BF16),
        compiler_params=_cparams(1), name="row_gather",
    )(nvalid, tok2d, h_packed)


PREP_ROWS = 1024


def _tile_clamp(i, na):
    return jnp.minimum(i, na[0] - 1)


def _weights_changed(te_ref, na_ref, i):
    cur = _tile_clamp(i, na_ref)
    prev = jnp.maximum(cur - 1, 0)
    return (i < na_ref[0]) & ((i == 0) | (te_ref[cur] != te_ref[prev]))


def _stream_expert_weights(w_hbm, raw, wsem, te_ref, meta_ref, nxt_ref, eidx_ref, convert):
    j = pl.program_id(0)
    i = pl.program_id(1)
    nj = pl.num_programs(0)
    tc = raw.shape[2]
    cur = _tile_clamp(i, meta_ref)

    def copy(e, jj, slot):
        col = pl.multiple_of(jj * tc, tc)
        return pltpu.make_async_copy(w_hbm.at[e, :, pl.ds(col, tc)], raw.at[slot], wsem.at[slot])

    @pl.when(_weights_changed(te_ref, meta_ref, i))
    def _():
        e = te_ref[cur]
        b = j * meta_ref[1] + eidx_ref[cur]
        slot = lax.rem(b, 2)
        nxt = nxt_ref[cur]

        @pl.when(b == 0)
        def _():
            copy(e, j, 0).start()

        @pl.when(nxt >= 0)
        def _():
            copy(nxt, j, 1 - slot).start()

        @pl.when((nxt < 0) & (j + 1 < nj))
        def _():
            copy(meta_ref[2], j + 1, 1 - slot).start()

        copy(e, j, slot).wait()
        convert(raw.at[slot])


def _row_cases(na_ref, tv_ref, i, tm, compute, zero):
    active = i < na_ref[0]
    valid = tv_ref[_tile_clamp(i, na_ref)]

    @pl.when(active & (valid > tm // 2))
    def _():
        compute(tm)

    @pl.when(active & (valid <= tm // 2))
    def _():
        compute(tm // 2)

    @pl.when(jnp.logical_not(active))
    def _():
        zero()


def _cast_rows(dst, src):
    n = dst.shape[0]
    for r0 in range(0, n, PREP_ROWS):
        rows = slice(r0, min(r0 + PREP_ROWS, n))
        dst[rows, :] = src[rows, :].astype(dst.dtype)


def _gateup_kernel(te_ref, na_ref, tv_ref, nxt_ref, eidx_ref, xs_ref, w_hbm, b_ref, pe_ref, h_ref,
                   w_s, raw, wsem):
    i = pl.program_id(1)
    d, tc = w_s.shape
    tm = h_ref.shape[0]
    _stream_expert_weights(w_hbm, raw, wsem, te_ref, na_ref, nxt_ref, eidx_ref,
                           functools.partial(_cast_rows, w_s))

    def compute(nrows):
        gu = jnp.dot(xs_ref[0:nrows, :], w_s[...], preferred_element_type=F32) + b_ref[...]
        acts = []
        for c in range(tc // LANES):
            g = gu[:, c * LANES:(c + 1) * LANES]
            u = pltpu.roll(g, LANES - 1, axis=1)
            gate = jnp.minimum(g, SWIGLU_LIMIT)
            up = jnp.clip(u, -SWIGLU_LIMIT, SWIGLU_LIMIT)
            acts.append(((up + 1.0) * gate * jax.nn.sigmoid(SWIGLU_ALPHA * gate)).astype(BF16))
        act = jnp.concatenate(acts, axis=1)
        h_ref[0:nrows, :] = jnp.dot(act, pe_ref[...], preferred_element_type=F32).astype(h_ref.dtype)
        if nrows < tm:
            h_ref[nrows:tm, :] = jnp.zeros((tm - nrows, h_ref.shape[1]), h_ref.dtype)

    def zero():
        h_ref[...] = jnp.zeros_like(h_ref)

    _row_cases(na_ref, tv_ref, i, tm, compute, zero)


def _down_kernel(te_ref, na_ref, tv_ref, nxt_ref, eidx_ref, h_ref, w_hbm, bd_ref, rw_ref, o_ref,
                 wd_s, o_scr, raw, wsem):
    i = pl.program_id(1)
    tm, chunks, _ = o_ref.shape
    _stream_expert_weights(w_hbm, raw, wsem, te_ref, na_ref, nxt_ref, eidx_ref,
                           functools.partial(_cast_rows, wd_s))

    def compute(nrows):
        out = jnp.dot(h_ref[0:nrows, :], wd_s[...], preferred_element_type=F32)
        out = (out + bd_ref[...]) * rw_ref[0:nrows, :]
        for c in range(chunks):
            o_scr[pl.ds(c, nrows, stride=chunks), :] = out[:, c * LANES:(c + 1) * LANES]
        if nrows < tm:
            o_scr[nrows * chunks:tm * chunks, :] = jnp.zeros(((tm - nrows) * chunks, LANES), F32)
        o_ref[...] = o_scr[...].reshape(tm, chunks, LANES)

    def zero():
        o_ref[...] = jnp.zeros_like(o_ref)

    _row_cases(na_ref, tv_ref, i, tm, compute, zero)


def _experts(xs, row_w, plan, w_gate_up, b_gate_up, w_down, bd, tn_ff, tn_d):
    r_max, d = xs.shape
    ff = w_down.shape[1]
    tm = EXPERT_TM
    n_tiles = r_max // tm
    tc = 2 * tn_ff
    p_even = jnp.asarray(np.arange(tc)[:, None] == 2 * np.arange(tn_ff)[None, :], dtype=BF16)
    n_plan = len(plan)
    hbm = pl.BlockSpec(memory_space=pl.ANY)
    weight_stream = lambda rows, cols: [pltpu.VMEM((2, rows, cols), F32), pltpu.SemaphoreType.DMA((2,))]

    def row_tile(j, i, te, meta, *_):
        return (_tile_clamp(i, meta), 0)

    def expert_cols(j, i, te, meta, *_):
        return (te[_tile_clamp(i, meta)], 0, j)

    gu_spec = pltpu.PrefetchScalarGridSpec(
        num_scalar_prefetch=n_plan, grid=(ff // tn_ff, n_tiles),
        in_specs=[
            pl.BlockSpec((tm, d), row_tile),
            hbm,
            pl.BlockSpec((None, 1, tc), expert_cols),
            pl.BlockSpec((tc, tn_ff), lambda j, i, *_: (0, 0)),
        ],
        out_specs=pl.BlockSpec((tm, tn_ff), lambda j, i, *_: (i, j)),
        scratch_shapes=[pltpu.VMEM((d, tc), BF16)] + weight_stream(d, tc))
    hidden = pl.pallas_call(
        _gateup_kernel, grid_spec=gu_spec, out_shape=jax.ShapeDtypeStruct((r_max, ff), BF16),
        compiler_params=_cparams(2), name="expert_gate_up",
    )(*plan, xs, w_gate_up, b_gate_up, p_even)
    dn_spec = pltpu.PrefetchScalarGridSpec(
        num_scalar_prefetch=n_plan, grid=(d // tn_d, n_tiles),
        in_specs=[
            pl.BlockSpec((tm, ff), row_tile),
            hbm,
            pl.BlockSpec((None, 1, tn_d), expert_cols),
            pl.BlockSpec((tm, 1), row_tile),
        ],
        out_specs=pl.BlockSpec((tm, tn_d // LANES, LANES), lambda j, i, *_: (i, j, 0)),
        scratch_shapes=[pltpu.VMEM((ff, tn_d), BF16), pltpu.VMEM((tm * (tn_d // LANES), LANES), F32)]
        + weight_stream(ff, tn_d))
    return pl.pallas_call(
        _down_kernel, grid_spec=dn_spec, out_shape=jax.ShapeDtypeStruct((r_max, d // LANES, LANES), F32),
        compiler_params=_cparams(2), name="expert_down",
    )(*plan, hidden, w_down, bd, row_w)


def _combine_kernel(dest_hbm, rows_hbm, x_ref, o_ref, dest_smem, buf, sum_scr, isem, rsem):
    tm = COMBINE_TM
    chunks = rows_hbm.shape[1]
    pitch = sum_scr.shape[0] // tm
    seg = tm * pitch

    @pl.when(pl.program_id(0) == 0)
    def _():
        buf[...] = jnp.zeros_like(buf)

    def start_rows(step, idx_slot, row_slot):
        def body(r, c):
            for k in range(TOP_K):
                src = dest_smem[idx_slot, r * TOP_K + k]
                off = pl.multiple_of((k * tm + r) * pitch, math.gcd(pitch, SUBLANES))
                pltpu.make_async_copy(rows_hbm.at[src], buf.at[row_slot, pl.ds(off, chunks)],
                                      rsem.at[row_slot]).start()
            return c
        lax.fori_loop(0, tm, body, 0, unroll=2)

    def wait_rows(row_slot):
        landed = buf.at[row_slot, pl.ds(0, TOP_K * tm * chunks)]
        pltpu.make_async_copy(landed, landed, rsem.at[row_slot]).wait()

    row_slot = _row_prefetch_ring(dest_hbm, dest_smem, isem, start_rows, wait_rows)

    acc = buf[row_slot, pl.ds(0, seg), :]
    for k in range(1, TOP_K):
        acc = acc + buf[row_slot, pl.ds(k * seg, seg), :]
    sum_scr[...] = acc
    for c in range(chunks):
        cols = slice(c * LANES, (c + 1) * LANES)
        o_ref[:, cols] = x_ref[:, cols] + sum_scr[pl.ds(c, tm, stride=pitch), :]


def _combine(rows, dest, x2d):
    t, d = x2d.shape
    chunks = rows.shape[1]
    tm = COMBINE_TM
    n_steps = t // tm
    dest2d = dest.reshape(n_steps, tm * TOP_K)
    blk = pl.BlockSpec((tm, d), lambda i: (i, 0))
    pitch = chunks + SUBLANES if chunks % (2 * SUBLANES) == 0 else chunks
    return pl.pallas_call(
        _combine_kernel, grid=(n_steps,),
        in_specs=[pl.BlockSpec(memory_space=pl.ANY), pl.BlockSpec(memory_space=pl.ANY), blk],
        out_specs=blk,
        out_shape=jax.ShapeDtypeStruct((t, d), F32),
        scratch_shapes=[pltpu.SMEM((INDEX_SLOTS, tm * TOP_K), jnp.int32),
                        pltpu.VMEM((ROW_SLOTS, TOP_K * tm * pitch, LANES), F32),
                        pltpu.VMEM((tm * pitch, LANES), F32),
                        pltpu.SemaphoreType.DMA((INDEX_SLOTS,)), pltpu.SemaphoreType.DMA((ROW_SLOTS,))],
        compiler_params=_cparams(1), name="combine",
    )(dest2d, rows, x2d)


def _moe(x_mid, g2, w_router, b_router, w_gate_up, b_gate_up, w_down, b_down, tn_ff, tn_d):
    t, d = x_mid.shape
    ne = w_router.shape[1]
    h2, idx_t, gate_t, rank_t, cnt = _router(x_mid, g2, w_router, b_router)

    tm = EXPERT_TM
    r_max = t * TOP_K + ne * tm
    n_tiles = r_max // tm
    counts = cnt[0, :ne].astype(jnp.int32)
    padded = (counts + tm - 1) // tm * tm
    pend = jnp.cumsum(padded)
    pstart = pend - padded
    idx = idx_t[:, :TOP_K]
    dest = (pstart[idx] + rank_t[:, :TOP_K]).astype(jnp.int32)
    n_rows_used = pend[-1:].astype(jnp.int32)
    n_active = n_rows_used // tm
    tok = jnp.repeat(jnp.arange(t, dtype=jnp.int32), TOP_K)
    gate_bits = lax.bitcast_convert_type(gate_t[:, :TOP_K].reshape(-1), jnp.int32)
    empty = jnp.broadcast_to(jnp.asarray([-1, 0], jnp.int32), (r_max, 2))
    table = empty.at[dest.reshape(-1)].set(
        jnp.stack([tok, gate_bits], axis=1), unique_indices=True)
    row_tok = table[:, 0]
    row_w = lax.bitcast_convert_type(table[:, 1], F32)[:, None]
    tile_start = jnp.arange(n_tiles, dtype=jnp.int32) * tm
    tile_e = jnp.minimum(jnp.sum((pend[None, :] <= tile_start[:, None]).astype(jnp.int32), axis=1), ne - 1)
    tile_valid = jnp.clip((pstart + counts)[tile_e] - tile_start, 0, tm).astype(jnp.int32)
    present = counts > 0
    eids = jnp.arange(ne, dtype=jnp.int32)
    later = present[None, :] & (eids[None, :] > eids[:, None])
    next_e = jnp.min(jnp.where(later, eids[None, :], ne), axis=1)
    next_e = jnp.where(next_e == ne, -1, next_e).astype(jnp.int32)
    rank_e = (jnp.cumsum(present.astype(jnp.int32)) - 1).astype(jnp.int32)
    meta = jnp.stack([n_active[0], jnp.sum(present.astype(jnp.int32)),
                      jnp.argmax(present).astype(jnp.int32)]).astype(jnp.int32)
    plan = (tile_e, meta, tile_valid, next_e[tile_e], rank_e[tile_e])

    xs = _gather_rows(h2, row_tok)

    rows = _experts(xs, row_w, plan, w_gate_up, b_gate_up[:, None, :], w_down, b_down[:, None, :], tn_ff, tn_d)
    return _combine(rows, dest, x_mid)


def _alibi_slopes(n):
    return (LOG2E * 2.0 ** (-ALIBI_MAX_BIAS * (np.arange(n, dtype=np.float64) + 1.0) / n)).astype(np.float32)


def _mixer(x2d, batch, seq, norm1_g, w_in, q_norm_swa, k_norm_swa, q_norm_dil, k_norm_dil, sinks, w_out,
           n_swa, n_kv, n_dil, tm_proj, tn_out):
    t, d = x2d.shape
    swa_q, swa_kv, dil_w = n_swa * HEAD_DIM, n_kv * HEAD_DIM, n_dil * HEAD_DIM
    assert n_swa // n_kv == GQA_GROUP and seq % (16 * ATTN_BLOCK) == 0
    s1, s2, s3 = swa_q, swa_q + swa_kv, swa_q + 2 * swa_kv

    def dup(wc):
        wc = wc.reshape(d, n_kv, HEAD_DIM)
        return jnp.concatenate([wc, wc], axis=-1).reshape(d, 2 * swa_kv)

    w = jnp.concatenate([w_in[:, :s1], dup(w_in[:, s1:s2]), dup(w_in[:, s2:s3]), w_in[:, s3:]],
                        axis=1).astype(BF16)
    scale = HEAD_DIM ** -0.5 * LOG2E
    ones = lambda n: jnp.ones((n,), F32)
    gains = jnp.concatenate([
        jnp.tile(q_norm_swa * scale, n_swa), jnp.tile(k_norm_swa, 2 * n_kv), ones(2 * swa_kv),
        jnp.tile(q_norm_dil * scale, n_dil), jnp.tile(k_norm_dil, n_dil), ones(dil_w)])[None, :]
    c = w.shape[1]
    bounds = np.cumsum([0, swa_q, 2 * swa_kv, 2 * swa_kv, dil_w, dil_w, dil_w])
    assert all(b % PROJ_TN == 0 for b in bounds)
    seg_norm = [1, 1, 0, 1, 1, 0]
    flags = np.zeros((c // PROJ_TN,), np.int32)
    for sidx in range(6):
        flags[bounds[sidx] // PROJ_TN:bounds[sidx + 1] // PROJ_TN] = seg_norm[sidx]
    qa0, ka0, va0, qb0, kb0, vb0 = (int(b) for b in bounds[:6])
    proj, *residue_major = _inproj(x2d, norm1_g[None, :], w, gains, jnp.asarray(flags), tm_proj, dil_col0=qb0)
    by_dilation = dict(zip(RESIDUE_DILATIONS, residue_major))

    sink_row = jnp.repeat(sinks.astype(F32), HEAD_DIM)[None, :]
    o_a = _band_attention(proj[None], jnp.asarray(_alibi_slopes(n_swa)), batch=batch, seq=seq, dilation=1,
                          q_col0=qa0, k_col0=ka0, v_col0=va0, n_heads=n_swa, kv_shared=True,
                          max_diff=SWA_WINDOW - 1, sinks=sink_row)[0]
    slopes_dil = jnp.asarray(_alibi_slopes(n_dil))
    outs, lses = [], []
    for window, dil in DILATED_BRANCHES:
        src, col0 = (proj[None], qb0) if dil == 1 else (by_dilation[dil], 0)
        o_i, lse_i = _band_attention(src, slopes_dil, batch=batch, seq=seq, dilation=dil,
                                     q_col0=col0, k_col0=col0 + kb0 - qb0, v_col0=col0 + vb0 - qb0,
                                     n_heads=n_dil, kv_shared=False, max_diff=window // dil)
        outs.append(o_i)
        lses.append(lse_i)
    o_b = _merge(outs, lses, [dil for _, dil in DILATED_BRANCHES])
    return _outproj(o_a, o_b, w_out.astype(BF16), x2d, tm_proj, tn_out)


def kernel(x, norm1_g, w_in, q_norm_swa, k_norm_swa, q_norm_dil, k_norm_dil, sinks, w_out, norm2_g,
           w_router, b_router, w_gate_up, b_gate_up, w_down, b_down):
    b, s, d = x.shape
    depth = norm1_g.shape[0]
    n_heads = d // HEAD_DIM
    n_swa = n_heads // 2
    n_kv = n_swa // GQA_GROUP
    n_dil = n_heads - n_swa
    x2d = x.reshape(b * s, d)
    for l in range(depth):
        x_mid = _mixer(x2d, b, s, norm1_g[l], w_in[l], q_norm_swa[l], k_norm_swa[l], q_norm_dil[l],
                       k_norm_dil[l], sinks[l], w_out[l], n_swa, n_kv, n_dil, tm_proj=512, tn_out=1024)
        x2d = _moe(x_mid, norm2_g[l][None, :], w_router[l], b_router[l], w_gate_up[l], b_gate_up[l],
                   w_down[l], b_down[l], tn_ff=512, tn_d=1024)
    return x2d.reshape(b, s, d)
```

```python
import functools
import math

import jax
import jax.numpy as jnp
import numpy as np
from jax import lax
from jax.experimental import pallas as pl
from jax.experimental.pallas import tpu as pltpu

F32 = jnp.float32
BF16 = jnp.bfloat16

HEAD_DIM = 64
LANES = 128
SUBLANES = 8
ATTN_BLOCK = 128
GQA_GROUP = 8
SWA_WINDOW = 128
DILATED_BRANCHES = ((128, 1), (512, 4), (2048, 16))
N_EXPERTS = 32
TOP_K = 4
SWIGLU_LIMIT = 7.0
SWIGLU_ALPHA = 1.702
ALIBI_MAX_BIAS = 8.0
EPS = 1e-6
MASK_DIST = 1e30
LOG2E = 1.4426950408889634
LN2 = 0.6931471805599453
NEG_BIG = -1e30

VMEM_LIMIT = 58 * 1024 * 1024

PROJ_TN = 512
ATTN_W_CHOICES = (2048, 1024, 512)
PAIRS_PER_KV = GQA_GROUP // 2
EXPERT_TM = 512
GATHER_ROWS = 512
COMBINE_TM = 128


def _cparams(n_axes):
    return pltpu.CompilerParams(dimension_semantics=("arbitrary",) * n_axes,
                                vmem_limit_bytes=VMEM_LIMIT)


RESIDUE_DILATIONS = tuple(d for _, d in DILATED_BRANCHES if d > 1)


def _inproj_kernel(flag_ref, x_ref, g_ref, w_ref, gain_ref, ones_ref, o_ref, *rest, dil_tile0):
    res_refs, (h_scr, y_scr) = rest[:len(RESIDUE_DILATIONS)], rest[len(RESIDUE_DILATIONS):]
    j = pl.program_id(1)
    tm, tn = o_ref.shape

    @pl.when(j == 0)
    def _():
        x = x_ref[...]
        ms = jnp.mean(x * x, axis=-1, keepdims=True)
        h_scr[...] = (x * lax.rsqrt(ms + EPS) * g_ref[...]).astype(BF16)

    acc = jnp.dot(h_scr[...], w_ref[...], preferred_element_type=F32)

    def emit(y):
        o_ref[...] = y.astype(BF16)
        for ch in range(tn // LANES):
            y_scr[ch] = y[:, ch * LANES:(ch + 1) * LANES]

    @pl.when(flag_ref[j] == 1)
    def _():
        ss = jnp.dot((acc * acc).astype(BF16), ones_ref[...], preferred_element_type=F32)
        emit(acc * lax.rsqrt(ss * (1.0 / HEAD_DIM) + EPS) * gain_ref[...])

    @pl.when(flag_ref[j] == 0)
    def _():
        emit(acc)

    @pl.when(j >= dil_tile0)
    def _():
        for d, ref in zip(RESIDUE_DILATIONS, res_refs):
            for r in range(d):
                for ch in range(tn // LANES):
                    rows = y_scr[ch, pl.ds(r, tm // d, stride=d), :]
                    ref[r, :, ch * LANES:(ch + 1) * LANES] = rows.astype(BF16)


def _inproj(x2d, g1, w, gains, flags, tm, dil_col0):
    t, d = x2d.shape
    c = w.shape[1]
    tn = PROJ_TN
    dil_tile0 = dil_col0 // tn
    cb = c - dil_col0
    head_id = np.arange(tn) // HEAD_DIM
    ones_bd = jnp.asarray(head_id[:, None] == head_id[None, :], dtype=BF16)
    res_specs = [pl.BlockSpec((dd, tm // dd, tn), lambda i, j, f: (0, i, jnp.maximum(j - dil_tile0, 0)))
                 for dd in RESIDUE_DILATIONS]
    res_shapes = [jax.ShapeDtypeStruct((dd, t // dd, cb), BF16) for dd in RESIDUE_DILATIONS]
    grid_spec = pltpu.PrefetchScalarGridSpec(
        num_scalar_prefetch=1,
        grid=(t // tm, c // tn),
        in_specs=[
            pl.BlockSpec((tm, d), lambda i, j, f: (i, 0)),
            pl.BlockSpec((1, d), lambda i, j, f: (0, 0)),
            pl.BlockSpec((d, tn), lambda i, j, f: (0, j)),
            pl.BlockSpec((1, tn), lambda i, j, f: (0, j)),
            pl.BlockSpec((tn, tn), lambda i, j, f: (0, 0)),
        ],
        out_specs=[pl.BlockSpec((tm, tn), lambda i, j, f: (i, j))] + res_specs,
        scratch_shapes=[pltpu.VMEM((tm, d), BF16), pltpu.VMEM((tn // LANES, tm, LANES), F32)],
    )
    return pl.pallas_call(
        functools.partial(_inproj_kernel, dil_tile0=dil_tile0), grid_spec=grid_spec,
        out_shape=[jax.ShapeDtypeStruct((t, c), BF16)] + res_shapes,
        compiler_params=_cparams(2), name="inproj",
    )(flags, x2d, g1, w, gains, ones_bd)


def _attn_kernel(slope_ref, q_ref, kp_ref, kc_ref, vp_ref, vc_ref, *rest,
                 n_pairs, kv_shared, max_diff, dist_scale, with_sink):
    if with_sink:
        sink_ref, o_ref = rest
        lse_ref = None
    else:
        o_ref, lse_ref = rest
    blk = pl.program_id(2)
    cg = pl.program_id(3)

    qi = lax.broadcasted_iota(jnp.int32, (ATTN_BLOCK, 2 * ATTN_BLOCK), 0)
    kj = lax.broadcasted_iota(jnp.int32, (ATTN_BLOCK, 2 * ATTN_BLOCK), 1)
    dist = qi + ATTN_BLOCK - kj
    valid = (dist >= 0) & (dist <= max_diff) & ((blk > 0) | (kj >= ATTN_BLOCK))
    dist_m = jnp.where(valid, (dist * dist_scale).astype(F32), MASK_DIST)

    lane = lax.broadcasted_iota(jnp.int32, (ATTN_BLOCK, LANES), 1)
    low = lane < HEAD_DIM
    if not with_sink:
        @pl.when(cg == 0)
        def _():
            lse_ref[...] = jnp.zeros_like(lse_ref)
        lse_tile = lse_ref[...]

    for p in range(n_pairs):
        cols = slice(p * LANES, (p + 1) * LANES)
        kv = p // PAIRS_PER_KV
        kcols = slice(kv * LANES, (kv + 1) * LANES) if kv_shared else cols
        q2 = q_ref[:, cols]
        kk = jnp.concatenate([kp_ref[:, kcols], kc_ref[:, kcols]], axis=0)
        vv = jnp.concatenate([vp_ref[:, kcols], vc_ref[:, kcols]], axis=0)
        outs, lses = [], []
        for hh in range(2):
            slope = slope_ref[cg * (2 * n_pairs) + 2 * p + hh]
            qm = jnp.where(low if hh == 0 else ~low, q2, jnp.zeros_like(q2))
            s = lax.dot_general(qm, kk, (((1,), (1,)), ((), ())),
                                preferred_element_type=F32)
            s = s - slope * dist_m
            m = jnp.max(s, axis=-1, keepdims=True)
            e = jnp.exp2(s - m)
            l = jnp.sum(e, axis=-1, keepdims=True)
            o = jnp.dot(e.astype(BF16), vv, preferred_element_type=F32)
            outs.append(o / l)
            lse = (m + jnp.log2(l)) * LN2
            if with_sink:
                lses.append(jnp.broadcast_to(lse, (ATTN_BLOCK, LANES)))
            else:
                head = cg * (2 * n_pairs) + 2 * p + hh
                lse_tile = jnp.where(lane == head, lse, lse_tile)
        o2 = jnp.where(low, outs[0], outs[1])
        if with_sink:
            lse2 = jnp.where(low, lses[0], lses[1])
            o2 = o2 * jax.nn.sigmoid(lse2 - sink_ref[:, cols])
        o_ref[:, cols] = o2.astype(o_ref.dtype)
    if not with_sink:
        lse_ref[...] = lse_tile


def _band_attention(proj, slopes, *, batch, seq, dilation, q_col0, k_col0, v_col0, n_heads,
                    kv_shared, max_diff, sinks=None):
    d = dilation
    assert proj.shape[0] == d
    lsub = seq // d
    nblk = lsub // ATTN_BLOCK
    width = n_heads * HEAD_DIM

    def kv_width(w):
        return LANES * (w // LANES // PAIRS_PER_KV) if kv_shared else w

    w = next(c for c in ATTN_W_CHOICES
             if width % c == 0 and q_col0 % c == 0 and k_col0 % kv_width(c) == 0 and v_col0 % kv_width(c) == 0)
    n_pairs = w // LANES
    ncg = width // w
    kw = kv_width(w)
    pv = proj

    def qmap(b, r, i, g, s):
        return (r, b * nblk + i, q_col0 // w + g)

    def kvmap(col0, prev):
        def f(b, r, i, g, s):
            blk = jnp.maximum(i - 1, 0) if prev else i
            return (r, b * nblk + blk, col0 // kw + g)
        return f

    def omap(b, r, i, g, s):
        return (r, b * nblk + i, g)

    in_specs = [
        pl.BlockSpec((None, ATTN_BLOCK, w), qmap),
        pl.BlockSpec((None, ATTN_BLOCK, kw), kvmap(k_col0, True)),
        pl.BlockSpec((None, ATTN_BLOCK, kw), kvmap(k_col0, False)),
        pl.BlockSpec((None, ATTN_BLOCK, kw), kvmap(v_col0, True)),
        pl.BlockSpec((None, ATTN_BLOCK, kw), kvmap(v_col0, False)),
    ]
    args = [pv, pv, pv, pv, pv]
    with_sink = sinks is not None
    o_shape = jax.ShapeDtypeStruct((d, batch * lsub, width), BF16)
    o_spec = pl.BlockSpec((None, ATTN_BLOCK, w), omap)
    if with_sink:
        in_specs.append(pl.BlockSpec((1, w), lambda b, r, i, g, s: (0, g)))
        args.append(sinks)
        out_shape, out_specs = o_shape, o_spec
    else:
        assert n_heads <= LANES
        out_shape = (o_shape, jax.ShapeDtypeStruct((d, batch * lsub, LANES), F32))
        out_specs = (o_spec, pl.BlockSpec((None, ATTN_BLOCK, LANES),
                                          lambda b, r, i, g, s: (r, b * nblk + i, 0)))
    kern = functools.partial(_attn_kernel, n_pairs=n_pairs, kv_shared=kv_shared,
                             max_diff=max_diff, dist_scale=d, with_sink=with_sink)
    grid_spec = pltpu.PrefetchScalarGridSpec(
        num_scalar_prefetch=1, grid=(batch, d, nblk, ncg),
        in_specs=in_specs, out_specs=out_specs)
    res = pl.pallas_call(kern, grid_spec=grid_spec, out_shape=out_shape,
                         compiler_params=_cparams(4), name=f"band_attn_d{d}" + ("_sink" if with_sink else ""),
                         )(slopes, *args)
    return res


def _merge_kernel(*refs, dilations):
    n = len(dilations)
    o_refs, l_refs = refs[:n], refs[n:2 * n]
    spread_ref, out, scr, lscr = refs[2 * n:]
    tm, w = out.shape
    nch = w // LANES
    for a, d in enumerate(dilations):
        if d == 1:
            continue
        for r in range(d):
            lscr[a, pl.ds(r, tm // d, stride=d), :] = l_refs[a][r]
            for ch in range(nch):
                scr[a, ch, pl.ds(r, tm // d, stride=d), :] = (
                    o_refs[a][r, :, ch * LANES:(ch + 1) * LANES].astype(F32))
    ls_ = [l_refs[a][0] if d == 1 else lscr[a] for a, d in enumerate(dilations)]
    m = functools.reduce(jnp.maximum, ls_)
    es = [jnp.exp(l - m) for l in ls_]
    den = functools.reduce(lambda p, q: p + q, es)
    parts = []
    for e in es:
        wgt = e / den
        hi = wgt.astype(BF16)
        parts.append((hi, (wgt - hi.astype(F32)).astype(BF16)))
    span = 2 * LANES
    for c0 in range(0, w, span):
        sel = spread_ref[:, c0:c0 + span]
        nums = [None] * (span // LANES)
        for a, d in enumerate(dilations):
            hi, lo = parts[a]
            wide = jnp.dot(hi, sel, preferred_element_type=F32) + jnp.dot(lo, sel, preferred_element_type=F32)
            for k in range(span // LANES):
                ch = c0 // LANES + k
                o = o_refs[a][0, :, ch * LANES:(ch + 1) * LANES].astype(F32) if d == 1 else scr[a, ch]
                term = wide[:, k * LANES:(k + 1) * LANES] * o
                nums[k] = term if nums[k] is None else nums[k] + term
        for k in range(span // LANES):
            ch = c0 // LANES + k
            out[:, ch * LANES:(ch + 1) * LANES] = nums[k].astype(out.dtype)


def _merge(os_, ls_, dilations, tm=256):
    w = os_[0].shape[2]
    t = os_[0].shape[0] * os_[0].shape[1]
    o_specs = [pl.BlockSpec((d, tm // d, w), lambda i: (0, i, 0)) for d in dilations]
    l_specs = [pl.BlockSpec((d, tm // d, LANES), lambda i: (0, i, 0)) for d in dilations]
    spread = jnp.asarray(np.arange(LANES)[:, None] == np.arange(w)[None, :] // HEAD_DIM, dtype=BF16)
    return pl.pallas_call(
        functools.partial(_merge_kernel, dilations=tuple(dilations)), grid=(t // tm,),
        in_specs=o_specs + l_specs + [pl.BlockSpec((LANES, w), lambda i: (0, 0))],
        out_specs=pl.BlockSpec((tm, w), lambda i: (i, 0)),
        out_shape=jax.ShapeDtypeStruct((t, w), BF16),
        scratch_shapes=[pltpu.VMEM((len(dilations), w // LANES, tm, LANES), F32),
                        pltpu.VMEM((len(dilations), tm, LANES), F32)],
        compiler_params=_cparams(1), name="branch_merge",
    )(*os_, *ls_, spread)


def _outproj_kernel(oa_ref, ob_ref, wt_ref, wb_ref, x_ref, o_ref):
    acc = jnp.dot(oa_ref[...], wt_ref[...], preferred_element_type=F32)
    acc = acc + jnp.dot(ob_ref[...], wb_ref[...], preferred_element_type=F32)
    o_ref[...] = x_ref[...] + acc


def _outproj(o_a, o_b, w_out, x2d, tm, tn):
    t, d = x2d.shape
    ha, hb = o_a.shape[1], o_b.shape[1]
    assert ha == hb
    return pl.pallas_call(
        _outproj_kernel, grid=(t // tm, d // tn),
        in_specs=[
            pl.BlockSpec((tm, ha), lambda i, j: (i, 0)),
            pl.BlockSpec((tm, hb), lambda i, j: (i, 0)),
            pl.BlockSpec((ha, tn), lambda i, j: (0, j)),
            pl.BlockSpec((hb, tn), lambda i, j: (1, j)),
            pl.BlockSpec((tm, tn), lambda i, j: (i, j)),
        ],
        out_specs=pl.BlockSpec((tm, tn), lambda i, j: (i, j)),
        out_shape=jax.ShapeDtypeStruct((t, d), F32), compiler_params=_cparams(2), name="outproj",
    )(o_a, o_b, w_out, w_out, x2d)


def _router_kernel(x_ref, g_ref, whi_ref, wlo_ref, b_ref, tri_ref,
                   h_ref, idx_ref, gate_ref, rank_ref, cnt_ref, carry, h_scr):
    i = pl.program_id(0)

    @pl.when(i == 0)
    def _():
        carry[...] = jnp.zeros_like(carry)

    x = x_ref[...]
    ms = jnp.mean(x * x, axis=-1, keepdims=True)
    h = x * lax.rsqrt(ms + EPS) * g_ref[...]
    h_hi = h.astype(BF16)
    h_lo = (h - h_hi.astype(F32)).astype(BF16)
    u = pltpu.bitcast(h, jnp.uint32)
    r = (u + jnp.uint32(0x7FFF) + ((u >> 16) & jnp.uint32(1))) >> 16
    tok, words, _ = h_ref.shape
    for s in range(words):
        lo = r[:, 2 * s * LANES:(2 * s + 1) * LANES]
        hi = r[:, (2 * s + 1) * LANES:(2 * s + 2) * LANES]
        h_scr[pl.ds(s, tok, stride=words), :] = lo | (hi << 16)
    h_ref[...] = h_scr[...].reshape(tok, words, LANES)
    logits = (jnp.dot(h_hi, whi_ref[...], preferred_element_type=F32)
              + jnp.dot(h_lo, whi_ref[...], preferred_element_type=F32)
              + jnp.dot(h_hi, wlo_ref[...], preferred_element_type=F32)) + b_ref[...]

    tm = x.shape[0]
    lane = lax.broadcasted_iota(jnp.int32, (tm, LANES), 1).astype(F32)
    work = logits
    multihot = jnp.zeros((tm, LANES), F32)
    vals, idxs = [], []
    for _ in range(TOP_K):
        m = jnp.max(work, axis=-1, keepdims=True)
        ik = jnp.min(jnp.where(work == m, lane, float(LANES)), axis=-1, keepdims=True)
        sel = lane == ik
        work = jnp.where(sel, -jnp.inf, work)
        multihot = jnp.where(sel, 1.0, multihot)
        vals.append(m)
        idxs.append(ik)
    es = [jnp.exp(v - vals[0]) for v in vals]
    denom = es[0] + es[1] + es[2] + es[3]

    cum = jnp.dot(tri_ref[...], multihot.astype(BF16), preferred_element_type=F32) + carry[0:1, :]
    idx_t = jnp.zeros((tm, LANES), F32)
    gate_t = jnp.zeros((tm, LANES), F32)
    rank_t = jnp.zeros((tm, LANES), F32)
    for k in range(TOP_K):
        rk = jnp.sum(jnp.where(lane == idxs[k], cum, 0.0), axis=-1, keepdims=True)
        here = lane == float(k)
        idx_t = jnp.where(here, idxs[k], idx_t)
        gate_t = jnp.where(here, es[k] / denom, gate_t)
        rank_t = jnp.where(here, rk, rank_t)
    idx_ref[...] = idx_t.astype(jnp.int32)
    gate_ref[...] = gate_t
    rank_ref[...] = rank_t.astype(jnp.int32)
    new_carry = carry[0:1, :] + jnp.sum(multihot, axis=0, keepdims=True)
    carry[...] = jnp.broadcast_to(new_carry, carry.shape)
    cnt_ref[...] = jnp.broadcast_to(new_carry, cnt_ref.shape)


def _router(x2d, g2, w_router, b_router, tm=512):
    t, d = x2d.shape
    ne = w_router.shape[1]
    w_pad = jnp.zeros((d, LANES), F32).at[:, :ne].set(w_router)
    w_hi = w_pad.astype(BF16)
    w_lo = (w_pad - w_hi.astype(F32)).astype(BF16)
    b_pad = jnp.full((1, LANES), NEG_BIG, F32).at[0, :ne].set(b_router)
    tri = jnp.asarray(np.tril(np.ones((tm, tm), np.float32), -1), dtype=BF16)
    tile = lambda dt: jax.ShapeDtypeStruct((t, LANES), dt)
    row = pl.BlockSpec((tm, LANES), lambda i: (i, 0))
    const = lambda shape: pl.BlockSpec(shape, lambda i: (0, 0))
    return pl.pallas_call(
        _router_kernel, grid=(t // tm,),
        in_specs=[pl.BlockSpec((tm, d), lambda i: (i, 0)), const((1, d)), const((d, LANES)),
                  const((d, LANES)), const((1, LANES)), const((tm, tm))],
        out_specs=(pl.BlockSpec((tm, d // (2 * LANES), LANES), lambda i: (i, 0, 0)), row, row, row,
                   const((8, LANES))),
        out_shape=(jax.ShapeDtypeStruct((t, d // (2 * LANES), LANES), jnp.uint32),
                   tile(jnp.int32), tile(F32), tile(jnp.int32),
                   jax.ShapeDtypeStruct((8, LANES), F32)),
        scratch_shapes=[pltpu.VMEM((8, LANES), F32), pltpu.VMEM((tm * (d // (2 * LANES)), LANES), jnp.uint32)],
        compiler_params=_cparams(1), name="router",
    )(x2d, g2, w_hi, w_lo, b_pad, tri)


INDEX_SLOTS = 3
ROW_SLOTS = 2


def _row_prefetch_ring(idx_hbm, idx_smem, isem, start_rows, wait_rows):
    i = pl.program_id(0)
    n = pl.num_programs(0)

    def idx_copy(step):
        slot = lax.rem(step, INDEX_SLOTS)
        return pltpu.make_async_copy(idx_hbm.at[step], idx_smem.at[slot], isem.at[slot])

    @pl.when(i == 0)
    def _():
        idx_copy(0).start()
        idx_copy(0).wait()
        start_rows(0, 0, 0)

        @pl.when(n > 1)
        def _():
            idx_copy(1).start()

    @pl.when(i + 1 < n)
    def _():
        idx_copy(i + 1).wait()
        start_rows(i + 1, lax.rem(i + 1, INDEX_SLOTS), lax.rem(i + 1, ROW_SLOTS))

    @pl.when(i + 2 < n)
    def _():
        idx_copy(i + 2).start()

    row_slot = lax.rem(i, ROW_SLOTS)
    wait_rows(row_slot)
    return row_slot


def _gather_kernel(nvalid_ref, tok_hbm, h_hbm, xs_ref, tok_smem, buf, isem, rsem):
    g = GATHER_ROWS
    words = h_hbm.shape[1]

    @pl.when(pl.program_id(0) == 0)
    def _():
        buf[...] = jnp.zeros_like(buf)

    def row_copy(t, r, row_slot):
        dst = buf.at[row_slot, pl.ds(pl.multiple_of(r * words, words), words)]
        return pltpu.make_async_copy(h_hbm.at[t], dst, rsem.at[row_slot])

    def start_rows(step, idx_slot, row_slot):
        def body(r2, c):
            for prio in range(2):
                r = 2 * r2 + prio
                t = tok_smem[idx_slot, r]

                @pl.when(t >= 0)
                def _():
                    row_copy(t, r, row_slot).start(priority=prio)
            return c
        lax.fori_loop(0, g // 2, body, 0)

    def wait_rows(row_slot):
        nvalid = nvalid_ref[pl.program_id(0)]

        @pl.when(nvalid == g)
        def _():
            pltpu.make_async_copy(buf.at[row_slot], buf.at[row_slot], rsem.at[row_slot]).wait()

        @pl.when(nvalid < g)
        def _():
            def body(r, c):
                row_copy(0, 0, row_slot).wait()
                return c
            lax.fori_loop(0, nvalid, body, 0)

    row_slot = _row_prefetch_ring(tok_hbm, tok_smem, isem, start_rows, wait_rows)

    for s in range(words):
        x = buf[row_slot, pl.ds(s, g, stride=words), :]
        lo = pltpu.bitcast(x << 16, F32)
        hi = pltpu.bitcast(x & jnp.uint32(0xFFFF0000), F32)
        xs_ref[:, 2 * s * LANES:(2 * s + 1) * LANES] = lo.astype(BF16)
        xs_ref[:, (2 * s + 1) * LANES:(2 * s + 2) * LANES] = hi.astype(BF16)


def _gather_rows(h_packed, row_tok):
    t, words, _ = h_packed.shape
    r_max = row_tok.shape[0]
    g = GATHER_ROWS
    n_steps = r_max // g
    tok2d = row_tok.reshape(n_steps, g)
    nvalid = jnp.sum((tok2d >= 0).astype(jnp.int32), axis=1)
    grid_spec = pltpu.PrefetchScalarGridSpec(
        num_scalar_prefetch=1, grid=(n_steps,),
        in_specs=[pl.BlockSpec(memory_space=pl.ANY), pl.BlockSpec(memory_space=pl.ANY)],
        out_specs=pl.BlockSpec((g, 2 * words * LANES), lambda i, nv: (i, 0)),
        scratch_shapes=[pltpu.SMEM((INDEX_SLOTS, g), jnp.int32),
                        pltpu.VMEM((ROW_SLOTS, g * words, LANES), jnp.uint32),
                        pltpu.SemaphoreType.DMA((INDEX_SLOTS,)), pltpu.SemaphoreType.DMA((ROW_SLOTS,))])
    return pl.pallas_call(
        _gather_kernel, grid_spec=grid_spec,
        out_shape=jax.ShapeDtypeStruct((r_max, 2 * words * LANES), BF16),
        compiler_params=_cparams(1), name="row_gather",
    )(nvalid, tok2d, h_packed)


PREP_ROWS = 1024


def _tile_clamp(i, na):
    return jnp.minimum(i, na[0] - 1)


def _weights_changed(te_ref, na_ref, i):
    cur = _tile_clamp(i, na_ref)
    prev = jnp.maximum(cur - 1, 0)
    return (i < na_ref[0]) & ((i == 0) | (te_ref[cur] != te_ref[prev]))


def _stream_expert_weights(w_hbm, raw, wsem, te_ref, meta_ref, nxt_ref, eidx_ref, convert):
    j = pl.program_id(0)
    i = pl.program_id(1)
    nj = pl.num_programs(0)
    tc = raw.shape[2]
    cur = _tile_clamp(i, meta_ref)

    def copy(e, jj, slot):
        col = pl.multiple_of(jj * tc, tc)
        return pltpu.make_async_copy(w_hbm.at[e, :, pl.ds(col, tc)], raw.at[slot], wsem.at[slot])

    @pl.when(_weights_changed(te_ref, meta_ref, i))
    def _():
        e = te_ref[cur]
        b = j * meta_ref[1] + eidx_ref[cur]
        slot = lax.rem(b, 2)
        nxt = nxt_ref[cur]

        @pl.when(b == 0)
        def _():
            copy(e, j, 0).start()

        @pl.when(nxt >= 0)
        def _():
            copy(nxt, j, 1 - slot).start()

        @pl.when((nxt < 0) & (j + 1 < nj))
        def _():
            copy(meta_ref[2], j + 1, 1 - slot).start()

        copy(e, j, slot).wait()
        convert(raw.at[slot])


def _row_cases(na_ref, tv_ref, i, tm, compute, zero):
    active = i < na_ref[0]
    valid = tv_ref[_tile_clamp(i, na_ref)]

    @pl.when(active & (valid > tm // 2))
    def _():
        compute(tm)

    @pl.when(active & (valid <= tm // 2))
    def _():
        compute(tm // 2)

    @pl.when(jnp.logical_not(active))
    def _():
        zero()


def _cast_rows(dst, src):
    n = dst.shape[0]
    for r0 in range(0, n, PREP_ROWS):
        rows = slice(r0, min(r0 + PREP_ROWS, n))
        dst[rows, :] = src[rows, :].astype(dst.dtype)


def _gateup_kernel(te_ref, na_ref, tv_ref, nxt_ref, eidx_ref, xs_ref, w_hbm, b_ref, pe_ref, h_ref,
                   w_s, raw, wsem):
    i = pl.program_id(1)
    d, tc = w_s.shape
    tm = h_ref.shape[0]
    _stream_expert_weights(w_hbm, raw, wsem, te_ref, na_ref, nxt_ref, eidx_ref,
                           functools.partial(_cast_rows, w_s))

    def compute(nrows):
        gu = jnp.dot(xs_ref[0:nrows, :], w_s[...], preferred_element_type=F32) + b_ref[...]
        acts = []
        for c in range(tc // LANES):
            g = gu[:, c * LANES:(c + 1) * LANES]
            u = pltpu.roll(g, LANES - 1, axis=1)
            gate = jnp.minimum(g, SWIGLU_LIMIT)
            up = jnp.clip(u, -SWIGLU_LIMIT, SWIGLU_LIMIT)
            acts.append(((up + 1.0) * gate * jax.nn.sigmoid(SWIGLU_ALPHA * gate)).astype(BF16))
        act = jnp.concatenate(acts, axis=1)
        h_ref[0:nrows, :] = jnp.dot(act, pe_ref[...], preferred_element_type=F32).astype(h_ref.dtype)
        if nrows < tm:
            h_ref[nrows:tm, :] = jnp.zeros((tm - nrows, h_ref.shape[1]), h_ref.dtype)

    def zero():
        h_ref[...] = jnp.zeros_like(h_ref)

    _row_cases(na_ref, tv_ref, i, tm, compute, zero)


def _down_kernel(te_ref, na_ref, tv_ref, nxt_ref, eidx_ref, h_ref, w_hbm, bd_ref, rw_ref, o_ref,
                 wd_s, o_scr, raw, wsem):
    i = pl.program_id(1)
    tm, chunks, _ = o_ref.shape
    _stream_expert_weights(w_hbm, raw, wsem, te_ref, na_ref, nxt_ref, eidx_ref,
                           functools.partial(_cast_rows, wd_s))

    def compute(nrows):
        out = jnp.dot(h_ref[0:nrows, :], wd_s[...], preferred_element_type=F32)
        out = (out + bd_ref[...]) * rw_ref[0:nrows, :]
        for c in range(chunks):
            o_scr[pl.ds(c, nrows, stride=chunks), :] = out[:, c * LANES:(c + 1) * LANES]
        if nrows < tm:
            o_scr[nrows * chunks:tm * chunks, :] = jnp.zeros(((tm - nrows) * chunks, LANES), F32)
        o_ref[...] = o_scr[...].reshape(tm, chunks, LANES)

    def zero():
        o_ref[...] = jnp.zeros_like(o_ref)

    _row_cases(na_ref, tv_ref, i, tm, compute, zero)


def _experts(xs, row_w, plan, w_gate_up, b_gate_up, w_down, bd, tn_ff, tn_d):
    r_max, d = xs.shape
    ff = w_down.shape[1]
    tm = EXPERT_TM
    n_tiles = r_max // tm
    tc = 2 * tn_ff
    p_even = jnp.asarray(np.arange(tc)[:, None] == 2 * np.arange(tn_ff)[None, :], dtype=BF16)
    n_plan = len(plan)
    hbm = pl.BlockSpec(memory_space=pl.ANY)
    weight_stream = lambda rows, cols: [pltpu.VMEM((2, rows, cols), F32), pltpu.SemaphoreType.DMA((2,))]

    def row_tile(j, i, te, meta, *_):
        return (_tile_clamp(i, meta), 0)

    def expert_cols(j, i, te, meta, *_):
        return (te[_tile_clamp(i, meta)], 0, j)

    gu_spec = pltpu.PrefetchScalarGridSpec(
        num_scalar_prefetch=n_plan, grid=(ff // tn_ff, n_tiles),
        in_specs=[
            pl.BlockSpec((tm, d), row_tile),
            hbm,
            pl.BlockSpec((None, 1, tc), expert_cols),
            pl.BlockSpec((tc, tn_ff), lambda j, i, *_: (0, 0)),
        ],
        out_specs=pl.BlockSpec((tm, tn_ff), lambda j, i, *_: (i, j)),
        scratch_shapes=[pltpu.VMEM((d, tc), BF16)] + weight_stream(d, tc))
    hidden = pl.pallas_call(
        _gateup_kernel, grid_spec=gu_spec, out_shape=jax.ShapeDtypeStruct((r_max, ff), BF16),
        compiler_params=_cparams(2), name="expert_gate_up",
    )(*plan, xs, w_gate_up, b_gate_up, p_even)
    dn_spec = pltpu.PrefetchScalarGridSpec(
        num_scalar_prefetch=n_plan, grid=(d // tn_d, n_tiles),
        in_specs=[
            pl.BlockSpec((tm, ff), row_tile),
            hbm,
            pl.BlockSpec((None, 1, tn_d), expert_cols),
            pl.BlockSpec((tm, 1), row_tile),
        ],
        out_specs=pl.BlockSpec((tm, tn_d // LANES, LANES), lambda j, i, *_: (i, j, 0)),
        scratch_shapes=[pltpu.VMEM((ff, tn_d), BF16), pltpu.VMEM((tm * (tn_d // LANES), LANES), F32)]
        + weight_stream(ff, tn_d))
    return pl.pallas_call(
        _down_kernel, grid_spec=dn_spec, out_shape=jax.ShapeDtypeStruct((r_max, d // LANES, LANES), F32),
        compiler_params=_cparams(2), name="expert_down",
    )(*plan, hidden, w_down, bd, row_w)


def _combine_kernel(dest_hbm, rows_hbm, x_ref, o_ref, dest_smem, buf, sum_scr, isem, rsem):
    tm = COMBINE_TM
    chunks = rows_hbm.shape[1]
    pitch = sum_scr.shape[0] // tm
    seg = tm * pitch

    @pl.when(pl.program_id(0) == 0)
    def _():
        buf[...] = jnp.zeros_like(buf)

    def start_rows(step, idx_slot, row_slot):
        def body(r, c):
            for k in range(TOP_K):
                src = dest_smem[idx_slot, r * TOP_K + k]
                off = pl.multiple_of((k * tm + r) * pitch, math.gcd(pitch, SUBLANES))
                pltpu.make_async_copy(rows_hbm.at[src], buf.at[row_slot, pl.ds(off, chunks)],
                                      rsem.at[row_slot]).start()
            return c
        lax.fori_loop(0, tm, body, 0, unroll=2)

    def wait_rows(row_slot):
        landed = buf.at[row_slot, pl.ds(0, TOP_K * tm * chunks)]
        pltpu.make_async_copy(landed, landed, rsem.at[row_slot]).wait()

    row_slot = _row_prefetch_ring(dest_hbm, dest_smem, isem, start_rows, wait_rows)

    acc = buf[row_slot, pl.ds(0, seg), :]
    for k in range(1, TOP_K):
        acc = acc + buf[row_slot, pl.ds(k * seg, seg), :]
    sum_scr[...] = acc
    for c in range(chunks):
        cols = slice(c * LANES, (c + 1) * LANES)
        o_ref[:, cols] = x_ref[:, cols] + sum_scr[pl.ds(c, tm, stride=pitch), :]


def _combine(rows, dest, x2d):
    t, d = x2d.shape
    chunks = rows.shape[1]
    tm = COMBINE_TM
    n_steps = t // tm
    dest2d = dest.reshape(n_steps, tm * TOP_K)
    blk = pl.BlockSpec((tm, d), lambda i: (i, 0))
    pitch = chunks + SUBLANES if chunks % (2 * SUBLANES) == 0 else chunks
    return pl.pallas_call(
        _combine_kernel, grid=(n_steps,),
        in_specs=[pl.BlockSpec(memory_space=pl.ANY), pl.BlockSpec(memory_space=pl.ANY), blk],
        out_specs=blk,
        out_shape=jax.ShapeDtypeStruct((t, d), F32),
        scratch_shapes=[pltpu.SMEM((INDEX_SLOTS, tm * TOP_K), jnp.int32),
                        pltpu.VMEM((ROW_SLOTS, TOP_K * tm * pitch, LANES), F32),
                        pltpu.VMEM((tm * pitch, LANES), F32),
                        pltpu.SemaphoreType.DMA((INDEX_SLOTS,)), pltpu.SemaphoreType.DMA((ROW_SLOTS,))],
        compiler_params=_cparams(1), name="combine",
    )(dest2d, rows, x2d)


def _moe(x_mid, g2, w_router, b_router, w_gate_up, b_gate_up, w_down, b_down, tn_ff, tn_d):
    t, d = x_mid.shape
    ne = w_router.shape[1]
    h2, idx_t, gate_t, rank_t, cnt = _router(x_mid, g2, w_router, b_router)

    tm = EXPERT_TM
    r_max = t * TOP_K + ne * tm
    n_tiles = r_max // tm
    counts = cnt[0, :ne].astype(jnp.int32)
    padded = (counts + tm - 1) // tm * tm
    pend = jnp.cumsum(padded)
    pstart = pend - padded
    idx = idx_t[:, :TOP_K]
    dest = (pstart[idx] + rank_t[:, :TOP_K]).astype(jnp.int32)
    n_rows_used = pend[-1:].astype(jnp.int32)
    n_active = n_rows_used // tm
    tok = jnp.repeat(jnp.arange(t, dtype=jnp.int32), TOP_K)
    gate_bits = lax.bitcast_convert_type(gate_t[:, :TOP_K].reshape(-1), jnp.int32)
    empty = jnp.broadcast_to(jnp.asarray([-1, 0], jnp.int32), (r_max, 2))
    table = empty.at[dest.reshape(-1)].set(
        jnp.stack([tok, gate_bits], axis=1), unique_indices=True)
    row_tok = table[:, 0]
    row_w = lax.bitcast_convert_type(table[:, 1], F32)[:, None]
    tile_start = jnp.arange(n_tiles, dtype=jnp.int32) * tm
    tile_e = jnp.minimum(jnp.sum((pend[None, :] <= tile_start[:, None]).astype(jnp.int32), axis=1), ne - 1)
    tile_valid = jnp.clip((pstart + counts)[tile_e] - tile_start, 0, tm).astype(jnp.int32)
    present = counts > 0
    eids = jnp.arange(ne, dtype=jnp.int32)
    later = present[None, :] & (eids[None, :] > eids[:, None])
    next_e = jnp.min(jnp.where(later, eids[None, :], ne), axis=1)
    next_e = jnp.where(next_e == ne, -1, next_e).astype(jnp.int32)
    rank_e = (jnp.cumsum(present.astype(jnp.int32)) - 1).astype(jnp.int32)
    meta = jnp.stack([n_active[0], jnp.sum(present.astype(jnp.int32)),
                      jnp.argmax(present).astype(jnp.int32)]).astype(jnp.int32)
    plan = (tile_e, meta, tile_valid, next_e[tile_e], rank_e[tile_e])

    xs = _gather_rows(h2, row_tok)

    rows = _experts(xs, row_w, plan, w_gate_up, b_gate_up[:, None, :], w_down, b_down[:, None, :], tn_ff, tn_d)
    return _combine(rows, dest, x_mid)


def _alibi_slopes(n):
    return (LOG2E * 2.0 ** (-ALIBI_MAX_BIAS * (np.arange(n, dtype=np.float64) + 1.0) / n)).astype(np.float32)


def _mixer(x2d, batch, seq, norm1_g, w_in, q_norm_swa, k_norm_swa, q_norm_dil, k_norm_dil, sinks, w_out,
           n_swa, n_kv, n_dil, tm_proj, tn_out):
    t, d = x2d.shape
    swa_q, swa_kv, dil_w = n_swa * HEAD_DIM, n_kv * HEAD_DIM, n_dil * HEAD_DIM
    assert n_swa // n_kv == GQA_GROUP and seq % (16 * ATTN_BLOCK) == 0
    s1, s2, s3 = swa_q, swa_q + swa_kv, swa_q + 2 * swa_kv

    def dup(wc):
        wc = wc.reshape(d, n_kv, HEAD_DIM)
        return jnp.concatenate([wc, wc], axis=-1).reshape(d, 2 * swa_kv)

    w = jnp.concatenate([w_in[:, :s1], dup(w_in[:, s1:s2]), dup(w_in[:, s2:s3]), w_in[:, s3:]],
                        axis=1).astype(BF16)
    scale = HEAD_DIM ** -0.5 * LOG2E
    ones = lambda n: jnp.ones((n,), F32)
    gains = jnp.concatenate([
        jnp.tile(q_norm_swa * scale, n_swa), jnp.tile(k_norm_swa, 2 * n_kv), ones(2 * swa_kv),
        jnp.tile(q_norm_dil * scale, n_dil), jnp.tile(k_norm_dil, n_dil), ones(dil_w)])[None, :]
    c = w.shape[1]
    bounds = np.cumsum([0, swa_q, 2 * swa_kv, 2 * swa_kv, dil_w, dil_w, dil_w])
    assert all(b % PROJ_TN == 0 for b in bounds)
    seg_norm = [1, 1, 0, 1, 1, 0]
    flags = np.zeros((c // PROJ_TN,), np.int32)
    for sidx in range(6):
        flags[bounds[sidx] // PROJ_TN:bounds[sidx + 1] // PROJ_TN] = seg_norm[sidx]
    qa0, ka0, va0, qb0, kb0, vb0 = (int(b) for b in bounds[:6])
    proj, *residue_major = _inproj(x2d, norm1_g[None, :], w, gains, jnp.asarray(flags), tm_proj, dil_col0=qb0)
    by_dilation = dict(zip(RESIDUE_DILATIONS, residue_major))

    sink_row = jnp.repeat(sinks.astype(F32), HEAD_DIM)[None, :]
    o_a = _band_attention(proj[None], jnp.asarray(_alibi_slopes(n_swa)), batch=batch, seq=seq, dilation=1,
                          q_col0=qa0, k_col0=ka0, v_col0=va0, n_heads=n_swa, kv_shared=True,
                          max_diff=SWA_WINDOW - 1, sinks=sink_row)[0]
    slopes_dil = jnp.asarray(_alibi_slopes(n_dil))
    outs, lses = [], []
    for window, dil in DILATED_BRANCHES:
        src, col0 = (proj[None], qb0) if dil == 1 else (by_dilation[dil], 0)
        o_i, lse_i = _band_attention(src, slopes_dil, batch=batch, seq=seq, dilation=dil,
                                     q_col0=col0, k_col0=col0 + kb0 - qb0, v_col0=col0 + vb0 - qb0,
                                     n_heads=n_dil, kv_shared=False, max_diff=window // dil)
        outs.append(o_i)
        lses.append(lse_i)
    o_b = _merge(outs, lses, [dil for _, dil in DILATED_BRANCHES])
    return _outproj(o_a, o_b, w_out.astype(BF16), x2d, min(2 * tm_proj, t), tn_out)


def kernel(x, norm1_g, w_in, q_norm_swa, k_norm_swa, q_norm_dil, k_norm_dil, sinks, w_out, norm2_g,
           w_router, b_router, w_gate_up, b_gate_up, w_down, b_down):
    b, s, d = x.shape
    depth = norm1_g.shape[0]
    n_heads = d // HEAD_DIM
    n_swa = n_heads // 2
    n_kv = n_swa // GQA_GROUP
    n_dil = n_heads - n_swa
    x2d = x.reshape(b * s, d)
    for l in range(depth):
        x_mid = _mixer(x2d, b, s, norm1_g[l], w_in[l], q_norm_swa[l], k_norm_swa[l], q_norm_dil[l],
                       k_norm_dil[l], sinks[l], w_out[l], n_swa, n_kv, n_dil, tm_proj=512, tn_out=1024)
        x2d = _moe(x_mid, norm2_g[l][None, :], w_router[l], b_router[l], w_gate_up[l], b_gate_up[l],
                   w_down[l], b_down[l], tn_ff=512, tn_d=1024)
    return x2d.reshape(b, s, d)
```

```python
import functools
import math

import jax
import jax.numpy as jnp
import numpy as np
from jax import lax
from jax.experimental import pallas as pl
from jax.experimental.pallas import tpu as pltpu

F32 = jnp.float32
BF16 = jnp.bfloat16

HEAD_DIM = 64
LANES = 128
SUBLANES = 8
ATTN_BLOCK = 128
GQA_GROUP = 8
SWA_WINDOW = 128
DILATED_BRANCHES = ((128, 1), (512, 4), (2048, 16))
N_EXPERTS = 32
TOP_K = 4
SWIGLU_LIMIT = 7.0
SWIGLU_ALPHA = 1.702
ALIBI_MAX_BIAS = 8.0
EPS = 1e-6
MASK_DIST = 1e30
LOG2E = 1.4426950408889634
LN2 = 0.6931471805599453
NEG_BIG = -1e30

VMEM_LIMIT = 58 * 1024 * 1024

PROJ_TN = 512
ATTN_W_CHOICES = (2048, 1024, 512)
PAIRS_PER_KV = GQA_GROUP // 2
EXPERT_TM = 512
GATHER_ROWS = 512
COMBINE_TM = 128


def _cparams(n_axes):
    return pltpu.CompilerParams(dimension_semantics=("arbitrary",) * n_axes,
                                vmem_limit_bytes=VMEM_LIMIT)


RESIDUE_DILATIONS = tuple(d for _, d in DILATED_BRANCHES if d > 1)


def _inproj_kernel(flag_ref, x_ref, g_ref, w_ref, gain_ref, ones_ref, o_ref, *rest, dil_tile0):
    res_refs, (h_scr, y_scr) = rest[:len(RESIDUE_DILATIONS)], rest[len(RESIDUE_DILATIONS):]
    j = pl.program_id(1)
    tm, tn = o_ref.shape

    @pl.when(j == 0)
    def _():
        x = x_ref[...]
        ms = jnp.mean(x * x, axis=-1, keepdims=True)
        h_scr[...] = (x * lax.rsqrt(ms + EPS) * g_ref[...]).astype(BF16)

    acc = jnp.dot(h_scr[...], w_ref[...], preferred_element_type=F32)

    def emit(y):
        o_ref[...] = y.astype(BF16)
        for ch in range(tn // LANES):
            y_scr[ch] = y[:, ch * LANES:(ch + 1) * LANES]

    @pl.when(flag_ref[j] == 1)
    def _():
        ss = jnp.dot((acc * acc).astype(BF16), ones_ref[...], preferred_element_type=F32)
        emit(acc * lax.rsqrt(ss * (1.0 / HEAD_DIM) + EPS) * gain_ref[...])

    @pl.when(flag_ref[j] == 0)
    def _():
        emit(acc)

    @pl.when(j >= dil_tile0)
    def _():
        for d, ref in zip(RESIDUE_DILATIONS, res_refs):
            for r in range(d):
                for ch in range(tn // LANES):
                    rows = y_scr[ch, pl.ds(r, tm // d, stride=d), :]
                    ref[r, :, ch * LANES:(ch + 1) * LANES] = rows.astype(BF16)


def _inproj(x2d, g1, w, gains, flags, tm, dil_col0):
    t, d = x2d.shape
    c = w.shape[1]
    tn = PROJ_TN
    dil_tile0 = dil_col0 // tn
    cb = c - dil_col0
    head_id = np.arange(tn) // HEAD_DIM
    ones_bd = jnp.asarray(head_id[:, None] == head_id[None, :], dtype=BF16)
    res_specs = [pl.BlockSpec((dd, tm // dd, tn), lambda i, j, f: (0, i, jnp.maximum(j - dil_tile0, 0)))
                 for dd in RESIDUE_DILATIONS]
    res_shapes = [jax.ShapeDtypeStruct((dd, t // dd, cb), BF16) for dd in RESIDUE_DILATIONS]
    grid_spec = pltpu.PrefetchScalarGridSpec(
        num_scalar_prefetch=1,
        grid=(t // tm, c // tn),
        in_specs=[
            pl.BlockSpec((tm, d), lambda i, j, f: (i, 0)),
            pl.BlockSpec((1, d), lambda i, j, f: (0, 0)),
            pl.BlockSpec((d, tn), lambda i, j, f: (0, j)),
            pl.BlockSpec((1, tn), lambda i, j, f: (0, j)),
            pl.BlockSpec((tn, tn), lambda i, j, f: (0, 0)),
        ],
        out_specs=[pl.BlockSpec((tm, tn), lambda i, j, f: (i, j))] + res_specs,
        scratch_shapes=[pltpu.VMEM((tm, d), BF16), pltpu.VMEM((tn // LANES, tm, LANES), F32)],
    )
    return pl.pallas_call(
        functools.partial(_inproj_kernel, dil_tile0=dil_tile0), grid_spec=grid_spec,
        out_shape=[jax.ShapeDtypeStruct((t, c), BF16)] + res_shapes,
        compiler_params=_cparams(2), name="inproj",
    )(flags, x2d, g1, w, gains, ones_bd)


def _attn_kernel(slope_ref, q_ref, kp_ref, kc_ref, vp_ref, vc_ref, *rest,
                 n_pairs, kv_shared, max_diff, dist_scale, with_sink):
    if with_sink:
        sink_ref, o_ref = rest
        lse_ref = None
    else:
        o_ref, lse_ref = rest
    blk = pl.program_id(2)
    cg = pl.program_id(3)

    qi = lax.broadcasted_iota(jnp.int32, (ATTN_BLOCK, 2 * ATTN_BLOCK), 0)
    kj = lax.broadcasted_iota(jnp.int32, (ATTN_BLOCK, 2 * ATTN_BLOCK), 1)
    dist = qi + ATTN_BLOCK - kj
    valid = (dist >= 0) & (dist <= max_diff) & ((blk > 0) | (kj >= ATTN_BLOCK))
    dist_m = jnp.where(valid, (dist * dist_scale).astype(F32), MASK_DIST)

    lane = lax.broadcasted_iota(jnp.int32, (ATTN_BLOCK, LANES), 1)
    low = lane < HEAD_DIM
    if not with_sink:
        @pl.when(cg == 0)
        def _():
            lse_ref[...] = jnp.zeros_like(lse_ref)
        lse_tile = lse_ref[...]

    for p in range(n_pairs):
        cols = slice(p * LANES, (p + 1) * LANES)
        kv = p // PAIRS_PER_KV
        kcols = slice(kv * LANES, (kv + 1) * LANES) if kv_shared else cols
        q2 = q_ref[:, cols]
        kk = jnp.concatenate([kp_ref[:, kcols], kc_ref[:, kcols]], axis=0)
        vv = jnp.concatenate([vp_ref[:, kcols], vc_ref[:, kcols]], axis=0)
        outs, lses = [], []
        for hh in range(2):
            slope = slope_ref[cg * (2 * n_pairs) + 2 * p + hh]
            qm = jnp.where(low if hh == 0 else ~low, q2, jnp.zeros_like(q2))
            s = lax.dot_general(qm, kk, (((1,), (1,)), ((), ())),
                                preferred_element_type=F32)
            s = s - slope * dist_m
            m = jnp.max(s, axis=-1, keepdims=True)
            e = jnp.exp2(s - m)
            l = jnp.sum(e, axis=-1, keepdims=True)
            o = jnp.dot(e.astype(BF16), vv, preferred_element_type=F32)
            outs.append(o / l)
            lse = (m + jnp.log2(l)) * LN2
            if with_sink:
                lses.append(jnp.broadcast_to(lse, (ATTN_BLOCK, LANES)))
            else:
                head = cg * (2 * n_pairs) + 2 * p + hh
                lse_tile = jnp.where(lane == head, lse, lse_tile)
        o2 = jnp.where(low, outs[0], outs[1])
        if with_sink:
            lse2 = jnp.where(low, lses[0], lses[1])
            o2 = o2 * jax.nn.sigmoid(lse2 - sink_ref[:, cols])
        o_ref[:, cols] = o2.astype(o_ref.dtype)
    if not with_sink:
        lse_ref[...] = lse_tile


def _band_attention(proj, slopes, *, batch, seq, dilation, q_col0, k_col0, v_col0, n_heads,
                    kv_shared, max_diff, sinks=None):
    d = dilation
    assert proj.shape[0] == d
    lsub = seq // d
    nblk = lsub // ATTN_BLOCK
    width = n_heads * HEAD_DIM

    def kv_width(w):
        return LANES * (w // LANES // PAIRS_PER_KV) if kv_shared else w

    w = next(c for c in ATTN_W_CHOICES
             if width % c == 0 and q_col0 % c == 0 and k_col0 % kv_width(c) == 0 and v_col0 % kv_width(c) == 0)
    n_pairs = w // LANES
    ncg = width // w
    kw = kv_width(w)
    pv = proj

    def qmap(b, r, i, g, s):
        return (r, b * nblk + i, q_col0 // w + g)

    def kvmap(col0, prev):
        def f(b, r, i, g, s):
            blk = jnp.maximum(i - 1, 0) if prev else i
            return (r, b * nblk + blk, col0 // kw + g)
        return f

    def omap(b, r, i, g, s):
        return (r, b * nblk + i, g)

    in_specs = [
        pl.BlockSpec((None, ATTN_BLOCK, w), qmap),
        pl.BlockSpec((None, ATTN_BLOCK, kw), kvmap(k_col0, True)),
        pl.BlockSpec((None, ATTN_BLOCK, kw), kvmap(k_col0, False)),
        pl.BlockSpec((None, ATTN_BLOCK, kw), kvmap(v_col0, True)),
        pl.BlockSpec((None, ATTN_BLOCK, kw), kvmap(v_col0, False)),
    ]
    args = [pv, pv, pv, pv, pv]
    with_sink = sinks is not None
    o_shape = jax.ShapeDtypeStruct((d, batch * lsub, width), BF16)
    o_spec = pl.BlockSpec((None, ATTN_BLOCK, w), omap)
    if with_sink:
        in_specs.append(pl.BlockSpec((1, w), lambda b, r, i, g, s: (0, g)))
        args.append(sinks)
        out_shape, out_specs = o_shape, o_spec
    else:
        assert n_heads <= LANES
        out_shape = (o_shape, jax.ShapeDtypeStruct((d, batch * lsub, LANES), F32))
        out_specs = (o_spec, pl.BlockSpec((None, ATTN_BLOCK, LANES),
                                          lambda b, r, i, g, s: (r, b * nblk + i, 0)))
    kern = functools.partial(_attn_kernel, n_pairs=n_pairs, kv_shared=kv_shared,
                             max_diff=max_diff, dist_scale=d, with_sink=with_sink)
    grid_spec = pltpu.PrefetchScalarGridSpec(
        num_scalar_prefetch=1, grid=(batch, d, nblk, ncg),
        in_specs=in_specs, out_specs=out_specs)
    res = pl.pallas_call(kern, grid_spec=grid_spec, out_shape=out_shape,
                         compiler_params=_cparams(4), name=f"band_attn_d{d}" + ("_sink" if with_sink else ""),
                         )(slopes, *args)
    return res


def _merge_kernel(*refs, dilations):
    n = len(dilations)
    o_refs, l_refs = refs[:n], refs[n:2 * n]
    spread_ref, out, scr, lscr = refs[2 * n:]
    tm, w = out.shape
    nch = w // LANES
    for a, d in enumerate(dilations):
        if d == 1:
            continue
        for r in range(d):
            lscr[a, pl.ds(r, tm // d, stride=d), :] = l_refs[a][r]
            for ch in range(nch):
                scr[a, ch, pl.ds(r, tm // d, stride=d), :] = (
                    o_refs[a][r, :, ch * LANES:(ch + 1) * LANES].astype(F32))
    ls_ = [l_refs[a][0] if d == 1 else lscr[a] for a, d in enumerate(dilations)]
    m = functools.reduce(jnp.maximum, ls_)
    es = [jnp.exp(l - m) for l in ls_]
    den = functools.reduce(lambda p, q: p + q, es)
    parts = []
    for e in es:
        wgt = e / den
        hi = wgt.astype(BF16)
        parts.append((hi, (wgt - hi.astype(F32)).astype(BF16)))
    span = 2 * LANES
    for c0 in range(0, w, span):
        sel = spread_ref[:, c0:c0 + span]
        nums = [None] * (span // LANES)
        for a, d in enumerate(dilations):
            hi, lo = parts[a]
            wide = jnp.dot(hi, sel, preferred_element_type=F32) + jnp.dot(lo, sel, preferred_element_type=F32)
            for k in range(span // LANES):
                ch = c0 // LANES + k
                o = o_refs[a][0, :, ch * LANES:(ch + 1) * LANES].astype(F32) if d == 1 else scr[a, ch]
                term = wide[:, k * LANES:(k + 1) * LANES] * o
                nums[k] = term if nums[k] is None else nums[k] + term
        for k in range(span // LANES):
            ch = c0 // LANES + k
            out[:, ch * LANES:(ch + 1) * LANES] = nums[k].astype(out.dtype)


def _merge(os_, ls_, dilations, tm=256):
    w = os_[0].shape[2]
    t = os_[0].shape[0] * os_[0].shape[1]
    o_specs = [pl.BlockSpec((d, tm // d, w), lambda i: (0, i, 0)) for d in dilations]
    l_specs = [pl.BlockSpec((d, tm // d, LANES), lambda i: (0, i, 0)) for d in dilations]
    spread = jnp.asarray(np.arange(LANES)[:, None] == np.arange(w)[None, :] // HEAD_DIM, dtype=BF16)
    return pl.pallas_call(
        functools.partial(_merge_kernel, dilations=tuple(dilations)), grid=(t // tm,),
        in_specs=o_specs + l_specs + [pl.BlockSpec((LANES, w), lambda i: (0, 0))],
        out_specs=pl.BlockSpec((tm, w), lambda i: (i, 0)),
        out_shape=jax.ShapeDtypeStruct((t, w), BF16),
        scratch_shapes=[pltpu.VMEM((len(dilations), w // LANES, tm, LANES), F32),
                        pltpu.VMEM((len(dilations), tm, LANES), F32)],
        compiler_params=_cparams(1), name="branch_merge",
    )(*os_, *ls_, spread)


def _outproj_kernel(oa_ref, ob_ref, wt_ref, wb_ref, x_ref, o_ref):
    acc = jnp.dot(oa_ref[...], wt_ref[...], preferred_element_type=F32)
    acc = acc + jnp.dot(ob_ref[...], wb_ref[...], preferred_element_type=F32)
    o_ref[...] = x_ref[...] + acc


def _outproj(o_a, o_b, w_out, x2d, tm, tn):
    t, d = x2d.shape
    ha, hb = o_a.shape[1], o_b.shape[1]
    assert ha == hb
    return pl.pallas_call(
        _outproj_kernel, grid=(t // tm, d // tn),
        in_specs=[
            pl.BlockSpec((tm, ha), lambda i, j: (i, 0)),
            pl.BlockSpec((tm, hb), lambda i, j: (i, 0)),
            pl.BlockSpec((ha, tn), lambda i, j: (0, j)),
            pl.BlockSpec((hb, tn), lambda i, j: (1, j)),
            pl.BlockSpec((tm, tn), lambda i, j: (i, j)),
        ],
        out_specs=pl.BlockSpec((tm, tn), lambda i, j: (i, j)),
        out_shape=jax.ShapeDtypeStruct((t, d), F32), compiler_params=_cparams(2), name="outproj",
    )(o_a, o_b, w_out, w_out, x2d)


def _router_kernel(x_ref, g_ref, whi_ref, wlo_ref, b_ref, tri_ref,
                   h_ref, idx_ref, gate_ref, rank_ref, cnt_ref, carry, h_scr):
    i = pl.program_id(0)

    @pl.when(i == 0)
    def _():
        carry[...] = jnp.zeros_like(carry)

    x = x_ref[...]
    ms = jnp.mean(x * x, axis=-1, keepdims=True)
    h = x * lax.rsqrt(ms + EPS) * g_ref[...]
    h_hi = h.astype(BF16)
    h_lo = (h - h_hi.astype(F32)).astype(BF16)
    u = pltpu.bitcast(h, jnp.uint32)
    r = (u + jnp.uint32(0x7FFF) + ((u >> 16) & jnp.uint32(1))) >> 16
    tok, words, _ = h_ref.shape
    for s in range(words):
        lo = r[:, 2 * s * LANES:(2 * s + 1) * LANES]
        hi = r[:, (2 * s + 1) * LANES:(2 * s + 2) * LANES]
        h_scr[pl.ds(s, tok, stride=words), :] = lo | (hi << 16)
    h_ref[...] = h_scr[...].reshape(tok, words, LANES)
    logits = (jnp.dot(h_hi, whi_ref[...], preferred_element_type=F32)
              + jnp.dot(h_lo, whi_ref[...], preferred_element_type=F32)
              + jnp.dot(h_hi, wlo_ref[...], preferred_element_type=F32)) + b_ref[...]

    tm = x.shape[0]
    lane = lax.broadcasted_iota(jnp.int32, (tm, LANES), 1).astype(F32)
    work = logits
    multihot = jnp.zeros((tm, LANES), F32)
    vals, idxs = [], []
    for _ in range(TOP_K):
        m = jnp.max(work, axis=-1, keepdims=True)
        ik = jnp.min(jnp.where(work == m, lane, float(LANES)), axis=-1, keepdims=True)
        sel = lane == ik
        work = jnp.where(sel, -jnp.inf, work)
        multihot = jnp.where(sel, 1.0, multihot)
        vals.append(m)
        idxs.append(ik)
    es = [jnp.exp(v - vals[0]) for v in vals]
    denom = es[0] + es[1] + es[2] + es[3]

    cum = jnp.dot(tri_ref[...], multihot.astype(BF16), preferred_element_type=F32) + carry[0:1, :]
    idx_t = jnp.zeros((tm, LANES), F32)
    gate_t = jnp.zeros((tm, LANES), F32)
    rank_t = jnp.zeros((tm, LANES), F32)
    for k in range(TOP_K):
        rk = jnp.sum(jnp.where(lane == idxs[k], cum, 0.0), axis=-1, keepdims=True)
        here = lane == float(k)
        idx_t = jnp.where(here, idxs[k], idx_t)
        gate_t = jnp.where(here, es[k] / denom, gate_t)
        rank_t = jnp.where(here, rk, rank_t)
    idx_ref[...] = idx_t.astype(jnp.int32)
    gate_ref[...] = gate_t
    rank_ref[...] = rank_t.astype(jnp.int32)
    new_carry = carry[0:1, :] + jnp.sum(multihot, axis=0, keepdims=True)
    carry[...] = jnp.broadcast_to(new_carry, carry.shape)
    cnt_ref[...] = jnp.broadcast_to(new_carry, cnt_ref.shape)


def _router(x2d, g2, w_router, b_router, tm=512):
    t, d = x2d.shape
    ne = w_router.shape[1]
    w_pad = jnp.zeros((d, LANES), F32).at[:, :ne].set(w_router)
    w_hi = w_pad.astype(BF16)
    w_lo = (w_pad - w_hi.astype(F32)).astype(BF16)
    b_pad = jnp.full((1, LANES), NEG_BIG, F32).at[0, :ne].set(b_router)
    tri = jnp.asarray(np.tril(np.ones((tm, tm), np.float32), -1), dtype=BF16)
    tile = lambda dt: jax.ShapeDtypeStruct((t, LANES), dt)
    row = pl.BlockSpec((tm, LANES), lambda i: (i, 0))
    const = lambda shape: pl.BlockSpec(shape, lambda i: (0, 0))
    return pl.pallas_call(
        _router_kernel, grid=(t // tm,),
        in_specs=[pl.BlockSpec((tm, d), lambda i: (i, 0)), const((1, d)), const((d, LANES)),
                  const((d, LANES)), const((1, LANES)), const((tm, tm))],
        out_specs=(pl.BlockSpec((tm, d // (2 * LANES), LANES), lambda i: (i, 0, 0)), row, row, row,
                   const((8, LANES))),
        out_shape=(jax.ShapeDtypeStruct((t, d // (2 * LANES), LANES), jnp.uint32),
                   tile(jnp.int32), tile(F32), tile(jnp.int32),
                   jax.ShapeDtypeStruct((8, LANES), F32)),
        scratch_shapes=[pltpu.VMEM((8, LANES), F32), pltpu.VMEM((tm * (d // (2 * LANES)), LANES), jnp.uint32)],
        compiler_params=_cparams(1), name="router",
    )(x2d, g2, w_hi, w_lo, b_pad, tri)


INDEX_SLOTS = 3
ROW_SLOTS = 2


def _row_prefetch_ring(idx_hbm, idx_smem, isem, start_rows, wait_rows):
    i = pl.program_id(0)
    n = pl.num_programs(0)

    def idx_copy(step):
        slot = lax.rem(step, INDEX_SLOTS)
        return pltpu.make_async_copy(idx_hbm.at[step], idx_smem.at[slot], isem.at[slot])

    @pl.when(i == 0)
    def _():
        idx_copy(0).start()
        idx_copy(0).wait()
        start_rows(0, 0, 0)

        @pl.when(n > 1)
        def _():
            idx_copy(1).start()

    @pl.when(i + 1 < n)
    def _():
        idx_copy(i + 1).wait()
        start_rows(i + 1, lax.rem(i + 1, INDEX_SLOTS), lax.rem(i + 1, ROW_SLOTS))

    @pl.when(i + 2 < n)
    def _():
        idx_copy(i + 2).start()

    row_slot = lax.rem(i, ROW_SLOTS)
    wait_rows(row_slot)
    return row_slot


def _gather_kernel(nvalid_ref, tok_hbm, h_hbm, xs_ref, tok_smem, buf, isem, rsem):
    g = GATHER_ROWS
    words = h_hbm.shape[1]

    @pl.when(pl.program_id(0) == 0)
    def _():
        buf[...] = jnp.zeros_like(buf)

    def row_copy(t, r, row_slot):
        dst = buf.at[row_slot, pl.ds(pl.multiple_of(r * words, words), words)]
        return pltpu.make_async_copy(h_hbm.at[t], dst, rsem.at[row_slot])

    def start_rows(step, idx_slot, row_slot):
        def body(r2, c):
            for prio in range(2):
                r = 2 * r2 + prio
                t = tok_smem[idx_slot, r]

                @pl.when(t >= 0)
                def _():
                    row_copy(t, r, row_slot).start(priority=prio)
            return c
        lax.fori_loop(0, g // 2, body, 0)

    def wait_rows(row_slot):
        nvalid = nvalid_ref[pl.program_id(0)]

        @pl.when(nvalid == g)
        def _():
            pltpu.make_async_copy(buf.at[row_slot], buf.at[row_slot], rsem.at[row_slot]).wait()

        @pl.when(nvalid < g)
        def _():
            def body(r, c):
                row_copy(0, 0, row_slot).wait()
                return c
            lax.fori_loop(0, nvalid, body, 0)

    row_slot = _row_prefetch_ring(tok_hbm, tok_smem, isem, start_rows, wait_rows)

    for s in range(words):
        x = buf[row_slot, pl.ds(s, g, stride=words), :]
        lo = pltpu.bitcast(x << 16, F32)
        hi = pltpu.bitcast(x & jnp.uint32(0xFFFF0000), F32)
        xs_ref[:, 2 * s * LANES:(2 * s + 1) * LANES] = lo.astype(BF16)
        xs_ref[:, (2 * s + 1) * LANES:(2 * s + 2) * LANES] = hi.astype(BF16)


def _gather_rows(h_packed, row_tok):
    t, words, _ = h_packed.shape
    r_max = row_tok.shape[0]
    g = GATHER_ROWS
    n_steps = r_max // g
    tok2d = row_tok.reshape(n_steps, g)
    nvalid = jnp.sum((tok2d >= 0).astype(jnp.int32), axis=1)
    grid_spec = pltpu.PrefetchScalarGridSpec(
        num_scalar_prefetch=1, grid=(n_steps,),
        in_specs=[pl.BlockSpec(memory_space=pl.ANY), pl.BlockSpec(memory_space=pl.ANY)],
        out_specs=pl.BlockSpec((g, 2 * words * LANES), lambda i, nv: (i, 0)),
        scratch_shapes=[pltpu.SMEM((INDEX_SLOTS, g), jnp.int32),
                        pltpu.VMEM((ROW_SLOTS, g * words, LANES), jnp.uint32),
                        pltpu.SemaphoreType.DMA((INDEX_SLOTS,)), pltpu.SemaphoreType.DMA((ROW_SLOTS,))])
    return pl.pallas_call(
        _gather_kernel, grid_spec=grid_spec,
        out_shape=jax.ShapeDtypeStruct((r_max, 2 * words * LANES), BF16),
        compiler_params=_cparams(1), name="row_gather",
    )(nvalid, tok2d, h_packed)


PREP_ROWS = 1024


def _tile_clamp(i, na):
    return jnp.minimum(i, na[0] - 1)


def _weights_changed(te_ref, na_ref, i):
    cur = _tile_clamp(i, na_ref)
    prev = jnp.maximum(cur - 1, 0)
    return (i < na_ref[0]) & ((i == 0) | (te_ref[cur] != te_ref[prev]))


def _stream_expert_weights(w_hbm, raw, wsem, te_ref, meta_ref, nxt_ref, eidx_ref, convert):
    j = pl.program_id(0)
    i = pl.program_id(1)
    nj = pl.num_programs(0)
    tc = raw.shape[2]
    cur = _tile_clamp(i, meta_ref)

    def copy(e, jj, slot):
        col = pl.multiple_of(jj * tc, tc)
        return pltpu.make_async_copy(w_hbm.at[e, :, pl.ds(col, tc)], raw.at[slot], wsem.at[slot])

    @pl.when(_weights_changed(te_ref, meta_ref, i))
    def _():
        e = te_ref[cur]
        b = j * meta_ref[1] + eidx_ref[cur]
        slot = lax.rem(b, 2)
        nxt = nxt_ref[cur]

        @pl.when(b == 0)
        def _():
            copy(e, j, 0).start()

        @pl.when(nxt >= 0)
        def _():
            copy(nxt, j, 1 - slot).start()

        @pl.when((nxt < 0) & (j + 1 < nj))
        def _():
            copy(meta_ref[2], j + 1, 1 - slot).start()

        copy(e, j, slot).wait()
        convert(raw.at[slot])


def _row_cases(na_ref, tv_ref, i, tm, compute, zero):
    active = i < na_ref[0]
    valid = tv_ref[_tile_clamp(i, na_ref)]

    @pl.when(active & (valid > tm // 2))
    def _():
        compute(tm)

    @pl.when(active & (valid <= tm // 2))
    def _():
        compute(tm // 2)

    @pl.when(jnp.logical_not(active))
    def _():
        zero()


def _cast_rows(dst, src):
    n = dst.shape[0]
    for r0 in range(0, n, PREP_ROWS):
        rows = slice(r0, min(r0 + PREP_ROWS, n))
        dst[rows, :] = src[rows, :].astype(dst.dtype)


def _gateup_kernel(te_ref, na_ref, tv_ref, nxt_ref, eidx_ref, xs_ref, w_hbm, b_ref, pe_ref, h_ref,
                   w_s, raw, wsem):
    i = pl.program_id(1)
    d, tc = w_s.shape
    tm = h_ref.shape[0]
    _stream_expert_weights(w_hbm, raw, wsem, te_ref, na_ref, nxt_ref, eidx_ref,
                           functools.partial(_cast_rows, w_s))

    def compute(nrows):
        gu = jnp.dot(xs_ref[0:nrows, :], w_s[...], preferred_element_type=F32) + b_ref[...]
        acts = []
        for c in range(tc // LANES):
            g = gu[:, c * LANES:(c + 1) * LANES]
            u = pltpu.roll(g, LANES - 1, axis=1)
            gate = jnp.minimum(g, SWIGLU_LIMIT)
            up = jnp.clip(u, -SWIGLU_LIMIT, SWIGLU_LIMIT)
            acts.append(((up + 1.0) * gate * jax.nn.sigmoid(SWIGLU_ALPHA * gate)).astype(BF16))
        act = jnp.concatenate(acts, axis=1)
        h_ref[0:nrows, :] = jnp.dot(act, pe_ref[...], preferred_element_type=F32).astype(h_ref.dtype)
        if nrows < tm:
            h_ref[nrows:tm, :] = jnp.zeros((tm - nrows, h_ref.shape[1]), h_ref.dtype)

    def zero():
        h_ref[...] = jnp.zeros_like(h_ref)

    _row_cases(na_ref, tv_ref, i, tm, compute, zero)


def _down_kernel(te_ref, na_ref, tv_ref, nxt_ref, eidx_ref, h_ref, w_hbm, bd_ref, rw_ref, o_ref,
                 wd_s, o_scr, raw, wsem):
    i = pl.program_id(1)
    tm, chunks, _ = o_ref.shape
    _stream_expert_weights(w_hbm, raw, wsem, te_ref, na_ref, nxt_ref, eidx_ref,
                           functools.partial(_cast_rows, wd_s))

    def compute(nrows):
        out = jnp.dot(h_ref[0:nrows, :], wd_s[...], preferred_element_type=F32)
        out = (out + bd_ref[...]) * rw_ref[0:nrows, :]
        for c in range(chunks):
            o_scr[pl.ds(c, nrows, stride=chunks), :] = out[:, c * LANES:(c + 1) * LANES]
        if nrows < tm:
            o_scr[nrows * chunks:tm * chunks, :] = jnp.zeros(((tm - nrows) * chunks, LANES), F32)
        o_ref[...] = o_scr[...].reshape(tm, chunks, LANES)

    def zero():
        o_ref[...] = jnp.zeros_like(o_ref)

    _row_cases(na_ref, tv_ref, i, tm, compute, zero)


def _experts(xs, row_w, plan, w_gate_up, b_gate_up, w_down, bd, tn_ff, tn_d):
    r_max, d = xs.shape
    ff = w_down.shape[1]
    tm = EXPERT_TM
    n_tiles = r_max // tm
    tc = 2 * tn_ff
    p_even = jnp.asarray(np.arange(tc)[:, None] == 2 * np.arange(tn_ff)[None, :], dtype=BF16)
    n_plan = len(plan)
    hbm = pl.BlockSpec(memory_space=pl.ANY)
    weight_stream = lambda rows, cols: [pltpu.VMEM((2, rows, cols), F32), pltpu.SemaphoreType.DMA((2,))]

    def row_tile(j, i, te, meta, *_):
        return (_tile_clamp(i, meta), 0)

    def expert_cols(j, i, te, meta, *_):
        return (te[_tile_clamp(i, meta)], 0, j)

    gu_spec = pltpu.PrefetchScalarGridSpec(
        num_scalar_prefetch=n_plan, grid=(ff // tn_ff, n_tiles),
        in_specs=[
            pl.BlockSpec((tm, d), row_tile),
            hbm,
            pl.BlockSpec((None, 1, tc), expert_cols),
            pl.BlockSpec((tc, tn_ff), lambda j, i, *_: (0, 0)),
        ],
        out_specs=pl.BlockSpec((tm, tn_ff), lambda j, i, *_: (i, j)),
        scratch_shapes=[pltpu.VMEM((d, tc), BF16)] + weight_stream(d, tc))
    hidden = pl.pallas_call(
        _gateup_kernel, grid_spec=gu_spec, out_shape=jax.ShapeDtypeStruct((r_max, ff), BF16),
        compiler_params=_cparams(2), name="expert_gate_up",
    )(*plan, xs, w_gate_up, b_gate_up, p_even)
    dn_spec = pltpu.PrefetchScalarGridSpec(
        num_scalar_prefetch=n_plan, grid=(d // tn_d, n_tiles),
        in_specs=[
            pl.BlockSpec((tm, ff), row_tile),
            hbm,
            pl.BlockSpec((None, 1, tn_d), expert_cols),
            pl.BlockSpec((tm, 1), row_tile),
        ],
        out_specs=pl.BlockSpec((tm, tn_d // LANES, LANES), lambda j, i, *_: (i, j, 0)),
        scratch_shapes=[pltpu.VMEM((ff, tn_d), BF16), pltpu.VMEM((tm * (tn_d // LANES), LANES), F32)]
        + weight_stream(ff, tn_d))
    return pl.pallas_call(
        _down_kernel, grid_spec=dn_spec, out_shape=jax.ShapeDtypeStruct((r_max, d // LANES, LANES), F32),
        compiler_params=_cparams(2), name="expert_down",
    )(*plan, hidden, w_down, bd, row_w)


def _combine_kernel(dest_hbm, rows_hbm, x_ref, o_ref, dest_smem, buf, sum_scr, isem, rsem):
    tm = COMBINE_TM
    chunks = rows_hbm.shape[1]
    pitch = sum_scr.shape[0] // tm
    seg = tm * pitch

    @pl.when(pl.program_id(0) == 0)
    def _():
        buf[...] = jnp.zeros_like(buf)

    def start_rows(step, idx_slot, row_slot):
        def body(r, c):
            for k in range(TOP_K):
                src = dest_smem[idx_slot, r * TOP_K + k]
                off = pl.multiple_of((k * tm + r) * pitch, math.gcd(pitch, SUBLANES))
                pltpu.make_async_copy(rows_hbm.at[src], buf.at[row_slot, pl.ds(off, chunks)],
                                      rsem.at[row_slot]).start(priority=k % 2)
            return c
        lax.fori_loop(0, tm, body, 0, unroll=4)

    def wait_rows(row_slot):
        landed = buf.at[row_slot, pl.ds(0, TOP_K * tm * chunks)]
        pltpu.make_async_copy(landed, landed, rsem.at[row_slot]).wait()

    row_slot = _row_prefetch_ring(dest_hbm, dest_smem, isem, start_rows, wait_rows)

    acc = buf[row_slot, pl.ds(0, seg), :]
    for k in range(1, TOP_K):
        acc = acc + buf[row_slot, pl.ds(k * seg, seg), :]
    sum_scr[...] = acc
    for c in range(chunks):
        cols = slice(c * LANES, (c + 1) * LANES)
        o_ref[:, cols] = x_ref[:, cols] + sum_scr[pl.ds(c, tm, stride=pitch), :]


def _combine(rows, dest, x2d):
    t, d = x2d.shape
    chunks = rows.shape[1]
    tm = COMBINE_TM
    n_steps = t // tm
    dest2d = dest.reshape(n_steps, tm * TOP_K)
    blk = pl.BlockSpec((tm, d), lambda i: (i, 0))
    pitch = chunks + SUBLANES if chunks % (2 * SUBLANES) == 0 else chunks
    return pl.pallas_call(
        _combine_kernel, grid=(n_steps,),
        in_specs=[pl.BlockSpec(memory_space=pl.ANY), pl.BlockSpec(memory_space=pl.ANY), blk],
        out_specs=blk,
        out_shape=jax.ShapeDtypeStruct((t, d), F32),
        scratch_shapes=[pltpu.SMEM((INDEX_SLOTS, tm * TOP_K), jnp.int32),
                        pltpu.VMEM((ROW_SLOTS, TOP_K * tm * pitch, LANES), F32),
                        pltpu.VMEM((tm * pitch, LANES), F32),
                        pltpu.SemaphoreType.DMA((INDEX_SLOTS,)), pltpu.SemaphoreType.DMA((ROW_SLOTS,))],
        compiler_params=_cparams(1), name="combine",
    )(dest2d, rows, x2d)


def _moe(x_mid, g2, w_router, b_router, w_gate_up, b_gate_up, w_down, b_down, tn_ff, tn_d):
    t, d = x_mid.shape
    ne = w_router.shape[1]
    h2, idx_t, gate_t, rank_t, cnt = _router(x_mid, g2, w_router, b_router)

    tm = EXPERT_TM
    r_max = t * TOP_K + ne * tm
    n_tiles = r_max // tm
    counts = cnt[0, :ne].astype(jnp.int32)
    padded = (counts + tm - 1) // tm * tm
    pend = jnp.cumsum(padded)
    pstart = pend - padded
    idx = idx_t[:, :TOP_K]
    dest = (pstart[idx] + rank_t[:, :TOP_K]).astype(jnp.int32)
    n_rows_used = pend[-1:].astype(jnp.int32)
    n_active = n_rows_used // tm
    tok = jnp.repeat(jnp.arange(t, dtype=jnp.int32), TOP_K)
    gate_bits = lax.bitcast_convert_type(gate_t[:, :TOP_K].reshape(-1), jnp.int32)
    empty = jnp.broadcast_to(jnp.asarray([-1, 0], jnp.int32), (r_max, 2))
    table = empty.at[dest.reshape(-1)].set(
        jnp.stack([tok, gate_bits], axis=1), unique_indices=True)
    row_tok = table[:, 0]
    row_w = lax.bitcast_convert_type(table[:, 1], F32)[:, None]
    tile_start = jnp.arange(n_tiles, dtype=jnp.int32) * tm
    tile_e = jnp.minimum(jnp.sum((pend[None, :] <= tile_start[:, None]).astype(jnp.int32), axis=1), ne - 1)
    tile_valid = jnp.clip((pstart + counts)[tile_e] - tile_start, 0, tm).astype(jnp.int32)
    present = counts > 0
    eids = jnp.arange(ne, dtype=jnp.int32)
    later = present[None, :] & (eids[None, :] > eids[:, None])
    next_e = jnp.min(jnp.where(later, eids[None, :], ne), axis=1)
    next_e = jnp.where(next_e == ne, -1, next_e).astype(jnp.int32)
    rank_e = (jnp.cumsum(present.astype(jnp.int32)) - 1).astype(jnp.int32)
    meta = jnp.stack([n_active[0], jnp.sum(present.astype(jnp.int32)),
                      jnp.argmax(present).astype(jnp.int32)]).astype(jnp.int32)
    plan = (tile_e, meta, tile_valid, next_e[tile_e], rank_e[tile_e])

    xs = _gather_rows(h2, row_tok)

    rows = _experts(xs, row_w, plan, w_gate_up, b_gate_up[:, None, :], w_down, b_down[:, None, :], tn_ff, tn_d)
    return _combine(rows, dest, x_mid)


def _alibi_slopes(n):
    return (LOG2E * 2.0 ** (-ALIBI_MAX_BIAS * (np.arange(n, dtype=np.float64) + 1.0) / n)).astype(np.float32)


def _mixer(x2d, batch, seq, norm1_g, w_in, q_norm_swa, k_norm_swa, q_norm_dil, k_norm_dil, sinks, w_out,
           n_swa, n_kv, n_dil, tm_proj, tn_out):
    t, d = x2d.shape
    swa_q, swa_kv, dil_w = n_swa * HEAD_DIM, n_kv * HEAD_DIM, n_dil * HEAD_DIM
    assert n_swa // n_kv == GQA_GROUP and seq % (16 * ATTN_BLOCK) == 0
    s1, s2, s3 = swa_q, swa_q + swa_kv, swa_q + 2 * swa_kv

    def dup(wc):
        wc = wc.reshape(d, n_kv, HEAD_DIM)
        return jnp.concatenate([wc, wc], axis=-1).reshape(d, 2 * swa_kv)

    w = jnp.concatenate([w_in[:, :s1], dup(w_in[:, s1:s2]), dup(w_in[:, s2:s3]), w_in[:, s3:]],
                        axis=1).astype(BF16)
    scale = HEAD_DIM ** -0.5 * LOG2E
    ones = lambda n: jnp.ones((n,), F32)
    gains = jnp.concatenate([
        jnp.tile(q_norm_swa * scale, n_swa), jnp.tile(k_norm_swa, 2 * n_kv), ones(2 * swa_kv),
        jnp.tile(q_norm_dil * scale, n_dil), jnp.tile(k_norm_dil, n_dil), ones(dil_w)])[None, :]
    c = w.shape[1]
    bounds = np.cumsum([0, swa_q, 2 * swa_kv, 2 * swa_kv, dil_w, dil_w, dil_w])
    assert all(b % PROJ_TN == 0 for b in bounds)
    seg_norm = [1, 1, 0, 1, 1, 0]
    flags = np.zeros((c // PROJ_TN,), np.int32)
    for sidx in range(6):
        flags[bounds[sidx] // PROJ_TN:bounds[sidx + 1] // PROJ_TN] = seg_norm[sidx]
    qa0, ka0, va0, qb0, kb0, vb0 = (int(b) for b in bounds[:6])
    proj, *residue_major = _inproj(x2d, norm1_g[None, :], w, gains, jnp.asarray(flags), tm_proj, dil_col0=qb0)
    by_dilation = dict(zip(RESIDUE_DILATIONS, residue_major))

    sink_row = jnp.repeat(sinks.astype(F32), HEAD_DIM)[None, :]
    o_a = _band_attention(proj[None], jnp.asarray(_alibi_slopes(n_swa)), batch=batch, seq=seq, dilation=1,
                          q_col0=qa0, k_col0=ka0, v_col0=va0, n_heads=n_swa, kv_shared=True,
                          max_diff=SWA_WINDOW - 1, sinks=sink_row)[0]
    slopes_dil = jnp.asarray(_alibi_slopes(n_dil))
    outs, lses = [], []
    for window, dil in DILATED_BRANCHES:
        src, col0 = (proj[None], qb0) if dil == 1 else (by_dilation[dil], 0)
        o_i, lse_i = _band_attention(src, slopes_dil, batch=batch, seq=seq, dilation=dil,
                                     q_col0=col0, k_col0=col0 + kb0 - qb0, v_col0=col0 + vb0 - qb0,
                                     n_heads=n_dil, kv_shared=False, max_diff=window // dil)
        outs.append(o_i)
        lses.append(lse_i)
    o_b = _merge(outs, lses, [dil for _, dil in DILATED_BRANCHES])
    return _outproj(o_a, o_b, w_out.astype(BF16), x2d, min(2 * tm_proj, t), tn_out)


def kernel(x, norm1_g, w_in, q_norm_swa, k_norm_swa, q_norm_dil, k_norm_dil, sinks, w_out, norm2_g,
           w_router, b_router, w_gate_up, b_gate_up, w_down, b_down):
    b, s, d = x.shape
    depth = norm1_g.shape[0]
    n_heads = d // HEAD_DIM
    n_swa = n_heads // 2
    n_kv = n_swa // GQA_GROUP
    n_dil = n_heads - n_swa
    x2d = x.reshape(b * s, d)
    for l in range(depth):
        x_mid = _mixer(x2d, b, s, norm1_g[l], w_in[l], q_norm_swa[l], k_norm_swa[l], q_norm_dil[l],
                       k_norm_dil[l], sinks[l], w_out[l], n_swa, n_kv, n_dil, tm_proj=512, tn_out=1024)
        x2d = _moe(x_mid, norm2_g[l][None, :], w_router[l], b_router[l], w_gate_up[l], b_gate_up[l],
                   w_down[l], b_down[l], tn_ff=512, tn_d=1024)
    return x2d.reshape(b, s, d)
```
